```python
import functools
import jax
import jax.numpy as jnp
from jax import lax
import numpy as np

D_MODEL = 1024
BATCH = 8
SEQ = 2048
DEPTH = 4

N_MIXERS = 2
N_MLA_LAYERS = (DEPTH + 1) // 2
N_GDN_LAYERS = DEPTH // 2
NORM_EPS = 1e-6
N_MOD = 6

MLA_HEADS = 8
QK_NOPE_DIM = 128
QK_ROPE_DIM = 64
V_HEAD_DIM = 128
Q_LORA_RANK = 384
KV_LORA_RANK = 256
MLA_IN_DIM = Q_LORA_RANK + KV_LORA_RANK + QK_ROPE_DIM
ROPE_THETA = 10000.0
ATTN_BLOCK = 128
MAX_POS_OFFSET = 4096

GDN_K_HEADS = 8
GDN_V_HEADS = 16
GDN_HEAD_DIM = 128
GDN_QK_DIM = GDN_K_HEADS * GDN_HEAD_DIM
GDN_V_DIM = GDN_V_HEADS * GDN_HEAD_DIM
GDN_CONV_DIM = 2 * GDN_QK_DIM + GDN_V_DIM
GDN_IN_DIM = GDN_CONV_DIM + GDN_V_DIM + 2 * GDN_V_HEADS
GDN_CONV = 4
GDN_CHUNK = 64

N_EXPERTS = 32
TOP_K = 4
D_FF_EXPERT = 1024
SWIGLU_LIMIT = 7.0
SWIGLU_ALPHA = 1.702
MOE_BLOCK = 256

kernel_name = 'hybrid_mla_gdn_moe_adaln'


def rmsnorm(x, w):
    xf = x.astype(jnp.float32)
    y = xf * lax.rsqrt(jnp.mean(xf * xf, axis=-1, keepdims=True) + NORM_EPS)
    return (y * w.astype(jnp.float32)).astype(x.dtype)


def modulate(h, shift, scale):
    return h * (1 + scale[:, None, :]) + shift[:, None, :]


def rope_tables(positions):
    inv = ROPE_THETA ** (-jnp.arange(0, QK_ROPE_DIM, 2, dtype=jnp.float32) / QK_ROPE_DIM)
    ang = positions.astype(jnp.float32)[..., None] * inv
    return jnp.cos(ang), jnp.sin(ang)


def apply_rope(x, cos, sin):
    half = x.shape[-1] // 2
    x1 = x[..., :half].astype(jnp.float32)
    x2 = x[..., half:].astype(jnp.float32)
    c = cos[:, :, None, :]
    s = sin[:, :, None, :]
    return jnp.concatenate([x1 * c - x2 * s, x2 * c + x1 * s], axis=-1).astype(x.dtype)


def mla(h, positions, w_in, q_norm_w, w_uq, kv_norm_w, w_ukv, w_o):
    B, S, _ = h.shape
    lat = h @ w_in
    q_lat = lat[..., :Q_LORA_RANK]
    kv_lat = lat[..., Q_LORA_RANK:Q_LORA_RANK + KV_LORA_RANK]
    k_pe = lat[..., Q_LORA_RANK + KV_LORA_RANK:]
    q = (rmsnorm(q_lat, q_norm_w) @ w_uq).reshape(B, S, MLA_HEADS, QK_NOPE_DIM + QK_ROPE_DIM)
    kv = (rmsnorm(kv_lat, kv_norm_w) @ w_ukv).reshape(B, S, MLA_HEADS, QK_NOPE_DIM + V_HEAD_DIM)
    cos, sin = rope_tables(positions)
    q_pe = apply_rope(q[..., QK_NOPE_DIM:], cos, sin)
    k_pe = apply_rope(k_pe[:, :, None, :], cos, sin)
    q = jnp.concatenate([q[..., :QK_NOPE_DIM], q_pe], axis=-1)
    k = jnp.concatenate([kv[..., :QK_NOPE_DIM],
                         jnp.broadcast_to(k_pe, (B, S, MLA_HEADS, QK_ROPE_DIM))], axis=-1)
    v = kv[..., QK_NOPE_DIM:]
    scale = (QK_NOPE_DIM + QK_ROPE_DIM) ** -0.5
    n_blk = S // ATTN_BLOCK
    q_blocks = q.reshape(B, n_blk, ATTN_BLOCK, MLA_HEADS, QK_NOPE_DIM + QK_ROPE_DIM).transpose(1, 0, 2, 3, 4)
    k_pos = jnp.arange(S)

    def attend(args):
        q_b, blk = args
        s = jnp.einsum('bqhd,bkhd->bhqk', q_b, k).astype(jnp.float32) * scale
        q_pos = blk * ATTN_BLOCK + jnp.arange(ATTN_BLOCK)
        s = jnp.where(k_pos[None, :] <= q_pos[:, None], s, -1e30)
        p = jax.nn.softmax(s, axis=-1).astype(v.dtype)
        return jnp.einsum('bhqk,bkhd->bqhd', p, v)

    o = lax.map(attend, (q_blocks, jnp.arange(n_blk)))
    o = o.transpose(1, 0, 2, 3, 4).reshape(B, S, MLA_HEADS * V_HEAD_DIM)
    return o @ w_o


def causal_conv_silu(x, w):
    y = lax.conv_general_dilated(x, w[:, None, :].astype(x.dtype), window_strides=(1,),
                                 padding=[(GDN_CONV - 1, 0)],
                                 dimension_numbers=('NWC', 'WIO', 'NWC'),
                                 feature_group_count=x.shape[-1])
    return jax.nn.silu(y)


def l2norm(x):
    return x * lax.rsqrt(jnp.sum(x * x, axis=-1, keepdims=True) + 1e-6)


def chunk_gated_delta_rule(q, k, v, g, beta):
    B, S, H, DK = q.shape
    DV = v.shape[-1]
    C = GDN_CHUNK
    N = S // C

    def chunks(t):
        return t.reshape(B, N, C, H, -1).transpose(0, 3, 1, 2, 4)

    q, k, v = chunks(q), chunks(k), chunks(v)
    g = g.reshape(B, N, C, H).transpose(0, 3, 1, 2)
    beta = beta.reshape(B, N, C, H).transpose(0, 3, 1, 2)
    gc = jnp.cumsum(g, axis=-1)
    causal = jnp.tril(jnp.ones((C, C), dtype=bool))
    strict = jnp.tril(jnp.ones((C, C), dtype=bool), -1)
    diff = gc[..., :, None] - gc[..., None, :]
    decay = jnp.where(causal, jnp.exp(jnp.where(causal, diff, 0.0)), 0.0)
    kb = k * beta[..., None]
    L = jnp.where(strict, jnp.einsum('bhncd,bhnjd->bhncj', kb, k) * decay, 0.0)
    T = L + jnp.eye(C, dtype=L.dtype)
    solve = functools.partial(lax.linalg.triangular_solve, left_side=True, lower=True, unit_diagonal=True)
    u = solve(T, v * beta[..., None])
    w = solve(T, kb * jnp.exp(gc)[..., None])
    a_intra = jnp.where(causal, jnp.einsum('bhncd,bhnjd->bhncj', q, k) * decay, 0.0)
    q_dec = q * jnp.exp(gc)[..., None]
    g_last = gc[..., -1]
    k_dec = k * jnp.exp(g_last[..., None] - gc)[..., None]

    def step(state, inp):
        q_i, k_i, u_i, w_i, a_i, gl_i = inp
        v_new = u_i - jnp.einsum('bhcd,bhde->bhce', w_i, state)
        o_i = jnp.einsum('bhcd,bhde->bhce', q_i, state) + jnp.einsum('bhcj,bhje->bhce', a_i, v_new)
        state = state * jnp.exp(gl_i)[..., None, None] + jnp.einsum('bhcd,bhce->bhde', k_i, v_new)
        return state, o_i

    lead = lambda t: jnp.moveaxis(t, 2, 0)
    state0 = jnp.zeros((B, H, DK, DV), jnp.float32)
    _, o = lax.scan(step, state0, (lead(q_dec), lead(k_dec), lead(u), lead(w), lead(a_intra), lead(g_last)))
    return o.transpose(1, 0, 3, 2, 4).reshape(B, S, H, DV)


def gated_deltanet(h, w_in, conv_w, a_log, dt_bias, norm_w, w_o):
    B, S, _ = h.shape
    f32 = jnp.float32
    proj = h @ w_in
    qkv = causal_conv_silu(proj[..., :GDN_CONV_DIM], conv_w)
    z = proj[..., GDN_CONV_DIM:GDN_CONV_DIM + GDN_V_DIM]
    b_logit = proj[..., GDN_CONV_DIM + GDN_V_DIM:GDN_CONV_DIM + GDN_V_DIM + GDN_V_HEADS]
    a_logit = proj[..., GDN_CONV_DIM + GDN_V_DIM + GDN_V_HEADS:]
    q = qkv[..., :GDN_QK_DIM].astype(f32).reshape(B, S, GDN_K_HEADS, GDN_HEAD_DIM)
    k = qkv[..., GDN_QK_DIM:2 * GDN_QK_DIM].astype(f32).reshape(B, S, GDN_K_HEADS, GDN_HEAD_DIM)
    v = qkv[..., 2 * GDN_QK_DIM:].astype(f32).reshape(B, S, GDN_V_HEADS, GDN_HEAD_DIM)
    rep = GDN_V_HEADS // GDN_K_HEADS
    q = jnp.repeat(l2norm(q) * GDN_HEAD_DIM ** -0.5, rep, axis=2)
    k = jnp.repeat(l2norm(k), rep, axis=2)
    beta = jax.nn.sigmoid(b_logit.astype(f32))
    g = -jnp.exp(a_log.astype(f32)) * jax.nn.softplus(a_logit.astype(f32) + dt_bias.astype(f32))
    o = chunk_gated_delta_rule(q, k, v, g, beta).astype(h.dtype)
    o = rmsnorm(o, norm_w) * jax.nn.silu(z.reshape(B, S, GDN_V_HEADS, GDN_HEAD_DIM))
    return o.reshape(B, S, GDN_V_DIM) @ w_o


def clamped_swiglu(gu):
    gate = jnp.minimum(gu[..., :D_FF_EXPERT], SWIGLU_LIMIT)
    lin = jnp.clip(gu[..., D_FF_EXPERT:], -SWIGLU_LIMIT, SWIGLU_LIMIT)
    return gate * jax.nn.sigmoid(SWIGLU_ALPHA * gate) * (lin + 1)


def moe(h, router_w, router_b, w_gate_up, b_gate_up, w_down, b_down):
    B, S, D = h.shape
    T = B * S
    xf = h.reshape(T, D)
    logits = (xf @ router_w + router_b).astype(jnp.float32)
    top_logit, top_e = lax.top_k(logits, TOP_K)
    gates = jax.nn.softmax(top_logit, axis=-1)
    A = T * TOP_K
    flat_e = top_e.reshape(A)
    order = jnp.argsort(flat_e)
    e_sorted = flat_e[order]
    tok_sorted = order // TOP_K
    gate_sorted = gates.reshape(A)[order]
    counts = jnp.bincount(flat_e, length=N_EXPERTS)
    start = jnp.cumsum(counts) - counts
    padded = (counts + MOE_BLOCK - 1) // MOE_BLOCK * MOE_BLOCK
    pad_end = jnp.cumsum(padded)
    pad_start = pad_end - padded
    dest = pad_start[e_sorted] + (jnp.arange(A) - start[e_sorted])
    n_blocks = -(-A // MOE_BLOCK) + N_EXPERTS
    n_slots = n_blocks * MOE_BLOCK
    slot_tok = jnp.full((n_slots,), T, dtype=jnp.int32).at[dest].set(tok_sorted.astype(jnp.int32))
    x_pad = jnp.concatenate([xf, jnp.zeros((1, D), xf.dtype)], axis=0)
    x_blocks = x_pad[slot_tok].reshape(n_blocks, MOE_BLOCK, D)
    blk_e = jnp.minimum(jnp.searchsorted(pad_end, jnp.arange(n_blocks) * MOE_BLOCK, side='right'), N_EXPERTS - 1)

    def expert_block(args):
        xb, e = args
        gu = xb @ w_gate_up[e] + b_gate_up[e]
        return clamped_swiglu(gu) @ w_down[e] + b_down[e]

    y_blocks = lax.map(expert_block, (x_blocks, blk_e))
    y = y_blocks.reshape(n_slots, D)[dest] * gate_sorted[:, None].astype(h.dtype)
    out = jnp.zeros((T, D), y.dtype).at[tok_sorted].add(y)
    return out.reshape(B, S, D)


def setup_inputs(seed: int = 0) -> dict:
    key = jax.random.key(seed)
    ks = jax.random.split(key, 26)
    f32 = jnp.float32

    def nrm(k, shape, fan_in, gain=1.0):
        return jax.random.normal(k, shape, f32) * (gain * fan_in ** -0.5)

    def gain_vec(k, shape):
        return 1.0 + 0.02 * jax.random.normal(k, shape, f32)

    x = jax.random.normal(ks[0], (BATCH, SEQ, D_MODEL), f32)
    c = jax.random.normal(ks[1], (BATCH, D_MODEL), f32)
    positions = (jax.random.randint(ks[2], (BATCH, 1), 0, MAX_POS_OFFSET, dtype=jnp.int32)
                 + jnp.arange(SEQ, dtype=jnp.int32)[None, :])
    norm_mix = gain_vec(ks[3], (DEPTH, D_MODEL))
    norm_ffn = gain_vec(ks[4], (DEPTH, D_MODEL))
    ada_w = nrm(ks[5], (DEPTH, D_MODEL, N_MOD * D_MODEL), D_MODEL, 0.5)
    ada_b = 0.02 * jax.random.normal(ks[6], (DEPTH, N_MOD * D_MODEL), f32)
    mla_w_in = nrm(ks[7], (N_MLA_LAYERS, D_MODEL, MLA_IN_DIM), D_MODEL)
    mla_q_norm = gain_vec(ks[8], (N_MLA_LAYERS, Q_LORA_RANK))
    mla_w_uq = nrm(ks[9], (N_MLA_LAYERS, Q_LORA_RANK, MLA_HEADS * (QK_NOPE_DIM + QK_ROPE_DIM)), Q_LORA_RANK)
    mla_kv_norm = gain_vec(ks[10], (N_MLA_LAYERS, KV_LORA_RANK))
    mla_w_ukv = nrm(ks[11], (N_MLA_LAYERS, KV_LORA_RANK, MLA_HEADS * (QK_NOPE_DIM + V_HEAD_DIM)), KV_LORA_RANK)
    mla_w_o = nrm(ks[12], (N_MLA_LAYERS, MLA_HEADS * V_HEAD_DIM, D_MODEL), MLA_HEADS * V_HEAD_DIM)
    gdn_w_in = nrm(ks[13], (N_GDN_LAYERS, D_MODEL, GDN_IN_DIM), D_MODEL)
    gdn_conv_w = nrm(ks[14], (N_GDN_LAYERS, GDN_CONV, GDN_CONV_DIM), GDN_CONV)
    gdn_a_log = jnp.log(jax.random.uniform(ks[15], (N_GDN_LAYERS, GDN_V_HEADS), f32, 1.0, 16.0))
    dt = jnp.exp(jax.random.uniform(ks[16], (N_GDN_LAYERS, GDN_V_HEADS), f32, np.log(1e-3), np.log(1e-1)))
    gdn_dt_bias = dt + jnp.log(-jnp.expm1(-dt))
    gdn_norm_w = gain_vec(ks[17], (N_GDN_LAYERS, GDN_HEAD_DIM))
    gdn_w_o = nrm(ks[18], (N_GDN_LAYERS, GDN_V_DIM, D_MODEL), GDN_V_DIM)
    router_w = nrm(ks[19], (DEPTH, D_MODEL, N_EXPERTS), D_MODEL)
    router_b = 0.01 * jax.random.normal(ks[20], (DEPTH, N_EXPERTS), f32)
    moe_w_gate_up = nrm(ks[21], (DEPTH, N_EXPERTS, D_MODEL, 2 * D_FF_EXPERT), D_MODEL)
    moe_b_gate_up = 0.01 * jax.random.normal(ks[22], (DEPTH, N_EXPERTS, 2 * D_FF_EXPERT), f32)
    moe_w_down = nrm(ks[23], (DEPTH, N_EXPERTS, D_FF_EXPERT, D_MODEL), D_FF_EXPERT)
    moe_b_down = 0.01 * jax.random.normal(ks[24], (DEPTH, N_EXPERTS, D_MODEL), f32)
    final_norm = gain_vec(ks[25], (D_MODEL,))
    return {'x': x, 'c': c, 'positions': positions, 'norm_mix': norm_mix, 'norm_ffn': norm_ffn,
            'ada_w': ada_w, 'ada_b': ada_b, 'mla_w_in': mla_w_in, 'mla_q_norm': mla_q_norm,
            'mla_w_uq': mla_w_uq, 'mla_kv_norm': mla_kv_norm, 'mla_w_ukv': mla_w_ukv, 'mla_w_o': mla_w_o,
            'gdn_w_in': gdn_w_in, 'gdn_conv_w': gdn_conv_w, 'gdn_a_log': gdn_a_log,
            'gdn_dt_bias': gdn_dt_bias, 'gdn_norm_w': gdn_norm_w, 'gdn_w_o': gdn_w_o,
            'router_w': router_w, 'router_b': router_b, 'moe_w_gate_up': moe_w_gate_up,
            'moe_b_gate_up': moe_b_gate_up, 'moe_w_down': moe_w_down, 'moe_b_down': moe_b_down,
            'final_norm': final_norm}


def reference(x, c, positions, norm_mix, norm_ffn, ada_w, ada_b, mla_w_in, mla_q_norm, mla_w_uq,
              mla_kv_norm, mla_w_ukv, mla_w_o, gdn_w_in, gdn_conv_w, gdn_a_log, gdn_dt_bias, gdn_norm_w,
              gdn_w_o, router_w, router_b, moe_w_gate_up, moe_b_gate_up, moe_w_down, moe_b_down, final_norm):
    B = x.shape[0]
    cond = jax.nn.silu(c)
    for layer in range(DEPTH):
        mod = (cond @ ada_w[layer] + ada_b[layer]).reshape(B, N_MOD, D_MODEL)
        h = modulate(rmsnorm(x, norm_mix[layer]), mod[:, 0], mod[:, 1])
        j = layer // N_MIXERS
        if layer % N_MIXERS == 0:
            y = mla(h, positions, mla_w_in[j], mla_q_norm[j], mla_w_uq[j], mla_kv_norm[j], mla_w_ukv[j], mla_w_o[j])
        else:
            y = gated_deltanet(h, gdn_w_in[j], gdn_conv_w[j], gdn_a_log[j], gdn_dt_bias[j], gdn_norm_w[j], gdn_w_o[j])
        x = x + mod[:, 2][:, None, :] * y
        h = modulate(rmsnorm(x, norm_ffn[layer]), mod[:, 3], mod[:, 4])
        y = moe(h, router_w[layer], router_b[layer], moe_w_gate_up[layer], moe_b_gate_up[layer],
                moe_w_down[layer], moe_b_down[layer])
        x = x + mod[:, 5][:, None, :] * y
    return rmsnorm(x, final_norm)
```

```python
import functools
import math

import jax
import jax.numpy as jnp
from jax import lax
from jax.experimental import pallas as pl
from jax.experimental.pallas import tpu as pltpu

F32, BF16, I32, U32 = jnp.float32, jnp.bfloat16, jnp.int32, jnp.uint32

NORM_EPS = 1e-6
N_MOD = 6
MLA_HEADS = 8
QK_NOPE_DIM = 128
QK_ROPE_DIM = 64
V_HEAD_DIM = 128
Q_LORA_RANK = 384
KV_LORA_RANK = 256
ROPE_THETA = 10000.0
GDN_K_HEADS = 8
GDN_V_HEADS = 16
GDN_HEAD_DIM = 128
GDN_CONV = 4
GDN_CHUNK = 64
N_EXPERTS = 32
TOP_K = 4
SWIGLU_LIMIT = 7.0
SWIGLU_ALPHA = 1.702

LANES = 128
SUBLANES = 8
VMEM_LIMIT = 56 * 1024 * 1024

TOKEN_TILE = 256
ATTN_TILE = 256
MOE_TILE = 256
GDN_SUPER = 4 * GDN_CHUNK
INV_LANES = 256
NEG_BIG = -1e30


def _cparams(*sem):
    return pltpu.CompilerParams(dimension_semantics=sem, vmem_limit_bytes=VMEM_LIMIT)


def _sigmoid(x):
    return 1.0 / (1.0 + jnp.exp(-x))


def _rms(x, w):
    return x * lax.rsqrt(jnp.mean(x * x, axis=-1, keepdims=True) + NORM_EPS) * w


def _norm_mod(x, w, shift, scale):
    return _rms(x, w) * (1.0 + scale) + shift


def _dot(a, b):
    return jnp.dot(a, b, preferred_element_type=F32)


def _dot_nt(a, b):
    return lax.dot_general(a, b, (((1,), (1,)), ((), ())), preferred_element_type=F32)


def _dot_f32(a, b):
    return jnp.dot(a, b, preferred_element_type=F32, precision=lax.Precision.HIGHEST)


def _pack_pair(lo, hi):
    ulo = lax.bitcast_convert_type(lo.astype(BF16).astype(F32), U32) >> 16
    uhi = lax.bitcast_convert_type(hi.astype(BF16).astype(F32), U32) & jnp.uint32(0xFFFF0000)
    return ulo | uhi


def _unpack_pair(p):
    lo = lax.bitcast_convert_type(p << 16, F32)
    hi = lax.bitcast_convert_type(p & jnp.uint32(0xFFFF0000), F32)
    return lo, hi


def _mod_kernel(c_ref, w_ref, b_ref, o_ref):
    c = c_ref[...]
    cond = c * _sigmoid(c)
    o_ref[0] = _dot(cond.astype(BF16), w_ref[0].astype(BF16)) + b_ref[0]


def _adaln_mod(c, ada_w, ada_b):
    depth, d, n = ada_w.shape
    b = c.shape[0]
    tn = 1024
    out = pl.pallas_call(
        _mod_kernel,
        grid=(depth, n // tn),
        in_specs=[pl.BlockSpec((b, d), lambda l, j: (0, 0)),
                  pl.BlockSpec((1, d, tn), lambda l, j: (l, 0, j)),
                  pl.BlockSpec((1, 1, tn), lambda l, j: (l, 0, j))],
        out_specs=pl.BlockSpec((1, b, tn), lambda l, j: (l, 0, j)),
        out_shape=jax.ShapeDtypeStruct((depth, b, n), F32),
        compiler_params=_cparams("arbitrary", "arbitrary"),
        name="adaln_mod",
    )(c, ada_w, ada_b.reshape(depth, 1, n))
    return out.reshape(depth, b, N_MOD, d)


def _rope_kernel(pos_ref, cos_ref, sin_ref):
    pos = pos_ref[...].astype(F32)
    lane = lax.broadcasted_iota(I32, (1, LANES), 1)
    j = (lane & (QK_ROPE_DIM // 2 - 1)).astype(F32)
    inv = jnp.exp(j * (-2.0 / QK_ROPE_DIM * math.log(ROPE_THETA)))
    ang = pos * inv
    cos_ref[...] = jnp.cos(ang)
    sin_ref[...] = jnp.sin(ang)


def _rope_tables(positions):
    t = positions.size
    tm = 1024
    spec = pl.BlockSpec((tm, LANES), lambda i: (i, 0))
    return pl.pallas_call(
        _rope_kernel,
        grid=(t // tm,),
        in_specs=[pl.BlockSpec((tm, 1), lambda i: (i, 0))],
        out_specs=[spec, spec],
        out_shape=[jax.ShapeDtypeStruct((t, LANES), F32)] * 2,
        compiler_params=_cparams("arbitrary"),
        name="rope_tables",
    )(positions.reshape(t, 1))


def _mla_proj_kernel(x_ref, mod_ref, nw_ref, win_ref, qn_ref, wqa_ref, wqb_ref, kvn_ref, wkv_ref,
                     cos_ref, sin_ref, q_ref, k_ref, v_ref):
    mod = mod_ref[0]
    h = _norm_mod(x_ref[...], nw_ref[...], mod[0:1], mod[1:2])
    lat = _dot(h.astype(BF16), win_ref[...])
    q_lat = lat[:, :Q_LORA_RANK]
    kv_lat = lat[:, Q_LORA_RANK:Q_LORA_RANK + KV_LORA_RANK]
    kp = lat[:, Q_LORA_RANK + KV_LORA_RANK:]
    cos = cos_ref[...]
    sin = sin_ref[...]
    scale = (QK_NOPE_DIM + QK_ROPE_DIM) ** -0.5

    qn = _rms(q_lat, qn_ref[...]).astype(BF16)
    qa = _dot(qn, wqa_ref[...])
    qb = _dot(qn, wqb_ref[...])
    for hh in range(MLA_HEADS):
        o = hh * 2 * LANES
        q_ref[:, o:o + LANES] = (qa[:, o:o + LANES] * scale).astype(BF16)
        pe = qa[:, o + LANES:o + 2 * LANES] * cos + qb[:, hh * LANES:(hh + 1) * LANES] * sin
        q_ref[:, o + LANES:o + 2 * LANES] = (pe * scale).astype(BF16)

    kvn = _rms(kv_lat, kvn_ref[...]).astype(BF16)
    kv = _dot(kvn, wkv_ref[...])
    lane = lax.broadcasted_iota(I32, (1, LANES), 1)
    first = lane < QK_ROPE_DIM
    u = kp * jnp.where(first, cos, sin)
    kr = jnp.where(first, u + pltpu.roll(u, QK_ROPE_DIM, 1), 0.0).astype(BF16)
    nk = MLA_HEADS * QK_NOPE_DIM
    for hh in range(MLA_HEADS):
        o = hh * 2 * LANES
        k_ref[:, o:o + LANES] = kv[:, hh * LANES:(hh + 1) * LANES].astype(BF16)
        k_ref[:, o + LANES:o + 2 * LANES] = kr
    v_ref[...] = kv[:, nk:].astype(BF16)


def _rotate_half_cols(w):
    half = w.shape[-1] // 2
    return jnp.concatenate([-w[..., half:], w[..., :half]], axis=-1)


def _mla_weights(w_in, w_uq, w_ukv):
    d = w_in.shape[0]
    kpe = w_in[:, Q_LORA_RANK + KV_LORA_RANK:]
    w_in_ext = jnp.concatenate([w_in, _rotate_half_cols(kpe)], axis=1).astype(BF16)
    wq = w_uq.reshape(Q_LORA_RANK, MLA_HEADS, QK_NOPE_DIM + QK_ROPE_DIM)
    zeros = jnp.zeros((Q_LORA_RANK, MLA_HEADS, LANES - QK_ROPE_DIM), w_uq.dtype)
    wq_pe = wq[:, :, QK_NOPE_DIM:]
    wqa = jnp.concatenate([wq[:, :, :QK_NOPE_DIM], wq_pe, zeros], axis=2)
    wqb = jnp.concatenate([_rotate_half_cols(wq_pe), zeros], axis=2)
    wkv = w_ukv.reshape(KV_LORA_RANK, MLA_HEADS, QK_NOPE_DIM + V_HEAD_DIM)
    wkv = jnp.concatenate([wkv[:, :, :QK_NOPE_DIM].reshape(KV_LORA_RANK, -1),
                           wkv[:, :, QK_NOPE_DIM:].reshape(KV_LORA_RANK, -1)], axis=1)
    del d
    return (w_in_ext, wqa.reshape(Q_LORA_RANK, -1).astype(BF16),
            wqb.reshape(Q_LORA_RANK, -1).astype(BF16), wkv.astype(BF16))


def _mla_proj(x, mod, norm_w, weights, q_norm, kv_norm, cos, sin, seq):
    t, d = x.shape
    tm = TOKEN_TILE
    w_in_ext, wqa, wqb, wkv = weights
    per_seq = seq // tm
    full = lambda a: pl.BlockSpec(a.shape, lambda i: (0,) * a.ndim)
    row = lambda n: pl.BlockSpec((tm, n), lambda i: (i, 0))
    nq = MLA_HEADS * 2 * LANES
    nv = MLA_HEADS * V_HEAD_DIM
    args = (x, mod, norm_w.reshape(1, d), w_in_ext, q_norm.reshape(1, -1), wqa, wqb,
            kv_norm.reshape(1, -1), wkv, cos, sin)
    in_specs = [row(d), pl.BlockSpec((1, N_MOD, d), lambda i: (i // per_seq, 0, 0)), full(args[2]),
                full(w_in_ext), full(args[4]), full(wqa), full(wqb), full(args[7]), full(wkv),
                row(LANES), row(LANES)]
    return pl.pallas_call(
        _mla_proj_kernel,
        grid=(t // tm,),
        in_specs=in_specs,
        out_specs=[row(nq), row(nq), row(nv)],
        out_shape=[jax.ShapeDtypeStruct((t, nq), BF16), jax.ShapeDtypeStruct((t, nq), BF16),
                   jax.ShapeDtypeStruct((t, nv), BF16)],
        compiler_params=_cparams("arbitrary"),
        name="mla_proj",
    )(*args)


def _attn_kernel(q_ref, k_ref, v_ref, o_ref):
    tq = q_ref.shape[0]
    qi = pl.program_id(2)
    q = q_ref[...]

    def step(kb, vb, carry, mask):
        m, l, acc = carry
        s = _dot_nt(q, kb)
        if mask is not None:
            s = jnp.where(mask, s, NEG_BIG)
        m_new = jnp.maximum(m, jnp.max(s, axis=-1, keepdims=True))
        p = jnp.exp(s - m_new)
        alpha = jnp.exp(m - m_new)
        l = alpha * l + jnp.sum(p, axis=-1, keepdims=True)
        acc = alpha * acc + _dot(p.astype(BF16), vb)
        return m_new, l, acc

    def body(j, carry):
        r0 = pl.multiple_of(j * tq, tq)
        return step(k_ref[pl.ds(r0, tq), :], v_ref[pl.ds(r0, tq), :], carry, None)

    init = (jnp.full((tq, 1), NEG_BIG, F32), jnp.zeros((tq, 1), F32),
            jnp.zeros((tq, v_ref.shape[1]), F32))
    carry = lax.fori_loop(0, qi, body, init)
    r0 = pl.multiple_of(qi * tq, tq)
    causal = (lax.broadcasted_iota(I32, (tq, tq), 1) <= lax.broadcasted_iota(I32, (tq, tq), 0))
    _, l, acc = step(k_ref[pl.ds(r0, tq), :], v_ref[pl.ds(r0, tq), :], carry, causal)
    o_ref[...] = (acc / l).astype(o_ref.dtype)


def _attention(q, k, v, batch, seq):
    t = q.shape[0]
    tq = ATTN_TILE
    nq = seq // tq
    dq = 2 * LANES
    return pl.pallas_call(
        _attn_kernel,
        grid=(batch, MLA_HEADS, nq),
        in_specs=[pl.BlockSpec((tq, dq), lambda b, h, i: (b * nq + i, h)),
                  pl.BlockSpec((seq, dq), lambda b, h, i: (b, h)),
                  pl.BlockSpec((seq, V_HEAD_DIM), lambda b, h, i: (b, h))],
        out_specs=pl.BlockSpec((tq, V_HEAD_DIM), lambda b, h, i: (b * nq + i, h)),
        out_shape=jax.ShapeDtypeStruct((t, MLA_HEADS * V_HEAD_DIM), BF16),
        compiler_params=_cparams("arbitrary", "arbitrary", "arbitrary"),
        name="mla_attention",
    )(q, k, v)


def _post_mixer_kernel(o_ref, x_ref, mod_ref, wo_ref, nw_ref, rw_ref, rb_ref,
                       x1_ref, hp_ref, meta_ref, gate_ref, cnt_ref, run_ref):
    tm, d = x_ref.shape
    i = pl.program_id(0)

    @pl.when(i == 0)
    def _():
        run_ref[...] = jnp.zeros_like(run_ref)

    mod = mod_ref[0]
    x1 = x_ref[...] + mod[2:3] * _dot(o_ref[...], wo_ref[...])
    x1_ref[...] = x1
    h = _norm_mod(x1, nw_ref[...], mod[3:4], mod[4:5])
    hp_ref[...] = _pack_pair(h[:, :d // 2], h[:, d // 2:])

    logits = _dot_f32(h, rw_ref[...]) + rb_ref[...]
    lane = lax.broadcasted_iota(I32, (tm, LANES), 1)
    lane_f = lane.astype(F32)
    work = logits
    idx, val, hot = [], [], []
    for _ in range(TOP_K):
        m = jnp.max(work, axis=-1, keepdims=True)
        a = jnp.min(jnp.where(work == m, lane_f, float(LANES)), axis=-1, keepdims=True)
        hot.append(lane_f == a)
        idx.append(a.astype(I32))
        val.append(m)
        work = jnp.where(hot[-1], -jnp.inf, work)
    ex = [jnp.exp(v - val[0]) for v in val]
    den = ex[0] + ex[1] + ex[2] + ex[3]

    onehot = jnp.where(hot[0] | hot[1] | hot[2] | hot[3], 1.0, 0.0)
    row = lax.broadcasted_iota(I32, (tm, tm), 0)
    col = lax.broadcasted_iota(I32, (tm, tm), 1)
    tri = jnp.where(col < row, 1.0, 0.0).astype(BF16)
    slot = run_ref[...] + _dot(tri, onehot.astype(BF16))

    meta = jnp.zeros((tm, LANES), I32)
    gates = jnp.zeros((tm, LANES), F32)
    for kk in range(TOP_K):
        pos = jnp.sum(jnp.where(hot[kk], slot, 0.0), axis=-1, keepdims=True).astype(I32)
        meta = jnp.where(lane == kk, idx[kk], meta)
        meta = jnp.where(lane == TOP_K + kk, pos, meta)
        gates = jnp.where(lane == kk, ex[kk] / den, gates)
    meta_ref[...] = meta
    gate_ref[...] = gates
    run = run_ref[...] + jnp.sum(onehot, axis=0, keepdims=True)
    run_ref[...] = run
    cnt_ref[...] = run.astype(I32)


def _post_mixer(o, x, mod, w_o, norm_w, router_w, router_b, seq):
    t, d = x.shape
    ko = o.shape[1]
    tm = TOKEN_TILE
    per_seq = seq // tm
    rw = jnp.zeros((d, LANES), F32).at[:, :N_EXPERTS].set(router_w)
    rb = jnp.full((1, LANES), NEG_BIG, F32).at[0, :N_EXPERTS].set(router_b)
    row = lambda n: pl.BlockSpec((tm, n), lambda i: (i, 0))
    const = lambda r, c: pl.BlockSpec((r, c), lambda i: (0, 0))
    return pl.pallas_call(
        _post_mixer_kernel,
        grid=(t // tm,),
        in_specs=[row(ko), row(d), pl.BlockSpec((1, N_MOD, d), lambda i: (i // per_seq, 0, 0)),
                  const(ko, d), const(1, d), const(d, LANES), const(1, LANES)],
        out_specs=[row(d), row(d // 2), row(LANES), row(LANES), const(1, LANES)],
        out_shape=[jax.ShapeDtypeStruct((t, d), F32), jax.ShapeDtypeStruct((t, d // 2), U32),
                   jax.ShapeDtypeStruct((t, LANES), I32), jax.ShapeDtypeStruct((t, LANES), F32),
                   jax.ShapeDtypeStruct((1, LANES), I32)],
        scratch_shapes=[pltpu.VMEM((1, LANES), F32)],
        compiler_params=_cparams("arbitrary"),
        name="post_mixer_router",
    )(o, x, mod, w_o.astype(BF16), norm_w.reshape(1, d), rw, rb)


def _expert_starts(counts_ref, starts_ref):
    def body(e, acc):
        starts_ref[e] = acc
        return acc + counts_ref[e]
    lax.fori_loop(0, N_EXPERTS, body, jnp.int32(0))


def _dispatch_kernel(eidx_ref, pos_ref, counts_ref, h_ref, xs_ref, starts_ref, sem):
    tm = h_ref.shape[0]

    @pl.when(pl.program_id(0) == 0)
    def _():
        _expert_starts(counts_ref, starts_ref)

    def issue(tok, carry):
        for kk in range(TOP_K):
            a = tok * TOP_K + kk
            dst = starts_ref[eidx_ref[a]] + pos_ref[a]
            pltpu.make_async_copy(h_ref.at[pl.ds(tok, 1), :], xs_ref.at[pl.ds(dst, 1), :], sem).start()
        return carry
    lax.fori_loop(0, tm, issue, 0)

    def drain(a, carry):
        pltpu.make_async_copy(h_ref.at[pl.ds(0, 1), :], xs_ref.at[pl.ds(0, 1), :], sem).wait()
        return carry
    lax.fori_loop(0, tm * TOP_K, drain, 0)


def _dispatch(hp, eidx, pos, counts):
    t, w = hp.shape
    tm = TOKEN_TILE
    smem = lambda n: pl.BlockSpec((n,), lambda i: (i,), memory_space=pltpu.SMEM)
    return pl.pallas_call(
        _dispatch_kernel,
        grid=(t // tm,),
        in_specs=[smem(tm * TOP_K), smem(tm * TOP_K),
                  pl.BlockSpec(memory_space=pltpu.SMEM),
                  pl.BlockSpec((tm, w), lambda i: (i, 0))],
        out_specs=pl.BlockSpec(memory_space=pl.ANY),
        out_shape=jax.ShapeDtypeStruct((t * TOP_K, w), U32),
        scratch_shapes=[pltpu.SMEM((N_EXPERTS,), I32), pltpu.SemaphoreType.DMA(())],
        compiler_params=_cparams("arbitrary"),
        name="moe_dispatch",
    )(eidx, pos, counts, hp)


def _moe_schedule(counts, n_tiles, tile):
    n_visits = n_tiles + N_EXPERTS - 1
    ends = jnp.cumsum(counts)
    starts = ends - counts
    first_tile = starts // tile
    n_vis = jnp.where(counts > 0, (ends - 1) // tile - first_tile + 1, 0)
    vis_end = jnp.cumsum(n_vis)
    vis_start = vis_end - n_vis
    total = vis_end[-1]
    v = jnp.minimum(jnp.arange(n_visits, dtype=I32), total - 1)
    e = jnp.minimum(jnp.searchsorted(vis_end, v, side="right"), N_EXPERTS - 1).astype(I32)
    tile_id = (first_tile[e] + v - vis_start[e]).astype(I32)
    lo = jnp.maximum(starts[e] - tile_id * tile, 0).astype(I32)
    hi = jnp.minimum(ends[e] - tile_id * tile, tile).astype(I32)
    live = jnp.arange(n_visits, dtype=I32) < total
    prev_tile = jnp.concatenate([jnp.full((1,), -1, I32), tile_id[:-1]])
    new_expert = (live & (v == vis_start[e])).astype(I32)
    new_tile = (live & (tile_id != prev_tile)).astype(I32)
    return tile_id, e, lo, hi, new_expert, new_tile, live.astype(I32)


def _moe_kernel(tile_ref, exp_ref, lo_ref, hi_ref, newe_ref, newt_ref, live_ref,
                x_ref, wgu_ref, bgu_ref, wd_ref, bd_ref, y_ref, wgu_bf, wd_bf):
    v = pl.program_id(0)
    f = wd_ref.shape[1]
    half = x_ref.shape[1]

    @pl.when(newe_ref[v] == 1)
    def _():
        wgu_bf[...] = wgu_ref[0].astype(BF16)
        wd_bf[...] = wd_ref[0].astype(BF16)

    def compute():
        x_lo, x_hi = _unpack_pair(x_ref[...])
        gu = (_dot(x_lo.astype(BF16), wgu_bf[:half, :]) + _dot(x_hi.astype(BF16), wgu_bf[half:, :])
              + bgu_ref[0])
        gate = jnp.minimum(gu[:, :f], SWIGLU_LIMIT)
        lin = jnp.clip(gu[:, f:], -SWIGLU_LIMIT, SWIGLU_LIMIT)
        act = gate * _sigmoid(SWIGLU_ALPHA * gate) * (lin + 1.0)
        y = _dot(act.astype(BF16), wd_bf[...]) + bd_ref[0]
        return _pack_pair(y[:, :half], y[:, half:])

    @pl.when((live_ref[v] == 1) & (newt_ref[v] == 1))
    def _():
        y_ref[...] = compute()

    @pl.when((live_ref[v] == 1) & (newt_ref[v] == 0))
    def _():
        rows = lax.broadcasted_iota(I32, y_ref.shape, 0)
        mine = (rows >= lo_ref[v]) & (rows < hi_ref[v])
        y_ref[...] = jnp.where(mine, compute(), y_ref[...])


def _moe_experts(xs, sched, w_gate_up, b_gate_up, w_down, b_down):
    a, half = xs.shape
    e, d, f2 = w_gate_up.shape
    f = f2 // 2
    tile = MOE_TILE
    n_visits = a // tile + N_EXPERTS - 1
    grid_spec = pltpu.PrefetchScalarGridSpec(
        num_scalar_prefetch=7,
        grid=(n_visits,),
        in_specs=[pl.BlockSpec((tile, half), lambda v, t, ex, *_: (t[v], 0)),
                  pl.BlockSpec((1, d, f2), lambda v, t, ex, *_: (ex[v], 0, 0)),
                  pl.BlockSpec((1, 1, f2), lambda v, t, ex, *_: (ex[v], 0, 0)),
                  pl.BlockSpec((1, f, d), lambda v, t, ex, *_: (ex[v], 0, 0)),
                  pl.BlockSpec((1, 1, d), lambda v, t, ex, *_: (ex[v], 0, 0))],
        out_specs=pl.BlockSpec((tile, half), lambda v, t, ex, *_: (t[v], 0)),
        scratch_shapes=[pltpu.VMEM((d, f2), BF16), pltpu.VMEM((f, d), BF16)],
    )
    return pl.pallas_call(
        _moe_kernel,
        grid_spec=grid_spec,
        out_shape=jax.ShapeDtypeStruct((a, half), U32),
        compiler_params=_cparams("arbitrary"),
        name="moe_experts",
    )(*sched, xs, w_gate_up, b_gate_up.reshape(e, 1, f2), w_down, b_down.reshape(e, 1, d))


def _combine_kernel(eidx_ref, pos_ref, counts_ref, ys_ref, gate_ref, x_ref, mod_ref, fw_ref,
                    o_ref, buf_ref, starts_ref, sem, *, final_norm):
    tm, d = x_ref.shape

    @pl.when(pl.program_id(0) == 0)
    def _():
        _expert_starts(counts_ref, starts_ref)

    def issue(tok, carry):
        for kk in range(TOP_K):
            a = tok * TOP_K + kk
            src = starts_ref[eidx_ref[a]] + pos_ref[a]
            pltpu.make_async_copy(ys_ref.at[pl.ds(src, 1), :], buf_ref.at[kk, pl.ds(tok, 1), :], sem).start()
        return carry
    lax.fori_loop(0, tm, issue, 0)

    def drain(a, carry):
        pltpu.make_async_copy(ys_ref.at[pl.ds(0, 1), :], buf_ref.at[0, pl.ds(0, 1), :], sem).wait()
        return carry
    lax.fori_loop(0, tm * TOP_K, drain, 0)

    gates = gate_ref[...]
    y_lo = jnp.zeros((tm, d // 2), F32)
    y_hi = jnp.zeros((tm, d // 2), F32)
    for kk in range(TOP_K):
        lo, hi = _unpack_pair(buf_ref[kk])
        g = gates[:, kk:kk + 1]
        y_lo = y_lo + g * lo
        y_hi = y_hi + g * hi
    y = jnp.concatenate([y_lo, y_hi], axis=1)
    out = x_ref[...] + mod_ref[0][5:6] * y
    if final_norm:
        out = _rms(out, fw_ref[...])
    o_ref[...] = out


def _combine(ys, eidx, pos, counts, gates, x1, mod, final_w, seq, final_norm):
    t, d = x1.shape
    tm = TOKEN_TILE
    per_seq = seq // tm
    smem = lambda n: pl.BlockSpec((n,), lambda i: (i,), memory_space=pltpu.SMEM)
    row = lambda n: pl.BlockSpec((tm, n), lambda i: (i, 0))
    return pl.pallas_call(
        functools.partial(_combine_kernel, final_norm=final_norm),
        grid=(t // tm,),
        in_specs=[smem(tm * TOP_K), smem(tm * TOP_K), pl.BlockSpec(memory_space=pltpu.SMEM),
                  pl.BlockSpec(memory_space=pl.ANY), row(LANES), row(d),
                  pl.BlockSpec((1, N_MOD, d), lambda i: (i // per_seq, 0, 0)),
                  pl.BlockSpec((1, d), lambda i: (0, 0))],
        out_specs=row(d),
        out_shape=jax.ShapeDtypeStruct((t, d), F32),
        scratch_shapes=[pltpu.VMEM((TOP_K, tm, d // 2), U32), pltpu.SMEM((N_EXPERTS,), I32),
                        pltpu.SemaphoreType.DMA(())],
        compiler_params=_cparams("arbitrary"),
        name="moe_combine",
    )(eidx, pos, counts, ys, gates, x1, mod, final_w.reshape(1, d))


def _moe_block(o, x, mod, w_o, norm_w, router_w, router_b, w_gate_up, b_gate_up, w_down, b_down,
               final_w, seq, final_norm):
    x1, hp, meta, gates, cnt = _post_mixer(o, x, mod, w_o, norm_w, router_w, router_b, seq)
    eidx = meta[:, :TOP_K].reshape(-1)
    pos = meta[:, TOP_K:2 * TOP_K].reshape(-1)
    counts = cnt[0, :N_EXPERTS]
    xs = _dispatch(hp, eidx, pos, counts)
    sched = _moe_schedule(counts, xs.shape[0] // MOE_TILE, MOE_TILE)
    ys = _moe_experts(xs, sched, w_gate_up, b_gate_up, w_down, b_down)
    return _combine(ys, eidx, pos, counts, gates, x1, mod, final_w, seq, final_norm)


def _gdn_conv_kernel(x_ref, mod_ref, nw_ref, win_ref, cw_ref, q_ref, k_ref, v_ref, buf_ref, *, per_seq):
    tm = x_ref.shape[0]
    nqk = q_ref.shape[1]
    nconv = win_ref.shape[1]
    i = pl.program_id(0)
    mod = mod_ref[0]
    h = _norm_mod(x_ref[...], nw_ref[...], mod[0:1], mod[1:2])

    @pl.when(i % per_seq == 0)
    def _():
        buf_ref[0:SUBLANES, :] = jnp.zeros((SUBLANES, nconv), F32)

    buf_ref[SUBLANES:SUBLANES + tm, :] = _dot(h.astype(BF16), win_ref[...])
    cw = cw_ref[...]
    acc = jnp.zeros((tm, nconv), F32)
    for j in range(GDN_CONV):
        o = SUBLANES - (GDN_CONV - 1) + j
        acc = acc + cw[j:j + 1, :] * buf_ref[o:o + tm, :]
    buf_ref[0:SUBLANES, :] = buf_ref[tm:tm + SUBLANES, :]
    y = acc * _sigmoid(acc)

    def l2(a):
        return a * lax.rsqrt(jnp.sum(a * a, axis=-1, keepdims=True) + 1e-6)

    for hh in range(nqk // GDN_HEAD_DIM):
        s = slice(hh * GDN_HEAD_DIM, (hh + 1) * GDN_HEAD_DIM)
        q_ref[:, s] = l2(y[:, s]) * GDN_HEAD_DIM ** -0.5
        k_ref[:, s] = l2(y[:, nqk + hh * GDN_HEAD_DIM:nqk + (hh + 1) * GDN_HEAD_DIM])
    v_ref[...] = y[:, 2 * nqk:]


def _gdn_gate_kernel(x_ref, mod_ref, nw_ref, win_ref, alog_ref, dtb_ref, z_ref, bg_ref):
    nv = z_ref.shape[1]
    mod = mod_ref[0]
    h = _norm_mod(x_ref[...], nw_ref[...], mod[0:1], mod[1:2])
    pr = _dot(h.astype(BF16), win_ref[...])
    z_ref[...] = pr[:, :nv]
    ba = pr[:, nv:]
    sp = ba + dtb_ref[...]
    softplus = jnp.maximum(sp, 0.0) + jnp.log(1.0 + jnp.exp(-jnp.abs(sp)))
    lane = lax.broadcasted_iota(I32, (1, LANES), 1)
    bg_ref[...] = jnp.where(lane < GDN_V_HEADS, _sigmoid(ba), -jnp.exp(alog_ref[...]) * softplus)


def _gdn_proj(x, mod, norm_w, w_in, conv_w, a_log, dt_bias, seq):
    t, d = x.shape
    tm = TOKEN_TILE
    per_seq = seq // tm
    nqk = GDN_K_HEADS * GDN_HEAD_DIM
    nv = GDN_V_HEADS * GDN_HEAD_DIM
    nconv = 2 * nqk + nv
    w_conv = w_in[:, :nconv].astype(BF16)
    w_gate = jnp.zeros((d, nv + LANES), BF16).at[:, :w_in.shape[1] - nconv].set(w_in[:, nconv:].astype(BF16))
    pad = jnp.zeros((1, LANES), F32)
    alog = pad.at[0, GDN_V_HEADS:2 * GDN_V_HEADS].set(a_log)
    dtb = pad.at[0, GDN_V_HEADS:2 * GDN_V_HEADS].set(dt_bias)
    row = lambda n: pl.BlockSpec((tm, n), lambda i: (i, 0))
    const = lambda a: pl.BlockSpec(a.shape, lambda i: (0,) * a.ndim)
    modspec = pl.BlockSpec((1, N_MOD, d), lambda i: (i // per_seq, 0, 0))
    nw = norm_w.reshape(1, d)
    q, k, v = pl.pallas_call(
        functools.partial(_gdn_conv_kernel, per_seq=per_seq),
        grid=(t // tm,),
        in_specs=[row(d), modspec, const(nw), const(w_conv), const(conv_w)],
        out_specs=[row(nqk), row(nqk), row(nv)],
        out_shape=[jax.ShapeDtypeStruct((t, nqk), F32), jax.ShapeDtypeStruct((t, nqk), F32),
                   jax.ShapeDtypeStruct((t, nv), F32)],
        scratch_shapes=[pltpu.VMEM((tm + 2 * SUBLANES, nconv), F32)],
        compiler_params=_cparams("arbitrary"),
        name="gdn_conv_proj",
    )(x, mod, nw, w_conv, conv_w)
    z, bg = pl.pallas_call(
        _gdn_gate_kernel,
        grid=(t // tm,),
        in_specs=[row(d), modspec, const(nw), const(w_gate), const(alog), const(dtb)],
        out_specs=[row(nv), row(LANES)],
        out_shape=[jax.ShapeDtypeStruct((t, nv), F32), jax.ShapeDtypeStruct((t, LANES), F32)],
        compiler_params=_cparams("arbitrary"),
        name="gdn_gate_proj",
    )(x, mod, nw, w_gate, alog, dtb)
    return q, k, v, z, bg


def _head_col(block, lane_idx):
    lane = lax.broadcasted_iota(I32, (1, LANES), 1)
    col = jnp.sum(jnp.where(lane == lane_idx, block, 0.0), axis=-1, keepdims=True)
    return jnp.broadcast_to(col, block.shape)


def _chunk_cumsum(x):
    rin = lax.broadcasted_iota(I32, x.shape, 0) & (GDN_CHUNK - 1)
    s = 1
    while s < GDN_CHUNK:
        x = x + jnp.where(rin >= s, pltpu.roll(x, s, 0), 0.0)
        s *= 2
    return x


def _fold_chunks(m):
    out = m[:, :GDN_CHUNK]
    for c in range(1, m.shape[1] // GDN_CHUNK):
        out = out + m[:, c * GDN_CHUNK:(c + 1) * GDN_CHUNK]
    return out


def _chunk_masks(n):
    r = lax.broadcasted_iota(I32, (n, n), 0)
    c = lax.broadcasted_iota(I32, (n, n), 1)
    same = (r // GDN_CHUNK) == (c // GDN_CHUNK)
    return same, same & (c <= r), same & (c < r)


def _chunk_decay(gc, causal):
    n = gc.shape[0]
    gi = jnp.concatenate([gc] * (n // LANES), axis=1)
    gj = jnp.broadcast_to(gc.T[0:1, :], (n, n))
    return jnp.where(causal, jnp.exp(jnp.where(causal, gi - gj, 0.0)), 0.0)


def _gdn_l_kernel(k_ref, bg_ref, l_ref):
    hd = pl.program_id(1)
    sc = GDN_SUPER
    _, causal, strict = _chunk_masks(sc)

    def body(s, carry):
        r0 = pl.multiple_of(s * sc, sc)
        k = k_ref[pl.ds(r0, sc), :]
        bg = bg_ref[pl.ds(r0, sc), :]
        beta = _head_col(bg, hd)
        gc = _chunk_cumsum(_head_col(bg, GDN_V_HEADS + hd))
        kk = _dot_nt((k * beta).astype(BF16), k.astype(BF16))
        l_mat = jnp.where(strict, kk * _chunk_decay(gc, causal), 0.0)
        l_ref[0, 0, pl.ds(r0, sc), :] = _fold_chunks(l_mat)
        return carry
    lax.fori_loop(0, k_ref.shape[0] // sc, body, 0)


def _gdn_l(k, bg, batch, seq):
    rep = GDN_V_HEADS // GDN_K_HEADS
    return pl.pallas_call(
        _gdn_l_kernel,
        grid=(batch, GDN_V_HEADS),
        in_specs=[pl.BlockSpec((seq, GDN_HEAD_DIM), lambda b, h: (b, h // rep)),
                  pl.BlockSpec((seq, LANES), lambda b, h: (b, 0))],
        out_specs=pl.BlockSpec((1, 1, seq, GDN_CHUNK), lambda b, h: (b, h, 0, 0)),
        out_shape=jax.ShapeDtypeStruct((batch, GDN_V_HEADS, seq, GDN_CHUNK), F32),
        compiler_params=_cparams("arbitrary", "arbitrary"),
        name="gdn_chunk_l",
    )(k, bg)


def _tri_inv_kernel(a_ref, x_ref):
    c, n = a_ref.shape[1], a_ref.shape[2]
    col = lax.broadcasted_iota(I32, (c, n), 0)

    def row(i, carry):
        def sub(j, acc):
            return acc - a_ref[i, pl.ds(j, 1), :] * x_ref[j]
        x_ref[i] = lax.fori_loop(0, i, sub, jnp.where(col == i, 1.0, 0.0))
        return carry
    lax.fori_loop(0, a_ref.shape[0], row, 0)


def _tri_inv(a):
    c, _, n = a.shape
    spec = pl.BlockSpec((c, c, INV_LANES), lambda i: (0, 0, i))
    return pl.pallas_call(
        _tri_inv_kernel,
        grid=(n // INV_LANES,),
        in_specs=[spec],
        out_specs=spec,
        out_shape=jax.ShapeDtypeStruct(a.shape, F32),
        compiler_params=_cparams("arbitrary"),
        name="gdn_tri_inv",
    )(a)


def _gdn_scan_kernel(q_ref, k_ref, v_ref, z_ref, bg_ref, ti_ref, nw_ref, o_ref):
    hd = pl.program_id(1)
    sc = GDN_SUPER
    nc = sc // GDN_CHUNK
    same, causal, _ = _chunk_masks(sc)
    nw = nw_ref[...]

    def body(s, state):
        r0 = pl.multiple_of(s * sc, sc)
        q = q_ref[pl.ds(r0, sc), :]
        k = k_ref[pl.ds(r0, sc), :]
        v = v_ref[pl.ds(r0, sc), :]
        z = z_ref[pl.ds(r0, sc), :]
        bg = bg_ref[pl.ds(r0, sc), :]
        beta = _head_col(bg, hd)
        gc = _chunk_cumsum(_head_col(bg, GDN_V_HEADS + hd))
        egc = jnp.exp(gc)
        kb = k * beta
        a_intra = _fold_chunks(jnp.where(
            causal, _dot_nt(q.astype(BF16), k.astype(BF16)) * _chunk_decay(gc, causal), 0.0))
        ti = ti_ref[0, 0, pl.ds(r0, sc), :]
        t_inv = jnp.where(same, jnp.concatenate([ti] * nc, axis=1), 0.0)
        uw = _dot_f32(t_inv, jnp.concatenate([v * beta, kb * egc], axis=1))
        u, w = uw[:, :GDN_HEAD_DIM], uw[:, GDN_HEAD_DIM:]
        q_dec = (q * egc).astype(BF16)
        g_last = [gc[(c + 1) * GDN_CHUNK - 1:(c + 1) * GDN_CHUNK, :] for c in range(nc)]
        gl_rows = jnp.concatenate([jnp.broadcast_to(g, (GDN_CHUNK, LANES)) for g in g_last], axis=0)
        k_dec_t = (k * jnp.exp(gl_rows - gc)).T.astype(BF16)
        outs = []
        for c in range(nc):
            rs = slice(c * GDN_CHUNK, (c + 1) * GDN_CHUNK)
            sb = state.astype(BF16)
            v_new = u[rs] - _dot(w[rs].astype(BF16), sb)
            vb = v_new.astype(BF16)
            outs.append(_dot(q_dec[rs], sb) + _dot(a_intra[rs].astype(BF16), vb))
            state = state * jnp.exp(g_last[c]) + _dot(k_dec_t[:, rs], vb)
        o = jnp.concatenate(outs, axis=0)
        o_ref[pl.ds(r0, sc), :] = (_rms(o, nw) * (z * _sigmoid(z))).astype(o_ref.dtype)
        return state
    lax.fori_loop(0, q_ref.shape[0] // sc, body, jnp.zeros((GDN_HEAD_DIM, GDN_HEAD_DIM), F32))


def _gdn_scan(q, k, v, z, bg, t_inv, norm_w, batch, seq):
    rep = GDN_V_HEADS // GDN_K_HEADS
    hd = GDN_HEAD_DIM
    kspec = pl.BlockSpec((seq, hd), lambda b, h: (b, h // rep))
    vspec = pl.BlockSpec((seq, hd), lambda b, h: (b, h))
    return pl.pallas_call(
        _gdn_scan_kernel,
        grid=(batch, GDN_V_HEADS),
        in_specs=[kspec, kspec, vspec, vspec, pl.BlockSpec((seq, LANES), lambda b, h: (b, 0)),
                  pl.BlockSpec((1, 1, seq, GDN_CHUNK), lambda b, h: (b, h, 0, 0)),
                  pl.BlockSpec((1, hd), lambda b, h: (0, 0))],
        out_specs=vspec,
        out_shape=jax.ShapeDtypeStruct((batch * seq, GDN_V_HEADS * hd), BF16),
        compiler_params=_cparams("arbitrary", "arbitrary"),
        name="gdn_scan",
    )(q, k, v, z, bg, t_inv, norm_w.reshape(1, hd))


def _gdn_mixer(x, mod, norm_w, w_in, conv_w, a_log, dt_bias, gnorm_w, batch, seq):
    q, k, v, z, bg = _gdn_proj(x, mod, norm_w, w_in, conv_w, a_log, dt_bias, seq)
    c = GDN_CHUNK
    l_c = _gdn_l(k, bg, batch, seq)
    n_chunks = batch * GDN_V_HEADS * (seq // c)
    a = l_c.reshape(n_chunks, c, c).transpose(1, 2, 0)
    t_inv = _tri_inv(a).transpose(2, 0, 1).reshape(batch, GDN_V_HEADS, seq, c)
    return _gdn_scan(q, k, v, z, bg, t_inv, gnorm_w, batch, seq)


def kernel(x, c, positions, norm_mix, norm_ffn, ada_w, ada_b, mla_w_in, mla_q_norm, mla_w_uq,
           mla_kv_norm, mla_w_ukv, mla_w_o, gdn_w_in, gdn_conv_w, gdn_a_log, gdn_dt_bias, gdn_norm_w,
           gdn_w_o, router_w, router_b, moe_w_gate_up, moe_b_gate_up, moe_w_down, moe_b_down, final_norm):
    batch, seq, d = x.shape
    depth = ada_w.shape[0]
    xt = x.reshape(batch * seq, d)
    mod = _adaln_mod(c, ada_w, ada_b)
    cos, sin = _rope_tables(positions)
    for layer in range(depth):
        j = layer // 2
        if layer % 2 == 0:
            weights = _mla_weights(mla_w_in[j], mla_w_uq[j], mla_w_ukv[j])
            q, k, v = _mla_proj(xt, mod[layer], norm_mix[layer], weights, mla_q_norm[j],
                                mla_kv_norm[j], cos, sin, seq)
            o = _attention(q, k, v, batch, seq)
            w_o = mla_w_o[j]
        else:
            o = _gdn_mixer(xt, mod[layer], norm_mix[layer], gdn_w_in[j], gdn_conv_w[j], gdn_a_log[j],
                           gdn_dt_bias[j], gdn_norm_w[j], batch, seq)
            w_o = gdn_w_o[j]
        xt = _moe_block(o, xt, mod[layer], w_o, norm_ffn[layer], router_w[layer], router_b[layer],
                        moe_w_gate_up[layer], moe_b_gate_up[layer], moe_w_down[layer],
                        moe_b_down[layer], final_norm, seq, layer == depth - 1)
    return xt.reshape(batch, seq, d)
```

```python
import functools
import math

import jax
import jax.numpy as jnp
from jax import lax
from jax.experimental import pallas as pl
from jax.experimental.pallas import tpu as pltpu

F32, BF16, I32, U32 = jnp.float32, jnp.bfloat16, jnp.int32, jnp.uint32

NORM_EPS = 1e-6
N_MOD = 6
MLA_HEADS = 8
QK_NOPE_DIM = 128
QK_ROPE_DIM = 64
V_HEAD_DIM = 128
Q_LORA_RANK = 384
KV_LORA_RANK = 256
ROPE_THETA = 10000.0
GDN_K_HEADS = 8
GDN_V_HEADS = 16
GDN_HEAD_DIM = 128
GDN_CONV = 4
GDN_CHUNK = 64
N_EXPERTS = 32
TOP_K = 4
SWIGLU_LIMIT = 7.0
SWIGLU_ALPHA = 1.702

LANES = 128
SUBLANES = 8
VMEM_LIMIT = 56 * 1024 * 1024

TOKEN_TILE = 256
ATTN_TILE = 256
ATTN_HEADS = 2
MOE_TILE = 256
GDN_SUPER = 4 * GDN_CHUNK
GDN_REP = GDN_V_HEADS // GDN_K_HEADS
GDN_SCAN_HEADS = 4
INV_LANES = 256
NEG_BIG = -1e30
SEG_PIECES = tuple(1 << b for b in range(TOKEN_TILE.bit_length() - 1, SUBLANES.bit_length() - 2, -1))
SEG_ROWS = TOKEN_TILE * TOP_K + N_EXPERTS * SUBLANES


def _cparams(*sem):
    return pltpu.CompilerParams(dimension_semantics=sem, vmem_limit_bytes=VMEM_LIMIT)


def _sigmoid(x):
    return 1.0 / (1.0 + jnp.exp(-x))


def _rms(x, w):
    return x * lax.rsqrt(jnp.mean(x * x, axis=-1, keepdims=True) + NORM_EPS) * w


def _norm_mod(x, w, shift, scale):
    return _rms(x, w) * (1.0 + scale) + shift


def _dot(a, b):
    return jnp.dot(a, b, preferred_element_type=F32)


def _dot_nt(a, b):
    return lax.dot_general(a, b, (((1,), (1,)), ((), ())), preferred_element_type=F32)


def _dot_f32(a, b):
    return jnp.dot(a, b, preferred_element_type=F32, precision=lax.Precision.HIGHEST)


def _pack_pair(lo, hi):
    ulo = lax.bitcast_convert_type(lo.astype(BF16).astype(F32), U32) >> 16
    uhi = lax.bitcast_convert_type(hi.astype(BF16).astype(F32), U32) & jnp.uint32(0xFFFF0000)
    return ulo | uhi


def _unpack_pair(p):
    lo = lax.bitcast_convert_type(p << 16, F32)
    hi = lax.bitcast_convert_type(p & jnp.uint32(0xFFFF0000), F32)
    return lo, hi


def _mod_kernel(c_ref, w_ref, b_ref, o_ref):
    c = c_ref[...]
    cond = c * _sigmoid(c)
    o_ref[0] = _dot(cond.astype(BF16), w_ref[0].astype(BF16)) + b_ref[0]


def _adaln_mod(c, ada_w, ada_b):
    depth, d, n = ada_w.shape
    b = c.shape[0]
    tn = 1024
    out = pl.pallas_call(
        _mod_kernel,
        grid=(depth, n // tn),
        in_specs=[pl.BlockSpec((b, d), lambda l, j: (0, 0)),
                  pl.BlockSpec((1, d, tn), lambda l, j: (l, 0, j)),
                  pl.BlockSpec((1, 1, tn), lambda l, j: (l, 0, j))],
        out_specs=pl.BlockSpec((1, b, tn), lambda l, j: (l, 0, j)),
        out_shape=jax.ShapeDtypeStruct((depth, b, n), F32),
        compiler_params=_cparams("arbitrary", "arbitrary"),
        name="adaln_mod",
    )(c, ada_w, ada_b.reshape(depth, 1, n))
    return out.reshape(depth, b, N_MOD, d)


def _rope_kernel(pos_ref, cos_ref, sin_ref):
    pos = pos_ref[...].astype(F32)
    lane = lax.broadcasted_iota(I32, (1, LANES), 1)
    j = (lane & (QK_ROPE_DIM // 2 - 1)).astype(F32)
    inv = jnp.exp(j * (-2.0 / QK_ROPE_DIM * math.log(ROPE_THETA)))
    ang = pos * inv
    cos_ref[...] = jnp.cos(ang)
    sin_ref[...] = jnp.sin(ang)


def _rope_tables(positions):
    t = positions.size
    tm = 1024
    spec = pl.BlockSpec((tm, LANES), lambda i: (i, 0))
    return pl.pallas_call(
        _rope_kernel,
        grid=(t // tm,),
        in_specs=[pl.BlockSpec((tm, 1), lambda i: (i, 0))],
        out_specs=[spec, spec],
        out_shape=[jax.ShapeDtypeStruct((t, LANES), F32)] * 2,
        compiler_params=_cparams("arbitrary"),
        name="rope_tables",
    )(positions.reshape(t, 1))


def _mla_proj_kernel(x_ref, mod_ref, nw_ref, win_ref, qn_ref, wqa_ref, wqb_ref, kvn_ref, wkv_ref,
                     cos_ref, sin_ref, q_ref, k_ref, v_ref):
    mod = mod_ref[0]
    h = _norm_mod(x_ref[...], nw_ref[...], mod[0:1], mod[1:2])
    lat = _dot(h.astype(BF16), win_ref[...])
    q_lat = lat[:, :Q_LORA_RANK]
    kv_lat = lat[:, Q_LORA_RANK:Q_LORA_RANK + KV_LORA_RANK]
    kp = lat[:, Q_LORA_RANK + KV_LORA_RANK:]
    cos = cos_ref[...]
    sin = sin_ref[...]
    scale = (QK_NOPE_DIM + QK_ROPE_DIM) ** -0.5

    qn = _rms(q_lat, qn_ref[...]).astype(BF16)
    qa = _dot(qn, wqa_ref[...])
    qb = _dot(qn, wqb_ref[...])
    for hh in range(MLA_HEADS):
        o = hh * 2 * LANES
        q_ref[:, o:o + LANES] = (qa[:, o:o + LANES] * scale).astype(BF16)
        pe = qa[:, o + LANES:o + 2 * LANES] * cos + qb[:, hh * LANES:(hh + 1) * LANES] * sin
        q_ref[:, o + LANES:o + 2 * LANES] = (pe * scale).astype(BF16)

    kvn = _rms(kv_lat, kvn_ref[...]).astype(BF16)
    kv = _dot(kvn, wkv_ref[...])
    lane = lax.broadcasted_iota(I32, (1, LANES), 1)
    first = lane < QK_ROPE_DIM
    u = kp * jnp.where(first, cos, sin)
    kr = jnp.where(first, u + pltpu.roll(u, QK_ROPE_DIM, 1), 0.0).astype(BF16)
    nk = MLA_HEADS * QK_NOPE_DIM
    for hh in range(MLA_HEADS):
        o = hh * 2 * LANES
        k_ref[:, o:o + LANES] = kv[:, hh * LANES:(hh + 1) * LANES].astype(BF16)
        k_ref[:, o + LANES:o + 2 * LANES] = kr
    v_ref[...] = kv[:, nk:].astype(BF16)


def _rotate_half_cols(w):
    half = w.shape[-1] // 2
    return jnp.concatenate([-w[..., half:], w[..., :half]], axis=-1)


def _mla_weights(w_in, w_uq, w_ukv):
    kpe = w_in[:, Q_LORA_RANK + KV_LORA_RANK:]
    w_in_ext = jnp.concatenate([w_in, _rotate_half_cols(kpe)], axis=1).astype(BF16)
    wq = w_uq.reshape(Q_LORA_RANK, MLA_HEADS, QK_NOPE_DIM + QK_ROPE_DIM)
    zeros = jnp.zeros((Q_LORA_RANK, MLA_HEADS, LANES - QK_ROPE_DIM), w_uq.dtype)
    wq_pe = wq[:, :, QK_NOPE_DIM:]
    wqa = jnp.concatenate([wq[:, :, :QK_NOPE_DIM], wq_pe, zeros], axis=2)
    wqb = jnp.concatenate([_rotate_half_cols(wq_pe), zeros], axis=2)
    wkv = w_ukv.reshape(KV_LORA_RANK, MLA_HEADS, QK_NOPE_DIM + V_HEAD_DIM)
    wkv = jnp.concatenate([wkv[:, :, :QK_NOPE_DIM].reshape(KV_LORA_RANK, -1),
                           wkv[:, :, QK_NOPE_DIM:].reshape(KV_LORA_RANK, -1)], axis=1)
    return (w_in_ext, wqa.reshape(Q_LORA_RANK, -1).astype(BF16),
            wqb.reshape(Q_LORA_RANK, -1).astype(BF16), wkv.astype(BF16))


def _mla_proj(x, mod, norm_w, weights, q_norm, kv_norm, cos, sin, seq):
    t, d = x.shape
    tm = TOKEN_TILE
    w_in_ext, wqa, wqb, wkv = weights
    per_seq = seq // tm
    full = lambda a: pl.BlockSpec(a.shape, lambda i: (0,) * a.ndim)
    row = lambda n: pl.BlockSpec((tm, n), lambda i: (i, 0))
    nq = MLA_HEADS * 2 * LANES
    nv = MLA_HEADS * V_HEAD_DIM
    args = (x, mod, norm_w.reshape(1, d), w_in_ext, q_norm.reshape(1, -1), wqa, wqb,
            kv_norm.reshape(1, -1), wkv, cos, sin)
    in_specs = [row(d), pl.BlockSpec((1, N_MOD, d), lambda i: (i // per_seq, 0, 0)), full(args[2]),
                full(w_in_ext), full(args[4]), full(wqa), full(wqb), full(args[7]), full(wkv),
                row(LANES), row(LANES)]
    return pl.pallas_call(
        _mla_proj_kernel,
        grid=(t // tm,),
        in_specs=in_specs,
        out_specs=[row(nq), row(nq), row(nv)],
        out_shape=[jax.ShapeDtypeStruct((t, nq), BF16), jax.ShapeDtypeStruct((t, nq), BF16),
                   jax.ShapeDtypeStruct((t, nv), BF16)],
        compiler_params=_cparams("arbitrary"),
        name="mla_proj",
    )(*args)


def _attn_kernel(q_ref, k_ref, v_ref, o_ref):
    tq = q_ref.shape[0]
    dq = 2 * LANES
    dv = V_HEAD_DIM
    qi = pl.program_id(2)
    heads = range(ATTN_HEADS)
    q = [q_ref[:, h * dq:(h + 1) * dq] for h in heads]

    def step(r0, carry, mask):
        out = []
        for h in heads:
            m, l, acc = carry[h]
            s = _dot_nt(q[h], k_ref[pl.ds(r0, tq), h * dq:(h + 1) * dq])
            if mask is not None:
                s = jnp.where(mask, s, NEG_BIG)
            m_new = jnp.maximum(m, jnp.max(s, axis=-1, keepdims=True))
            p = jnp.exp(s - m_new)
            alpha = jnp.exp(m - m_new)
            l = alpha * l + jnp.sum(p, axis=-1, keepdims=True)
            acc = alpha * acc + _dot(p.astype(BF16), v_ref[pl.ds(r0, tq), h * dv:(h + 1) * dv])
            out.append((m_new, l, acc))
        return tuple(out)

    init = tuple((jnp.full((tq, 1), NEG_BIG, F32), jnp.zeros((tq, 1), F32), jnp.zeros((tq, dv), F32))
                 for _ in heads)
    carry = lax.fori_loop(0, qi, lambda j, c: step(pl.multiple_of(j * tq, tq), c, None), init)
    causal = (lax.broadcasted_iota(I32, (tq, tq), 1) <= lax.broadcasted_iota(I32, (tq, tq), 0))
    carry = step(pl.multiple_of(qi * tq, tq), carry, causal)
    for h in heads:
        _, l, acc = carry[h]
        o_ref[:, h * dv:(h + 1) * dv] = (acc / l).astype(o_ref.dtype)


def _attention(q, k, v, batch, seq):
    t = q.shape[0]
    tq = ATTN_TILE
    nq = seq // tq
    dq = ATTN_HEADS * 2 * LANES
    dv = ATTN_HEADS * V_HEAD_DIM
    return pl.pallas_call(
        _attn_kernel,
        grid=(batch, MLA_HEADS // ATTN_HEADS, nq),
        in_specs=[pl.BlockSpec((tq, dq), lambda b, h, i: (b * nq + i, h)),
                  pl.BlockSpec((seq, dq), lambda b, h, i: (b, h)),
                  pl.BlockSpec((seq, dv), lambda b, h, i: (b, h))],
        out_specs=pl.BlockSpec((tq, dv), lambda b, h, i: (b * nq + i, h)),
        out_shape=jax.ShapeDtypeStruct((t, MLA_HEADS * V_HEAD_DIM), BF16),
        compiler_params=_cparams("arbitrary", "arbitrary", "arbitrary"),
        name="mla_attention",
    )(q, k, v)


def _post_mixer_kernel(o_ref, x_ref, mod_ref, wo_ref, nw_ref, rw_ref, rb_ref,
                       x1_ref, h_ref, route_ref, tcnt_ref, trun_ref, cnt_ref, run_ref):
    tm, d = x_ref.shape
    i = pl.program_id(0)

    @pl.when(i == 0)
    def _():
        run_ref[...] = jnp.zeros_like(run_ref)

    mod = mod_ref[0]
    x1 = x_ref[...] + mod[2:3] * _dot(o_ref[...], wo_ref[...])
    x1_ref[...] = x1
    h = _norm_mod(x1, nw_ref[...], mod[3:4], mod[4:5])
    h_ref[...] = h.astype(BF16)

    logits = _dot_f32(h, rw_ref[...]) + rb_ref[...]
    lane = lax.broadcasted_iota(I32, (tm, LANES), 1)
    lane_f = lane.astype(F32)
    work = logits
    val, hot = [], []
    for _ in range(TOP_K):
        m = jnp.max(work, axis=-1, keepdims=True)
        a = jnp.min(jnp.where(work == m, lane_f, float(LANES)), axis=-1, keepdims=True)
        hot.append(lane_f == a)
        val.append(m)
        work = jnp.where(hot[-1], -jnp.inf, work)
    ex = [jnp.exp(v - val[0]) for v in val]
    den = ex[0] + ex[1] + ex[2] + ex[3]

    onehot = jnp.where(hot[0] | hot[1] | hot[2] | hot[3], 1.0, 0.0)
    row = lax.broadcasted_iota(I32, (tm, tm), 0)
    col = lax.broadcasted_iota(I32, (tm, tm), 1)
    tri = jnp.where(col < row, 1.0, 0.0).astype(BF16)
    rank = _dot(tri, onehot.astype(BF16))
    cnt = jnp.sum(onehot, axis=0, keepdims=True)
    cnt = jnp.floor((cnt + (SUBLANES - 1)) * (1.0 / SUBLANES)) * SUBLANES
    e_row = lax.broadcasted_iota(I32, (LANES, LANES), 0)
    e_col = lax.broadcasted_iota(I32, (LANES, LANES), 1)
    before = jnp.where(e_row < e_col, 1.0, 0.0)
    off = _dot_f32(jnp.broadcast_to(cnt, (SUBLANES, LANES)), before)[0:1]
    local = off + rank

    route = jnp.zeros((tm, LANES), F32)
    for kk in range(TOP_K):
        pos = jnp.sum(jnp.where(hot[kk], local, 0.0), axis=-1, keepdims=True)
        route = jnp.where(lane == kk, pos, route)
        route = jnp.where(lane == TOP_K + kk, ex[kk] / den, route)
    route_ref[...] = route
    tcnt_ref[0] = cnt.astype(I32)
    trun_ref[0] = run_ref[...].astype(I32)
    run = run_ref[...] + cnt
    run_ref[...] = run
    cnt_ref[...] = run.astype(I32)


def _post_mixer(o, x, mod, w_o, norm_w, router_w, router_b, seq):
    t, d = x.shape
    ko = o.shape[1]
    tm = TOKEN_TILE
    per_seq = seq // tm
    n_tiles = t // tm
    rw = jnp.zeros((d, LANES), F32).at[:, :N_EXPERTS].set(router_w)
    rb = jnp.full((1, LANES), NEG_BIG, F32).at[0, :N_EXPERTS].set(router_b)
    row = lambda n: pl.BlockSpec((tm, n), lambda i: (i, 0))
    const = lambda r, c: pl.BlockSpec((r, c), lambda i: (0, 0))
    per_tile = pl.BlockSpec((1, 1, LANES), lambda i: (i, 0, 0))
    return pl.pallas_call(
        _post_mixer_kernel,
        grid=(n_tiles,),
        in_specs=[row(ko), row(d), pl.BlockSpec((1, N_MOD, d), lambda i: (i // per_seq, 0, 0)),
                  const(ko, d), const(1, d), const(d, LANES), const(1, LANES)],
        out_specs=[row(d), row(d), row(LANES), per_tile, per_tile, const(1, LANES)],
        out_shape=[jax.ShapeDtypeStruct((t, d), F32), jax.ShapeDtypeStruct((t, d), BF16),
                   jax.ShapeDtypeStruct((t, LANES), F32),
                   jax.ShapeDtypeStruct((n_tiles, 1, LANES), I32),
                   jax.ShapeDtypeStruct((n_tiles, 1, LANES), I32),
                   jax.ShapeDtypeStruct((1, LANES), I32)],
        scratch_shapes=[pltpu.VMEM((1, LANES), F32)],
        compiler_params=_cparams("arbitrary"),
        name="post_mixer_router",
    )(o, x, mod, w_o.astype(BF16), norm_w.reshape(1, d), rw, rb)


def _expert_starts(counts_ref, starts_ref):
    def body(e, acc):
        starts_ref[e] = acc
        return acc + counts_ref[e]
    lax.fori_loop(0, N_EXPERTS, body, jnp.int32(0))


def _segment_copies(tile, tcnt_ref, trun_ref, starts_ref, visit):
    def per_expert(e, off):
        n = tcnt_ref[tile, e]
        base = starts_ref[e] + trun_ref[tile, e]
        for piece in SEG_PIECES:
            done = n & (-2 * piece)

            @pl.when((n & piece) != 0)
            def _():
                visit(pl.multiple_of(off + done, SUBLANES), pl.multiple_of(base + done, SUBLANES), piece)
        return off + n
    lax.fori_loop(0, N_EXPERTS, per_expert, jnp.int32(0))


def _dispatch_kernel(tcnt_ref, trun_ref, counts_ref, route_ref, h_ref, xs_ref, g_ref, zero_ref, starts_ref, sem):
    i = pl.program_id(0)
    n = pl.num_programs(0)
    tm, d = h_ref.shape
    slot = lax.rem(i, 2)

    def copies(tile, s, act):
        def visit(src, dst, nrows):
            act(pltpu.make_async_copy(g_ref.at[s, pl.ds(src, nrows)], xs_ref.at[pl.ds(dst, nrows)], sem.at[s]))
        _segment_copies(tile, tcnt_ref, trun_ref, starts_ref, visit)

    start = lambda c: c.start()
    wait = lambda c: c.wait()

    @pl.when(i == 0)
    def _():
        _expert_starts(counts_ref, starts_ref)

    @pl.when(i >= 2)
    def _():
        copies(i - 2, slot, wait)

    rt = route_ref[...].T
    r = lax.broadcasted_iota(I32, (SEG_ROWS, tm), 0).astype(F32)
    hit = (r == rt[0:1]) | (r == rt[1:2]) | (r == rt[2:3]) | (r == rt[3:4])
    g = _dot(jnp.where(hit, 1.0, 0.0).astype(BF16), h_ref[...])
    g_ref[slot] = _pack_pair(g[:, :d // 2], g[:, d // 2:])
    copies(i, slot, start)

    @pl.when(i == n - 1)
    def _():
        copies(i, slot, wait)

        @pl.when(i >= 1)
        def _():
            copies(i - 1, 1 - slot, wait)
        total = starts_ref[N_EXPERTS - 1] + counts_ref[N_EXPERTS - 1]
        zero_ref[...] = jnp.zeros_like(zero_ref)
        alloc = xs_ref.shape[0]

        def zero_tile(row):
            return pltpu.make_async_copy(zero_ref, xs_ref.at[pl.ds(pl.multiple_of(row, SUBLANES), MOE_TILE)],
                                         sem.at[slot])
        n_full = (alloc - total) // MOE_TILE

        def start_fill(kk, c):
            zero_tile(total + kk * MOE_TILE).start()
            return c

        def wait_fill(kk, c):
            zero_tile(total + kk * MOE_TILE).wait()
            return c
        lax.fori_loop(0, n_full, start_fill, 0)
        lax.fori_loop(0, n_full, wait_fill, 0)
        last = zero_tile(jnp.int32(alloc - MOE_TILE))
        last.start()
        last.wait()


def _sorted_rows(t):
    bound = t * TOP_K + (t // TOKEN_TILE) * N_EXPERTS * (SUBLANES - 1)
    return (bound + MOE_TILE - 1) // MOE_TILE * MOE_TILE + MOE_TILE


def _dispatch(h, route, tcnt, trun, counts):
    t, d = h.shape
    tm = TOKEN_TILE
    smem = pl.BlockSpec(memory_space=pltpu.SMEM)
    return pl.pallas_call(
        _dispatch_kernel,
        grid=(t // tm,),
        in_specs=[smem, smem, smem, pl.BlockSpec((tm, LANES), lambda i: (i, 0)),
                  pl.BlockSpec((tm, d), lambda i: (i, 0))],
        out_specs=pl.BlockSpec(memory_space=pl.ANY),
        out_shape=jax.ShapeDtypeStruct((_sorted_rows(t), d // 2), U32),
        scratch_shapes=[pltpu.VMEM((2, SEG_ROWS, d // 2), U32), pltpu.VMEM((MOE_TILE, d // 2), U32),
                        pltpu.SMEM((N_EXPERTS,), I32), pltpu.SemaphoreType.DMA((2,))],
        compiler_params=_cparams("arbitrary"),
        name="moe_dispatch",
    )(tcnt, trun, counts, route, h)


def _moe_schedule(counts, n_tiles, tile):
    n_visits = n_tiles + N_EXPERTS - 1
    ids = jnp.arange(N_EXPERTS, dtype=I32)
    ends = jnp.cumsum(counts)
    starts = ends - counts
    first_tile = starts // tile
    n_vis = jnp.where(counts > 0, (ends - 1) // tile - first_tile + 1, 0)
    vis_end = jnp.cumsum(n_vis)
    vis_start = vis_end - n_vis
    total = vis_end[-1]
    later = jnp.where((ids[None, :] > ids[:, None]) & (counts[None, :] > 0), ids[None, :], N_EXPERTS)
    next_e = jnp.min(later, axis=1)
    next_e = jnp.where(next_e == N_EXPERTS, -1, next_e)
    v = jnp.minimum(jnp.arange(n_visits, dtype=I32), total - 1)
    e = jnp.minimum(jnp.sum(vis_end[None, :] <= v[:, None], axis=1), N_EXPERTS - 1).astype(I32)
    pick = e[:, None] == ids[None, :]
    of_e = lambda a: jnp.sum(jnp.where(pick, a[None, :], 0), axis=1)
    tile_id = (of_e(first_tile) + v - of_e(vis_start)).astype(I32)
    lo = jnp.maximum(of_e(starts) - tile_id * tile, 0).astype(I32)
    hi = jnp.minimum(of_e(ends) - tile_id * tile, tile).astype(I32)
    step = jnp.arange(n_visits, dtype=I32)
    live = step < total
    prev_tile = jnp.concatenate([jnp.full((1,), -1, I32), tile_id[:-1]])
    new_expert = (live & (v == of_e(vis_start))).astype(I32)
    new_tile = (live & (tile_id != prev_tile)).astype(I32)
    last_tile = (ends[-1] - 1) // tile
    tile_id = jnp.where(live, tile_id, jnp.minimum(last_tile + step - (total - 1), n_tiles - 1)).astype(I32)
    fill = (~live & (tile_id > last_tile)).astype(I32)
    return tile_id, e, lo, hi, new_expert, new_tile, live.astype(I32), of_e(next_e).astype(I32), fill


def _moe_kernel(tile_ref, exp_ref, lo_ref, hi_ref, newe_ref, newt_ref, live_ref, next_ref, fill_ref,
                x_ref, wgu_hbm, bgu_ref, wd_hbm, bd_ref, y_ref,
                wgu_f32, wd_f32, wgu_bf, wd_bf, sem, *, layer):
    v = pl.program_id(0)
    f = wd_bf.shape[0]
    half = x_ref.shape[1]

    def weight_copies(e):
        return (pltpu.make_async_copy(wgu_hbm.at[layer, e], wgu_f32, sem.at[0]),
                pltpu.make_async_copy(wd_hbm.at[layer, e], wd_f32, sem.at[1]))

    @pl.when(newe_ref[v] == 1)
    def _():
        e = exp_ref[v]

        @pl.when(v == 0)
        def _():
            for c in weight_copies(e):
                c.start()
        for c in weight_copies(e):
            c.wait()
        wgu_bf[...] = wgu_f32[...].astype(BF16)
        wd_bf[...] = wd_f32[...].astype(BF16)
        nxt = next_ref[v]

        @pl.when(nxt >= 0)
        def _():
            for c in weight_copies(nxt):
                c.start()

    def compute():
        x_lo, x_hi = _unpack_pair(x_ref[...])
        gu = (_dot(x_lo.astype(BF16), wgu_bf[:half, :]) + _dot(x_hi.astype(BF16), wgu_bf[half:, :])
              + bgu_ref[0])
        gate = jnp.minimum(gu[:, :f], SWIGLU_LIMIT)
        lin = jnp.clip(gu[:, f:], -SWIGLU_LIMIT, SWIGLU_LIMIT)
        act = gate * _sigmoid(SWIGLU_ALPHA * gate) * (lin + 1.0)
        y = _dot(act.astype(BF16), wd_bf[...]) + bd_ref[0]
        return _pack_pair(y[:, :half], y[:, half:])

    @pl.when((live_ref[v] == 1) & (newt_ref[v] == 1))
    def _():
        y_ref[...] = compute()

    @pl.when((live_ref[v] == 1) & (newt_ref[v] == 0))
    def _():
        rows = lax.broadcasted_iota(I32, y_ref.shape, 0)
        mine = (rows >= lo_ref[v]) & (rows < hi_ref[v])
        y_ref[...] = jnp.where(mine, compute(), y_ref[...])

    @pl.when(fill_ref[v] == 1)
    def _():
        y_ref[...] = jnp.zeros_like(y_ref)


def _moe_experts(xs, sched, w_gate_up, b_gate_up, w_down, b_down, layer):
    a, half = xs.shape
    depth, e, d, f2 = w_gate_up.shape
    f = f2 // 2
    tile = MOE_TILE
    n_visits = a // tile + N_EXPERTS - 1
    grid_spec = pltpu.PrefetchScalarGridSpec(
        num_scalar_prefetch=9,
        grid=(n_visits,),
        in_specs=[pl.BlockSpec((tile, half), lambda v, t, ex, *_: (t[v], 0)),
                  pl.BlockSpec(memory_space=pl.ANY),
                  pl.BlockSpec((None, 1, 1, f2), lambda v, t, ex, *_: (layer, ex[v], 0, 0)),
                  pl.BlockSpec(memory_space=pl.ANY),
                  pl.BlockSpec((None, 1, 1, d), lambda v, t, ex, *_: (layer, ex[v], 0, 0))],
        out_specs=pl.BlockSpec((tile, half), lambda v, t, ex, *_: (t[v], 0)),
        scratch_shapes=[pltpu.VMEM((d, f2), F32), pltpu.VMEM((f, d), F32),
                        pltpu.VMEM((d, f2), BF16), pltpu.VMEM((f, d), BF16),
                        pltpu.SemaphoreType.DMA((2,))],
    )
    return pl.pallas_call(
        functools.partial(_moe_kernel, layer=layer),
        grid_spec=grid_spec,
        out_shape=jax.ShapeDtypeStruct((a, half), U32),
        compiler_params=_cparams("arbitrary"),
        name="moe_experts",
    )(*sched, xs, w_gate_up, b_gate_up.reshape(depth, e, 1, f2), w_down, b_down.reshape(depth, e, 1, d))


def _combine_kernel(tcnt_ref, trun_ref, counts_ref, ys_ref, route_ref, x_ref, mod_ref, fw_ref,
                    o_ref, y_buf, starts_ref, sem, *, final_norm):
    i = pl.program_id(0)
    n = pl.num_programs(0)
    tm, d = x_ref.shape
    slot = lax.rem(i, 2)

    def copies(tile, s, act):
        def visit(dst, src, nrows):
            act(pltpu.make_async_copy(ys_ref.at[pl.ds(src, nrows)], y_buf.at[s, pl.ds(dst, nrows)], sem.at[s]))
        _segment_copies(tile, tcnt_ref, trun_ref, starts_ref, visit)

    @pl.when(i == 0)
    def _():
        _expert_starts(counts_ref, starts_ref)
        y_buf[...] = jnp.zeros_like(y_buf)
        copies(0, 0, lambda c: c.start())

    @pl.when(i + 1 < n)
    def _():
        copies(i + 1, 1 - slot, lambda c: c.start())

    copies(i, slot, lambda c: c.wait())
    y_lo, y_hi = _unpack_pair(y_buf[slot])
    route = route_ref[...]
    r = lax.broadcasted_iota(I32, (tm, SEG_ROWS), 1).astype(F32)
    w = jnp.zeros((tm, SEG_ROWS), F32)
    for kk in range(TOP_K):
        w = jnp.where(r == route[:, kk:kk + 1], route[:, TOP_K + kk:TOP_K + kk + 1], w)
    w_hi = w.astype(BF16)
    w_lo = (w - w_hi.astype(F32)).astype(BF16)
    y_lo = y_lo.astype(BF16)
    y_hi = y_hi.astype(BF16)
    y = jnp.concatenate([_dot(w_hi, y_lo) + _dot(w_lo, y_lo), _dot(w_hi, y_hi) + _dot(w_lo, y_hi)], axis=1)
    out = x_ref[...] + mod_ref[0][5:6] * y
    if final_norm:
        out = _rms(out, fw_ref[...])
    o_ref[...] = out


def _combine(ys, route, tcnt, trun, counts, x1, mod, final_w, seq, final_norm):
    t, d = x1.shape
    tm = TOKEN_TILE
    per_seq = seq // tm
    smem = pl.BlockSpec(memory_space=pltpu.SMEM)
    row = lambda n: pl.BlockSpec((tm, n), lambda i: (i, 0))
    return pl.pallas_call(
        functools.partial(_combine_kernel, final_norm=final_norm),
        grid=(t // tm,),
        in_specs=[smem, smem, smem, pl.BlockSpec(memory_space=pl.ANY), row(LANES), row(d),
                  pl.BlockSpec((1, N_MOD, d), lambda i: (i // per_seq, 0, 0)),
                  pl.BlockSpec((1, d), lambda i: (0, 0))],
        out_specs=row(d),
        out_shape=jax.ShapeDtypeStruct((t, d), F32),
        scratch_shapes=[pltpu.VMEM((2, SEG_ROWS, d // 2), U32), pltpu.SMEM((N_EXPERTS,), I32),
                        pltpu.SemaphoreType.DMA((2,))],
        compiler_params=_cparams("arbitrary"),
        name="moe_combine",
    )(tcnt, trun, counts, ys, route, x1, mod, final_w.reshape(1, d))


def _moe_block(o, x, mod, w_o, norm_w, router_w, router_b, w_gate_up, b_gate_up, w_down, b_down,
               final_w, seq, layer, final_norm):
    x1, h, route, tcnt, trun, cnt = _post_mixer(o, x, mod, w_o, norm_w, router_w, router_b, seq)
    n_tiles = tcnt.shape[0]
    tcnt = tcnt.reshape(n_tiles, LANES)
    trun = trun.reshape(n_tiles, LANES)
    counts = cnt[0, :N_EXPERTS]
    xs = _dispatch(h, route, tcnt, trun, counts)
    sched = _moe_schedule(counts, xs.shape[0] // MOE_TILE, MOE_TILE)
    ys = _moe_experts(xs, sched, w_gate_up, b_gate_up, w_down, b_down, layer)
    return _combine(ys, route, tcnt, trun, counts, x1, mod, final_w, seq, final_norm)


def _gdn_conv_kernel(x_ref, mod_ref, nw_ref, win_ref, cw_ref, q_ref, k_ref, v_ref, buf_ref, *, per_seq):
    tm = x_ref.shape[0]
    nqk = q_ref.shape[1]
    nconv = win_ref.shape[1]
    i = pl.program_id(0)
    mod = mod_ref[0]
    h = _norm_mod(x_ref[...], nw_ref[...], mod[0:1], mod[1:2])

    @pl.when(i % per_seq == 0)
    def _():
        buf_ref[0:SUBLANES, :] = jnp.zeros((SUBLANES, nconv), F32)

    buf_ref[SUBLANES:SUBLANES + tm, :] = _dot(h.astype(BF16), win_ref[...])
    cw = cw_ref[...]
    acc = jnp.zeros((tm, nconv), F32)
    for j in range(GDN_CONV):
        o = SUBLANES - (GDN_CONV - 1) + j
        acc = acc + cw[j:j + 1, :] * buf_ref[o:o + tm, :]
    buf_ref[0:SUBLANES, :] = buf_ref[tm:tm + SUBLANES, :]
    y = acc * _sigmoid(acc)

    def l2(a):
        return a * lax.rsqrt(jnp.sum(a * a, axis=-1, keepdims=True) + 1e-6)

    for hh in range(nqk // GDN_HEAD_DIM):
        s = slice(hh * GDN_HEAD_DIM, (hh + 1) * GDN_HEAD_DIM)
        q_ref[:, s] = (l2(y[:, s]) * GDN_HEAD_DIM ** -0.5).astype(BF16)
        k_ref[:, s] = l2(y[:, nqk + hh * GDN_HEAD_DIM:nqk + (hh + 1) * GDN_HEAD_DIM]).astype(BF16)
    v_ref[...] = y[:, 2 * nqk:].astype(BF16)


def _gdn_gate_kernel(x_ref, mod_ref, nw_ref, win_ref, alog_ref, dtb_ref, z_ref, bg_ref):
    nv = z_ref.shape[1]
    mod = mod_ref[0]
    h = _norm_mod(x_ref[...], nw_ref[...], mod[0:1], mod[1:2])
    pr = _dot(h.astype(BF16), win_ref[...])
    z_ref[...] = pr[:, :nv].astype(BF16)
    ba = pr[:, nv:]
    sp = ba + dtb_ref[...]
    softplus = jnp.maximum(sp, 0.0) + jnp.log(1.0 + jnp.exp(-jnp.abs(sp)))
    lane = lax.broadcasted_iota(I32, (1, LANES), 1)
    bg_ref[...] = jnp.where(lane < GDN_V_HEADS, _sigmoid(ba), -jnp.exp(alog_ref[...]) * softplus)


def _gdn_proj(x, mod, norm_w, w_in, conv_w, a_log, dt_bias, seq):
    t, d = x.shape
    tm = TOKEN_TILE
    per_seq = seq // tm
    nqk = GDN_K_HEADS * GDN_HEAD_DIM
    nv = GDN_V_HEADS * GDN_HEAD_DIM
    nconv = 2 * nqk + nv
    w_conv = w_in[:, :nconv].astype(BF16)
    w_gate = jnp.zeros((d, nv + LANES), BF16).at[:, :w_in.shape[1] - nconv].set(w_in[:, nconv:].astype(BF16))
    pad = jnp.zeros((1, LANES), F32)
    alog = pad.at[0, GDN_V_HEADS:2 * GDN_V_HEADS].set(a_log)
    dtb = pad.at[0, GDN_V_HEADS:2 * GDN_V_HEADS].set(dt_bias)
    row = lambda n: pl.BlockSpec((tm, n), lambda i: (i, 0))
    const = lambda a: pl.BlockSpec(a.shape, lambda i: (0,) * a.ndim)
    modspec = pl.BlockSpec((1, N_MOD, d), lambda i: (i // per_seq, 0, 0))
    nw = norm_w.reshape(1, d)
    q, k, v = pl.pallas_call(
        functools.partial(_gdn_conv_kernel, per_seq=per_seq),
        grid=(t // tm,),
        in_specs=[row(d), modspec, const(nw), const(w_conv), const(conv_w)],
        out_specs=[row(nqk), row(nqk), row(nv)],
        out_shape=[jax.ShapeDtypeStruct((t, nqk), BF16), jax.ShapeDtypeStruct((t, nqk), BF16),
                   jax.ShapeDtypeStruct((t, nv), BF16)],
        scratch_shapes=[pltpu.VMEM((tm + 2 * SUBLANES, nconv), F32)],
        compiler_params=_cparams("arbitrary"),
        name="gdn_conv_proj",
    )(x, mod, nw, w_conv, conv_w)
    z, bg = pl.pallas_call(
        _gdn_gate_kernel,
        grid=(t // tm,),
        in_specs=[row(d), modspec, const(nw), const(w_gate), const(alog), const(dtb)],
        out_specs=[row(nv), row(LANES)],
        out_shape=[jax.ShapeDtypeStruct((t, nv), BF16), jax.ShapeDtypeStruct((t, LANES), F32)],
        compiler_params=_cparams("arbitrary"),
        name="gdn_gate_proj",
    )(x, mod, nw, w_gate, alog, dtb)
    return q, k, v, z, bg


def _head_col(block, lane_idx):
    lane = lax.broadcasted_iota(I32, (1, LANES), 1)
    col = jnp.sum(jnp.where(lane == lane_idx, block, 0.0), axis=-1, keepdims=True)
    return jnp.broadcast_to(col, block.shape)


def _chunk_cumsum(x):
    rin = lax.broadcasted_iota(I32, x.shape, 0) & (GDN_CHUNK - 1)
    s = 1
    while s < GDN_CHUNK:
        x = x + jnp.where(rin >= s, pltpu.roll(x, s, 0), 0.0)
        s *= 2
    return x


def _fold_chunks(m):
    out = m[:, :GDN_CHUNK]
    for c in range(1, m.shape[1] // GDN_CHUNK):
        out = out + m[:, c * GDN_CHUNK:(c + 1) * GDN_CHUNK]
    return out


def _chunk_masks(n):
    r = lax.broadcasted_iota(I32, (n, n), 0)
    c = lax.broadcasted_iota(I32, (n, n), 1)
    same = (r // GDN_CHUNK) == (c // GDN_CHUNK)
    return same, same & (c <= r), same & (c < r)


def _wide(col):
    return jnp.concatenate([col] * (col.shape[0] // LANES), axis=1)


def _chunk_decay(gc, causal):
    n = gc.shape[0]
    gj = jnp.broadcast_to(gc.T[0:1, :], (n, n))
    return jnp.where(causal, jnp.exp(jnp.where(causal, _wide(gc) - gj, 0.0)), 0.0)


def _gdn_l_kernel(k_ref, bg_ref, l_ref):
    kh = pl.program_id(1)
    sc = GDN_SUPER
    _, causal, strict = _chunk_masks(sc)

    def body(s, carry):
        r0 = pl.multiple_of(s * sc, sc)
        k = k_ref[pl.ds(r0, sc), :]
        bg = bg_ref[pl.ds(r0, sc), :]
        kk = _dot_nt(k, k)
        for r in range(GDN_REP):
            hd = kh * GDN_REP + r
            beta = _head_col(bg, hd)
            gc = _chunk_cumsum(_head_col(bg, GDN_V_HEADS + hd))
            l_mat = jnp.where(strict, kk * _wide(beta) * _chunk_decay(gc, causal), 0.0)
            l_ref[0, r, pl.ds(r0, sc), :] = _fold_chunks(l_mat)
        return carry
    lax.fori_loop(0, k_ref.shape[0] // sc, body, 0)


def _gdn_l(k, bg, batch, seq):
    return pl.pallas_call(
        _gdn_l_kernel,
        grid=(batch, GDN_K_HEADS),
        in_specs=[pl.BlockSpec((seq, GDN_HEAD_DIM), lambda b, h: (b, h)),
                  pl.BlockSpec((seq, LANES), lambda b, h: (b, 0))],
        out_specs=pl.BlockSpec((1, GDN_REP, seq, GDN_CHUNK), lambda b, h: (b, h, 0, 0)),
        out_shape=jax.ShapeDtypeStruct((batch, GDN_V_HEADS, seq, GDN_CHUNK), F32),
        compiler_params=_cparams("arbitrary", "arbitrary"),
        name="gdn_chunk_l",
    )(k, bg)


def _tri_inv_kernel(a_ref, x_ref):
    c, n = a_ref.shape[1], a_ref.shape[2]
    col = lax.broadcasted_iota(I32, (c, n), 0)
    for i in range(a_ref.shape[0]):
        def sub(j, acc, i=i):
            return acc - a_ref[i, pl.ds(j, 1), :] * x_ref[j]
        row = jnp.where(col == i, 1.0, 0.0)
        if i:
            row = lax.fori_loop(0, i, sub, row, unroll=min(i, 4))
        x_ref[i] = row


def _tri_inv(a):
    c, _, n = a.shape
    spec = pl.BlockSpec((c, c, INV_LANES), lambda i: (0, 0, i))
    return pl.pallas_call(
        _tri_inv_kernel,
        grid=(n // INV_LANES,),
        in_specs=[spec],
        out_specs=spec,
        out_shape=jax.ShapeDtypeStruct(a.shape, F32),
        compiler_params=_cparams("arbitrary"),
        name="gdn_tri_inv",
    )(a)


def _gdn_scan_kernel(q_ref, k_ref, v_ref, z_ref, bg_ref, ti_ref, nw_ref, o_ref):
    grp = pl.program_id(1)
    sc = GDN_SUPER
    nc = sc // GDN_CHUNK
    hd = GDN_HEAD_DIM
    same, causal, _ = _chunk_masks(sc)
    nw = nw_ref[...]

    def body(s, states):
        r0 = pl.multiple_of(s * sc, sc)
        bg = bg_ref[pl.ds(r0, sc), :]
        new_states = []
        for kh in range(GDN_SCAN_HEADS // GDN_REP):
            qb = q_ref[pl.ds(r0, sc), kh * hd:(kh + 1) * hd]
            kb16 = k_ref[pl.ds(r0, sc), kh * hd:(kh + 1) * hd]
            q = qb.astype(F32)
            k = kb16.astype(F32)
            qk = _dot_nt(qb, kb16)
            for r in range(GDN_REP):
                j = kh * GDN_REP + r
                head = grp * GDN_SCAN_HEADS + j
                state = states[j]
                v = v_ref[pl.ds(r0, sc), j * hd:(j + 1) * hd].astype(F32)
                z = z_ref[pl.ds(r0, sc), j * hd:(j + 1) * hd].astype(F32)
                beta = _head_col(bg, head)
                gc = _chunk_cumsum(_head_col(bg, GDN_V_HEADS + head))
                egc = jnp.exp(gc)
                a_intra = _fold_chunks(jnp.where(causal, qk * _chunk_decay(gc, causal), 0.0)).astype(BF16)
                ti = ti_ref[0, j, pl.ds(r0, sc), :]
                t_inv = jnp.where(same, jnp.concatenate([ti] * nc, axis=1), 0.0).astype(BF16)
                uw = _dot(t_inv, jnp.concatenate([v * beta, k * beta * egc], axis=1).astype(BF16))
                u, w = uw[:, :hd], uw[:, hd:].astype(BF16)
                q_dec = (q * egc).astype(BF16)
                g_last = [gc[(c + 1) * GDN_CHUNK - 1:(c + 1) * GDN_CHUNK, :] for c in range(nc)]
                gl_rows = jnp.concatenate([jnp.broadcast_to(g, (GDN_CHUNK, LANES)) for g in g_last], axis=0)
                k_dec_t = (k * jnp.exp(gl_rows - gc)).T.astype(BF16)
                outs = []
                for c in range(nc):
                    rs = slice(c * GDN_CHUNK, (c + 1) * GDN_CHUNK)
                    sb = state.astype(BF16)
                    v_new = u[rs] - _dot(w[rs], sb)
                    vb = v_new.astype(BF16)
                    outs.append(_dot(q_dec[rs], sb) + _dot(a_intra[rs], vb))
                    state = state * jnp.exp(g_last[c]) + _dot(k_dec_t[:, rs], vb)
                o = jnp.concatenate(outs, axis=0)
                o_ref[pl.ds(r0, sc), j * hd:(j + 1) * hd] = (_rms(o, nw) * (z * _sigmoid(z))).astype(o_ref.dtype)
                new_states.append(state)
        return tuple(new_states)

    init = tuple(jnp.zeros((hd, hd), F32) for _ in range(GDN_SCAN_HEADS))
    lax.fori_loop(0, q_ref.shape[0] // sc, body, init)


def _gdn_scan(q, k, v, z, bg, t_inv, norm_w, batch, seq):
    hd = GDN_HEAD_DIM
    nh = GDN_SCAN_HEADS
    kspec = pl.BlockSpec((seq, nh // GDN_REP * hd), lambda b, g: (b, g))
    vspec = pl.BlockSpec((seq, nh * hd), lambda b, g: (b, g))
    return pl.pallas_call(
        _gdn_scan_kernel,
        grid=(batch, GDN_V_HEADS // nh),
        in_specs=[kspec, kspec, vspec, vspec, pl.BlockSpec((seq, LANES), lambda b, g: (b, 0)),
                  pl.BlockSpec((1, nh, seq, GDN_CHUNK), lambda b, g: (b, g, 0, 0)),
                  pl.BlockSpec((1, hd), lambda b, g: (0, 0))],
        out_specs=vspec,
        out_shape=jax.ShapeDtypeStruct((batch * seq, GDN_V_HEADS * hd), BF16),
        compiler_params=_cparams("arbitrary", "arbitrary"),
        name="gdn_scan",
    )(q, k, v, z, bg, t_inv, norm_w.reshape(1, hd))


def _gdn_mixer(x, mod, norm_w, w_in, conv_w, a_log, dt_bias, gnorm_w, batch, seq):
    q, k, v, z, bg = _gdn_proj(x, mod, norm_w, w_in, conv_w, a_log, dt_bias, seq)
    c = GDN_CHUNK
    l_c = _gdn_l(k, bg, batch, seq)
    n_chunks = batch * GDN_V_HEADS * (seq // c)
    a = l_c.reshape(n_chunks, c, c).transpose(1, 2, 0)
    t_inv = _tri_inv(a).transpose(2, 0, 1).reshape(batch, GDN_V_HEADS, seq, c)
    return _gdn_scan(q, k, v, z, bg, t_inv, gnorm_w, batch, seq)


def kernel(x, c, positions, norm_mix, norm_ffn, ada_w, ada_b, mla_w_in, mla_q_norm, mla_w_uq,
           mla_kv_norm, mla_w_ukv, mla_w_o, gdn_w_in, gdn_conv_w, gdn_a_log, gdn_dt_bias, gdn_norm_w,
           gdn_w_o, router_w, router_b, moe_w_gate_up, moe_b_gate_up, moe_w_down, moe_b_down, final_norm):
    batch, seq, d = x.shape
    depth = ada_w.shape[0]
    xt = x.reshape(batch * seq, d)
    mod = _adaln_mod(c, ada_w, ada_b)
    cos, sin = _rope_tables(positions)
    for layer in range(depth):
        j = layer // 2
        if layer % 2 == 0:
            weights = _mla_weights(mla_w_in[j], mla_w_uq[j], mla_w_ukv[j])
            q, k, v = _mla_proj(xt, mod[layer], norm_mix[layer], weights, mla_q_norm[j],
                                mla_kv_norm[j], cos, sin, seq)
            o = _attention(q, k, v, batch, seq)
            w_o = mla_w_o[j]
        else:
            o = _gdn_mixer(xt, mod[layer], norm_mix[layer], gdn_w_in[j], gdn_conv_w[j], gdn_a_log[j],
                           gdn_dt_bias[j], gdn_norm_w[j], batch, seq)
            w_o = gdn_w_o[j]
        xt = _moe_block(o, xt, mod[layer], w_o, norm_ffn[layer], router_w[layer], router_b[layer],
                        moe_w_gate_up, moe_b_gate_up, moe_w_down, moe_b_down, final_norm, seq, layer,
                        layer == depth - 1)
    return xt.reshape(batch, seq, d)
```

```python
import functools
import math

import jax
import jax.numpy as jnp
from jax import lax
from jax.experimental import pallas as pl
from jax.experimental.pallas import tpu as pltpu

F32, BF16, I32, U32 = jnp.float32, jnp.bfloat16, jnp.int32, jnp.uint32

NORM_EPS = 1e-6
N_MOD = 6
MLA_HEADS = 8
QK_NOPE_DIM = 128
QK_ROPE_DIM = 64
V_HEAD_DIM = 128
Q_LORA_RANK = 384
KV_LORA_RANK = 256
ROPE_THETA = 10000.0
GDN_K_HEADS = 8
GDN_V_HEADS = 16
GDN_HEAD_DIM = 128
GDN_CONV = 4
GDN_CHUNK = 64
N_EXPERTS = 32
TOP_K = 4
SWIGLU_LIMIT = 7.0
SWIGLU_ALPHA = 1.702

LANES = 128
SUBLANES = 8
VMEM_LIMIT = 56 * 1024 * 1024

TOKEN_TILE = 256
ROUTE_TILE = 512
ATTN_TILE = 256
ATTN_HEADS = 4
MOE_TILE = 256
GDN_SUPER = 4 * GDN_CHUNK
GDN_REP = GDN_V_HEADS // GDN_K_HEADS
GDN_SCAN_HEADS = 4
INV_LANES = 256
NEG_BIG = -1e30
SEG_PIECES = tuple(1 << b for b in range(ROUTE_TILE.bit_length() - 1, SUBLANES.bit_length() - 2, -1))
SEG_ROWS = ROUTE_TILE * TOP_K + N_EXPERTS * SUBLANES


def _cparams(*sem):
    return pltpu.CompilerParams(dimension_semantics=sem, vmem_limit_bytes=VMEM_LIMIT)


def _sigmoid(x):
    return 1.0 / (1.0 + jnp.exp(-x))


def _rms(x, w):
    return x * lax.rsqrt(jnp.mean(x * x, axis=-1, keepdims=True) + NORM_EPS) * w


def _norm_mod(x, w, shift, scale):
    return _rms(x, w) * (1.0 + scale) + shift


def _dot(a, b):
    return jnp.dot(a, b, preferred_element_type=F32)


def _dot_nt(a, b):
    return lax.dot_general(a, b, (((1,), (1,)), ((), ())), preferred_element_type=F32)


def _dot_f32(a, b):
    return jnp.dot(a, b, preferred_element_type=F32, precision=lax.Precision.HIGHEST)


def _pack_pair(lo, hi):
    ulo = lax.bitcast_convert_type(lo.astype(BF16).astype(F32), U32) >> 16
    uhi = lax.bitcast_convert_type(hi.astype(BF16).astype(F32), U32) & jnp.uint32(0xFFFF0000)
    return ulo | uhi


def _unpack_pair(p):
    lo = lax.bitcast_convert_type(p << 16, F32)
    hi = lax.bitcast_convert_type(p & jnp.uint32(0xFFFF0000), F32)
    return lo, hi


def _mod_kernel(c_ref, w_ref, b_ref, o_ref):
    c = c_ref[...]
    cond = c * _sigmoid(c)
    o_ref[0] = _dot(cond.astype(BF16), w_ref[0].astype(BF16)) + b_ref[0]


def _adaln_mod(c, ada_w, ada_b):
    depth, d, n = ada_w.shape
    b = c.shape[0]
    tn = 1024
    out = pl.pallas_call(
        _mod_kernel,
        grid=(depth, n // tn),
        in_specs=[pl.BlockSpec((b, d), lambda l, j: (0, 0)),
                  pl.BlockSpec((1, d, tn), lambda l, j: (l, 0, j)),
                  pl.BlockSpec((1, 1, tn), lambda l, j: (l, 0, j))],
        out_specs=pl.BlockSpec((1, b, tn), lambda l, j: (l, 0, j)),
        out_shape=jax.ShapeDtypeStruct((depth, b, n), F32),
        compiler_params=_cparams("arbitrary", "arbitrary"),
        name="adaln_mod",
    )(c, ada_w, ada_b.reshape(depth, 1, n))
    return out.reshape(depth, b, N_MOD, d)


def _rope_kernel(pos_ref, cos_ref, sin_ref):
    pos = pos_ref[...].astype(F32)
    lane = lax.broadcasted_iota(I32, (1, LANES), 1)
    j = (lane & (QK_ROPE_DIM // 2 - 1)).astype(F32)
    inv = jnp.exp(j * (-2.0 / QK_ROPE_DIM * math.log(ROPE_THETA)))
    ang = pos * inv
    cos_ref[...] = jnp.cos(ang)
    sin_ref[...] = jnp.sin(ang)


def _rope_tables(positions):
    t = positions.size
    tm = 1024
    spec = pl.BlockSpec((tm, LANES), lambda i: (i, 0))
    return pl.pallas_call(
        _rope_kernel,
        grid=(t // tm,),
        in_specs=[pl.BlockSpec((tm, 1), lambda i: (i, 0))],
        out_specs=[spec, spec],
        out_shape=[jax.ShapeDtypeStruct((t, LANES), F32)] * 2,
        compiler_params=_cparams("arbitrary"),
        name="rope_tables",
    )(positions.reshape(t, 1))


def _mla_proj_kernel(x_ref, mod_ref, nw_ref, win_ref, qn_ref, wqa_ref, wqb_ref, kvn_ref, wkv_ref,
                     cos_ref, sin_ref, q_ref, k_ref, v_ref):
    mod = mod_ref[0]
    h = _norm_mod(x_ref[...], nw_ref[...], mod[0:1], mod[1:2])
    lat = _dot(h.astype(BF16), win_ref[...])
    q_lat = lat[:, :Q_LORA_RANK]
    kv_lat = lat[:, Q_LORA_RANK:Q_LORA_RANK + KV_LORA_RANK]
    kp = lat[:, Q_LORA_RANK + KV_LORA_RANK:]
    cos = cos_ref[...]
    sin = sin_ref[...]
    scale = (QK_NOPE_DIM + QK_ROPE_DIM) ** -0.5

    qn = _rms(q_lat, qn_ref[...]).astype(BF16)
    qa = _dot(qn, wqa_ref[...])
    qb = _dot(qn, wqb_ref[...])
    for hh in range(MLA_HEADS):
        o = hh * 2 * LANES
        q_ref[:, o:o + LANES] = (qa[:, o:o + LANES] * scale).astype(BF16)
        pe = qa[:, o + LANES:o + 2 * LANES] * cos + qb[:, hh * LANES:(hh + 1) * LANES] * sin
        q_ref[:, o + LANES:o + 2 * LANES] = (pe * scale).astype(BF16)

    kvn = _rms(kv_lat, kvn_ref[...]).astype(BF16)
    kv = _dot(kvn, wkv_ref[...])
    lane = lax.broadcasted_iota(I32, (1, LANES), 1)
    first = lane < QK_ROPE_DIM
    u = kp * jnp.where(first, cos, sin)
    kr = jnp.where(first, u + pltpu.roll(u, QK_ROPE_DIM, 1), 0.0).astype(BF16)
    nk = MLA_HEADS * QK_NOPE_DIM
    for hh in range(MLA_HEADS):
        o = hh * 2 * LANES
        k_ref[:, o:o + LANES] = kv[:, hh * LANES:(hh + 1) * LANES].astype(BF16)
        k_ref[:, o + LANES:o + 2 * LANES] = kr
    v_ref[...] = kv[:, nk:].astype(BF16)


def _rotate_half_cols(w):
    half = w.shape[-1] // 2
    return jnp.concatenate([-w[..., half:], w[..., :half]], axis=-1)


def _mla_weights(w_in, w_uq, w_ukv):
    kpe = w_in[:, Q_LORA_RANK + KV_LORA_RANK:]
    w_in_ext = jnp.concatenate([w_in, _rotate_half_cols(kpe)], axis=1).astype(BF16)
    wq = w_uq.reshape(Q_LORA_RANK, MLA_HEADS, QK_NOPE_DIM + QK_ROPE_DIM)
    zeros = jnp.zeros((Q_LORA_RANK, MLA_HEADS, LANES - QK_ROPE_DIM), w_uq.dtype)
    wq_pe = wq[:, :, QK_NOPE_DIM:]
    wqa = jnp.concatenate([wq[:, :, :QK_NOPE_DIM], wq_pe, zeros], axis=2)
    wqb = jnp.concatenate([_rotate_half_cols(wq_pe), zeros], axis=2)
    wkv = w_ukv.reshape(KV_LORA_RANK, MLA_HEADS, QK_NOPE_DIM + V_HEAD_DIM)
    wkv = jnp.concatenate([wkv[:, :, :QK_NOPE_DIM].reshape(KV_LORA_RANK, -1),
                           wkv[:, :, QK_NOPE_DIM:].reshape(KV_LORA_RANK, -1)], axis=1)
    return (w_in_ext, wqa.reshape(Q_LORA_RANK, -1).astype(BF16),
            wqb.reshape(Q_LORA_RANK, -1).astype(BF16), wkv.astype(BF16))


def _mla_proj(x, mod, norm_w, weights, q_norm, kv_norm, cos, sin, seq):
    t, d = x.shape
    tm = TOKEN_TILE
    w_in_ext, wqa, wqb, wkv = weights
    per_seq = seq // tm
    full = lambda a: pl.BlockSpec(a.shape, lambda i: (0,) * a.ndim)
    row = lambda n: pl.BlockSpec((tm, n), lambda i: (i, 0))
    nq = MLA_HEADS * 2 * LANES
    nv = MLA_HEADS * V_HEAD_DIM
    args = (x, mod, norm_w.reshape(1, d), w_in_ext, q_norm.reshape(1, -1), wqa, wqb,
            kv_norm.reshape(1, -1), wkv, cos, sin)
    in_specs = [row(d), pl.BlockSpec((1, N_MOD, d), lambda i: (i // per_seq, 0, 0)), full(args[2]),
                full(w_in_ext), full(args[4]), full(wqa), full(wqb), full(args[7]), full(wkv),
                row(LANES), row(LANES)]
    return pl.pallas_call(
        _mla_proj_kernel,
        grid=(t // tm,),
        in_specs=in_specs,
        out_specs=[row(nq), row(nq), row(nv)],
        out_shape=[jax.ShapeDtypeStruct((t, nq), BF16), jax.ShapeDtypeStruct((t, nq), BF16),
                   jax.ShapeDtypeStruct((t, nv), BF16)],
        compiler_params=_cparams("arbitrary"),
        name="mla_proj",
    )(*args)


def _attn_kernel(q_ref, k_ref, v_ref, o_ref):
    tq = q_ref.shape[0]
    dq = 2 * LANES
    dv = V_HEAD_DIM
    qi = pl.program_id(2)
    heads = range(ATTN_HEADS)
    q = [q_ref[:, h * dq:(h + 1) * dq] for h in heads]

    def step(r0, carry, mask):
        s = [_dot_nt(q[h], k_ref[pl.ds(r0, tq), h * dq:(h + 1) * dq]) for h in heads]
        if mask is not None:
            s = [jnp.where(mask, sh, NEG_BIG) for sh in s]
        m_new = [jnp.maximum(carry[h][0], jnp.max(s[h], axis=-1, keepdims=True)) for h in heads]
        p = [jnp.exp(s[h] - m_new[h]) for h in heads]
        alpha = [jnp.exp(carry[h][0] - m_new[h]) for h in heads]
        l = [alpha[h] * carry[h][1] + jnp.sum(p[h], axis=-1, keepdims=True) for h in heads]
        acc = [alpha[h] * carry[h][2] + _dot(p[h].astype(BF16), v_ref[pl.ds(r0, tq), h * dv:(h + 1) * dv])
               for h in heads]
        return tuple((m_new[h], l[h], acc[h]) for h in heads)

    init = tuple((jnp.full((tq, 1), NEG_BIG, F32), jnp.zeros((tq, 1), F32), jnp.zeros((tq, dv), F32))
                 for _ in heads)
    carry = lax.fori_loop(0, qi, lambda j, c: step(pl.multiple_of(j * tq, tq), c, None), init)
    causal = (lax.broadcasted_iota(I32, (tq, tq), 1) <= lax.broadcasted_iota(I32, (tq, tq), 0))
    carry = step(pl.multiple_of(qi * tq, tq), carry, causal)
    for h in heads:
        _, l, acc = carry[h]
        o_ref[:, h * dv:(h + 1) * dv] = (acc / l).astype(o_ref.dtype)


def _attention(q, k, v, batch, seq):
    t = q.shape[0]
    tq = ATTN_TILE
    nq = seq // tq
    dq = ATTN_HEADS * 2 * LANES
    dv = ATTN_HEADS * V_HEAD_DIM
    return pl.pallas_call(
        _attn_kernel,
        grid=(batch, MLA_HEADS // ATTN_HEADS, nq),
        in_specs=[pl.BlockSpec((tq, dq), lambda b, h, i: (b * nq + i, h)),
                  pl.BlockSpec((seq, dq), lambda b, h, i: (b, h)),
                  pl.BlockSpec((seq, dv), lambda b, h, i: (b, h))],
        out_specs=pl.BlockSpec((tq, dv), lambda b, h, i: (b * nq + i, h)),
        out_shape=jax.ShapeDtypeStruct((t, MLA_HEADS * V_HEAD_DIM), BF16),
        compiler_params=_cparams("arbitrary", "arbitrary", "arbitrary"),
        name="mla_attention",
    )(q, k, v)


def _post_mixer_kernel(o_ref, x_ref, mod_ref, wo_ref, nw_ref, rw_ref, rb_ref,
                       x1_ref, h_ref, route_ref, tcnt_ref, trun_ref, cnt_ref, run_ref):
    tm, d = x_ref.shape
    i = pl.program_id(0)

    @pl.when(i == 0)
    def _():
        run_ref[...] = jnp.zeros_like(run_ref)

    mod = mod_ref[0]
    x1 = x_ref[...] + mod[2:3] * _dot(o_ref[...], wo_ref[...])
    x1_ref[...] = x1
    h = _norm_mod(x1, nw_ref[...], mod[3:4], mod[4:5])
    h_ref[...] = h.astype(BF16)

    h_hi = h.astype(BF16)
    h_lo = (h - h_hi.astype(F32)).astype(BF16)
    parts = _dot(h_hi, rw_ref[...]) + _dot(h_lo, rw_ref[...])
    lane = lax.broadcasted_iota(I32, (tm, LANES), 1)
    logits = jnp.where(lane < N_EXPERTS, parts + pltpu.roll(parts, LANES - N_EXPERTS, 1) + rb_ref[...], NEG_BIG)
    lane_f = lane.astype(F32)
    work = logits
    val, hot = [], []
    for _ in range(TOP_K):
        m = jnp.max(work, axis=-1, keepdims=True)
        a = jnp.min(jnp.where(work == m, lane_f, float(LANES)), axis=-1, keepdims=True)
        hot.append(lane_f == a)
        val.append(m)
        work = jnp.where(hot[-1], -jnp.inf, work)
    ex = [jnp.exp(v - val[0]) for v in val]
    den = ex[0] + ex[1] + ex[2] + ex[3]

    onehot = jnp.where(hot[0] | hot[1] | hot[2] | hot[3], 1.0, 0.0)
    row = lax.broadcasted_iota(I32, (tm, tm), 0)
    col = lax.broadcasted_iota(I32, (tm, tm), 1)
    tri = jnp.where(col < row, 1.0, 0.0).astype(BF16)
    rank = _dot(tri, onehot.astype(BF16))
    cnt = jnp.sum(onehot, axis=0, keepdims=True)
    cnt = jnp.floor((cnt + (SUBLANES - 1)) * (1.0 / SUBLANES)) * SUBLANES
    e_row = lax.broadcasted_iota(I32, (LANES, LANES), 0)
    e_col = lax.broadcasted_iota(I32, (LANES, LANES), 1)
    before = jnp.where(e_row < e_col, 1.0, 0.0)
    off = _dot_f32(jnp.broadcast_to(cnt, (SUBLANES, LANES)), before)[0:1]
    local = off + rank

    route = jnp.zeros((tm, LANES), F32)
    for kk in range(TOP_K):
        pos = jnp.sum(jnp.where(hot[kk], local, 0.0), axis=-1, keepdims=True)
        route = jnp.where(lane == kk, pos, route)
        route = jnp.where(lane == TOP_K + kk, ex[kk] / den, route)
    route_ref[...] = route
    tcnt_ref[0] = cnt.astype(I32)
    trun_ref[0] = run_ref[...].astype(I32)
    run = run_ref[...] + cnt
    run_ref[...] = run
    cnt_ref[...] = run.astype(I32)


def _post_mixer(o, x, mod, w_o, norm_w, router_w, router_b, seq):
    t, d = x.shape
    ko = o.shape[1]
    tm = ROUTE_TILE
    per_seq = seq // tm
    n_tiles = t // tm
    rw_hi = router_w.astype(BF16)
    rw_lo = (router_w - rw_hi.astype(F32)).astype(BF16)
    rw = jnp.zeros((d, LANES), BF16).at[:, :N_EXPERTS].set(rw_hi).at[:, N_EXPERTS:2 * N_EXPERTS].set(rw_lo)
    rb = jnp.zeros((1, LANES), F32).at[0, :N_EXPERTS].set(router_b)
    row = lambda n: pl.BlockSpec((tm, n), lambda i: (i, 0))
    const = lambda r, c: pl.BlockSpec((r, c), lambda i: (0, 0))
    per_tile = pl.BlockSpec((1, 1, LANES), lambda i: (i, 0, 0))
    return pl.pallas_call(
        _post_mixer_kernel,
        grid=(n_tiles,),
        in_specs=[row(ko), row(d), pl.BlockSpec((1, N_MOD, d), lambda i: (i // per_seq, 0, 0)),
                  const(ko, d), const(1, d), const(d, LANES), const(1, LANES)],
        out_specs=[row(d), row(d), row(LANES), per_tile, per_tile, const(1, LANES)],
        out_shape=[jax.ShapeDtypeStruct((t, d), F32), jax.ShapeDtypeStruct((t, d), BF16),
                   jax.ShapeDtypeStruct((t, LANES), F32),
                   jax.ShapeDtypeStruct((n_tiles, 1, LANES), I32),
                   jax.ShapeDtypeStruct((n_tiles, 1, LANES), I32),
                   jax.ShapeDtypeStruct((1, LANES), I32)],
        scratch_shapes=[pltpu.VMEM((1, LANES), F32)],
        compiler_params=_cparams("arbitrary"),
        name="post_mixer_router",
    )(o, x, mod, w_o.astype(BF16), norm_w.reshape(1, d), rw, rb)


def _expert_starts(counts_ref, starts_ref):
    def body(e, acc):
        starts_ref[e] = acc
        return acc + counts_ref[e]
    lax.fori_loop(0, N_EXPERTS, body, jnp.int32(0))


def _segment_copies(tile, tcnt_ref, trun_ref, starts_ref, visit):
    def per_expert(e, off):
        n = tcnt_ref[tile, e]
        base = starts_ref[e] + trun_ref[tile, e]
        for piece in SEG_PIECES:
            done = n & (-2 * piece)

            @pl.when((n & piece) != 0)
            def _():
                visit(pl.multiple_of(off + done, SUBLANES), pl.multiple_of(base + done, SUBLANES), piece)
        return off + n
    lax.fori_loop(0, N_EXPERTS, per_expert, jnp.int32(0))


def _dispatch_kernel(tcnt_ref, trun_ref, counts_ref, route_ref, h_ref, xs_ref, g_ref, zero_ref, starts_ref, sem):
    i = pl.program_id(0)
    n = pl.num_programs(0)
    tm, d = h_ref.shape
    slot = lax.rem(i, 2)

    def copies(tile, s, act):
        def visit(src, dst, nrows):
            act(pltpu.make_async_copy(g_ref.at[s, pl.ds(src, nrows)], xs_ref.at[pl.ds(dst, nrows)], sem.at[s]))
        _segment_copies(tile, tcnt_ref, trun_ref, starts_ref, visit)

    start = lambda c: c.start()
    wait = lambda c: c.wait()

    @pl.when(i == 0)
    def _():
        _expert_starts(counts_ref, starts_ref)

    @pl.when(i >= 2)
    def _():
        copies(i - 2, slot, wait)

    rt = route_ref[...].T
    r = lax.broadcasted_iota(I32, (SEG_ROWS, tm), 0).astype(F32)
    hit = (r == rt[0:1]) | (r == rt[1:2]) | (r == rt[2:3]) | (r == rt[3:4])
    g = _dot(jnp.where(hit, 1.0, 0.0).astype(BF16), h_ref[...])
    g_ref[slot] = _pack_pair(g[:, :d // 2], g[:, d // 2:])
    copies(i, slot, start)

    @pl.when(i == n - 1)
    def _():
        copies(i, slot, wait)

        @pl.when(i >= 1)
        def _():
            copies(i - 1, 1 - slot, wait)
        total = starts_ref[N_EXPERTS - 1] + counts_ref[N_EXPERTS - 1]
        zero_ref[...] = jnp.zeros_like(zero_ref)
        alloc = xs_ref.shape[0]

        def zero_tile(row):
            return pltpu.make_async_copy(zero_ref, xs_ref.at[pl.ds(pl.multiple_of(row, SUBLANES), MOE_TILE)],
                                         sem.at[slot])
        n_full = (alloc - total) // MOE_TILE

        def start_fill(kk, c):
            zero_tile(total + kk * MOE_TILE).start()
            return c

        def wait_fill(kk, c):
            zero_tile(total + kk * MOE_TILE).wait()
            return c
        lax.fori_loop(0, n_full, start_fill, 0)
        lax.fori_loop(0, n_full, wait_fill, 0)
        last = zero_tile(jnp.int32(alloc - MOE_TILE))
        last.start()
        last.wait()


def _sorted_rows(t):
    bound = t * TOP_K + (t // ROUTE_TILE) * N_EXPERTS * (SUBLANES - 1)
    return (bound + MOE_TILE - 1) // MOE_TILE * MOE_TILE + MOE_TILE


def _dispatch(h, route, tcnt, trun, counts):
    t, d = h.shape
    tm = ROUTE_TILE
    smem = pl.BlockSpec(memory_space=pltpu.SMEM)
    return pl.pallas_call(
        _dispatch_kernel,
        grid=(t // tm,),
        in_specs=[smem, smem, smem, pl.BlockSpec((tm, LANES), lambda i: (i, 0)),
                  pl.BlockSpec((tm, d), lambda i: (i, 0))],
        out_specs=pl.BlockSpec(memory_space=pl.ANY),
        out_shape=jax.ShapeDtypeStruct((_sorted_rows(t), d // 2), U32),
        scratch_shapes=[pltpu.VMEM((2, SEG_ROWS, d // 2), U32), pltpu.VMEM((MOE_TILE, d // 2), U32),
                        pltpu.SMEM((N_EXPERTS,), I32), pltpu.SemaphoreType.DMA((2,))],
        compiler_params=_cparams("arbitrary"),
        name="moe_dispatch",
    )(tcnt, trun, counts, route, h)


def _moe_schedule(counts, n_tiles, tile):
    n_visits = n_tiles + N_EXPERTS - 1
    ids = jnp.arange(N_EXPERTS, dtype=I32)
    ends = jnp.cumsum(counts)
    starts = ends - counts
    first_tile = starts // tile
    n_vis = jnp.where(counts > 0, (ends - 1) // tile - first_tile + 1, 0)
    vis_end = jnp.cumsum(n_vis)
    vis_start = vis_end - n_vis
    total = vis_end[-1]
    later = jnp.where((ids[None, :] > ids[:, None]) & (counts[None, :] > 0), ids[None, :], N_EXPERTS)
    next_e = jnp.min(later, axis=1)
    next_e = jnp.where(next_e == N_EXPERTS, -1, next_e)
    v = jnp.minimum(jnp.arange(n_visits, dtype=I32), total - 1)
    e = jnp.minimum(jnp.sum(vis_end[None, :] <= v[:, None], axis=1), N_EXPERTS - 1).astype(I32)
    pick = e[:, None] == ids[None, :]
    of_e = lambda a: jnp.sum(jnp.where(pick, a[None, :], 0), axis=1)
    tile_id = (of_e(first_tile) + v - of_e(vis_start)).astype(I32)
    lo = jnp.maximum(of_e(starts) - tile_id * tile, 0).astype(I32)
    hi = jnp.minimum(of_e(ends) - tile_id * tile, tile).astype(I32)
    step = jnp.arange(n_visits, dtype=I32)
    live = step < total
    prev_tile = jnp.concatenate([jnp.full((1,), -1, I32), tile_id[:-1]])
    new_expert = (live & (v == of_e(vis_start))).astype(I32)
    new_tile = (live & (tile_id != prev_tile)).astype(I32)
    last_tile = (ends[-1] - 1) // tile
    tile_id = jnp.where(live, tile_id, jnp.minimum(last_tile + step - (total - 1), n_tiles - 1)).astype(I32)
    fill = (~live & (tile_id > last_tile)).astype(I32)
    return tile_id, e, lo, hi, new_expert, new_tile, live.astype(I32), of_e(next_e).astype(I32), fill


def _moe_kernel(tile_ref, exp_ref, lo_ref, hi_ref, newe_ref, newt_ref, live_ref, next_ref, fill_ref,
                x_ref, wgu_hbm, bgu_ref, wd_hbm, bd_ref, y_ref,
                wgu_f32, wd_f32, wgu_bf, wd_bf, sem, *, layer):
    v = pl.program_id(0)
    f = wd_bf.shape[0]
    half = x_ref.shape[1]

    def weight_copies(e):
        return (pltpu.make_async_copy(wgu_hbm.at[layer, e], wgu_f32, sem.at[0]),
                pltpu.make_async_copy(wd_hbm.at[layer, e], wd_f32, sem.at[1]))

    @pl.when(newe_ref[v] == 1)
    def _():
        e = exp_ref[v]

        @pl.when(v == 0)
        def _():
            for c in weight_copies(e):
                c.start()
        for c in weight_copies(e):
            c.wait()
        wgu_bf[...] = wgu_f32[...].astype(BF16)
        wd_bf[...] = wd_f32[...].astype(BF16)
        nxt = next_ref[v]

        @pl.when(nxt >= 0)
        def _():
            for c in weight_copies(nxt):
                c.start()

    def compute():
        x_lo, x_hi = _unpack_pair(x_ref[...])
        gu = (_dot(x_lo.astype(BF16), wgu_bf[:half, :]) + _dot(x_hi.astype(BF16), wgu_bf[half:, :])
              + bgu_ref[0])
        gate = jnp.minimum(gu[:, :f], SWIGLU_LIMIT)
        lin = jnp.clip(gu[:, f:], -SWIGLU_LIMIT, SWIGLU_LIMIT)
        act = gate * _sigmoid(SWIGLU_ALPHA * gate) * (lin + 1.0)
        y = _dot(act.astype(BF16), wd_bf[...]) + bd_ref[0]
        return _pack_pair(y[:, :half], y[:, half:])

    @pl.when((live_ref[v] == 1) & (newt_ref[v] == 1))
    def _():
        y_ref[...] = compute()

    @pl.when((live_ref[v] == 1) & (newt_ref[v] == 0))
    def _():
        rows = lax.broadcasted_iota(I32, y_ref.shape, 0)
        mine = (rows >= lo_ref[v]) & (rows < hi_ref[v])
        y_ref[...] = jnp.where(mine, compute(), y_ref[...])

    @pl.when(fill_ref[v] == 1)
    def _():
        y_ref[...] = jnp.zeros_like(y_ref)


def _moe_experts(xs, sched, w_gate_up, b_gate_up, w_down, b_down, layer):
    a, half = xs.shape
    depth, e, d, f2 = w_gate_up.shape
    f = f2 // 2
    tile = MOE_TILE
    n_visits = a // tile + N_EXPERTS - 1
    grid_spec = pltpu.PrefetchScalarGridSpec(
        num_scalar_prefetch=9,
        grid=(n_visits,),
        in_specs=[pl.BlockSpec((tile, half), lambda v, t, ex, *_: (t[v], 0)),
                  pl.BlockSpec(memory_space=pl.ANY),
                  pl.BlockSpec((None, 1, 1, f2), lambda v, t, ex, *_: (layer, ex[v], 0, 0)),
                  pl.BlockSpec(memory_space=pl.ANY),
                  pl.BlockSpec((None, 1, 1, d), lambda v, t, ex, *_: (layer, ex[v], 0, 0))],
        out_specs=pl.BlockSpec((tile, half), lambda v, t, ex, *_: (t[v], 0)),
        scratch_shapes=[pltpu.VMEM((d, f2), F32), pltpu.VMEM((f, d), F32),
                        pltpu.VMEM((d, f2), BF16), pltpu.VMEM((f, d), BF16),
                        pltpu.SemaphoreType.DMA((2,))],
    )
    return pl.pallas_call(
        functools.partial(_moe_kernel, layer=layer),
        grid_spec=grid_spec,
        out_shape=jax.ShapeDtypeStruct((a, half), U32),
        compiler_params=_cparams("arbitrary"),
        name="moe_experts",
    )(*sched, xs, w_gate_up, b_gate_up.reshape(depth, e, 1, f2), w_down, b_down.reshape(depth, e, 1, d))


def _combine_kernel(tcnt_ref, trun_ref, counts_ref, ys_ref, route_ref, x_ref, mod_ref, fw_ref,
                    o_ref, y_buf, starts_ref, sem, *, final_norm):
    i = pl.program_id(0)
    n = pl.num_programs(0)
    tm, d = x_ref.shape
    slot = lax.rem(i, 2)

    def copies(tile, s, act):
        def visit(dst, src, nrows):
            act(pltpu.make_async_copy(ys_ref.at[pl.ds(src, nrows)], y_buf.at[s, pl.ds(dst, nrows)], sem.at[s]))
        _segment_copies(tile, tcnt_ref, trun_ref, starts_ref, visit)

    @pl.when(i == 0)
    def _():
        _expert_starts(counts_ref, starts_ref)
        y_buf[...] = jnp.zeros_like(y_buf)
        copies(0, 0, lambda c: c.start())

    @pl.when(i + 1 < n)
    def _():
        copies(i + 1, 1 - slot, lambda c: c.start())

    copies(i, slot, lambda c: c.wait())
    y_lo, y_hi = _unpack_pair(y_buf[slot])
    route = route_ref[...]
    r = lax.broadcasted_iota(I32, (tm, SEG_ROWS), 1).astype(F32)
    w = jnp.zeros((tm, SEG_ROWS), F32)
    for kk in range(TOP_K):
        w = jnp.where(r == route[:, kk:kk + 1], route[:, TOP_K + kk:TOP_K + kk + 1], w)
    w_hi = w.astype(BF16)
    w_lo = (w - w_hi.astype(F32)).astype(BF16)
    y_lo = y_lo.astype(BF16)
    y_hi = y_hi.astype(BF16)
    y = jnp.concatenate([_dot(w_hi, y_lo) + _dot(w_lo, y_lo), _dot(w_hi, y_hi) + _dot(w_lo, y_hi)], axis=1)
    out = x_ref[...] + mod_ref[0][5:6] * y
    if final_norm:
        out = _rms(out, fw_ref[...])
    o_ref[...] = out


def _combine(ys, route, tcnt, trun, counts, x1, mod, final_w, seq, final_norm):
    t, d = x1.shape
    tm = ROUTE_TILE
    per_seq = seq // tm
    smem = pl.BlockSpec(memory_space=pltpu.SMEM)
    row = lambda n: pl.BlockSpec((tm, n), lambda i: (i, 0))
    return pl.pallas_call(
        functools.partial(_combine_kernel, final_norm=final_norm),
        grid=(t // tm,),
        in_specs=[smem, smem, smem, pl.BlockSpec(memory_space=pl.ANY), row(LANES), row(d),
                  pl.BlockSpec((1, N_MOD, d), lambda i: (i // per_seq, 0, 0)),
                  pl.BlockSpec((1, d), lambda i: (0, 0))],
        out_specs=row(d),
        out_shape=jax.ShapeDtypeStruct((t, d), F32),
        scratch_shapes=[pltpu.VMEM((2, SEG_ROWS, d // 2), U32), pltpu.SMEM((N_EXPERTS,), I32),
                        pltpu.SemaphoreType.DMA((2,))],
        compiler_params=_cparams("arbitrary"),
        name="moe_combine",
    )(tcnt, trun, counts, ys, route, x1, mod, final_w.reshape(1, d))


def _moe_block(o, x, mod, w_o, norm_w, router_w, router_b, w_gate_up, b_gate_up, w_down, b_down,
               final_w, seq, layer, final_norm):
    x1, h, route, tcnt, trun, cnt = _post_mixer(o, x, mod, w_o, norm_w, router_w, router_b, seq)
    n_tiles = tcnt.shape[0]
    tcnt = tcnt.reshape(n_tiles, LANES)
    trun = trun.reshape(n_tiles, LANES)
    counts = cnt[0, :N_EXPERTS]
    xs = _dispatch(h, route, tcnt, trun, counts)
    sched = _moe_schedule(counts, xs.shape[0] // MOE_TILE, MOE_TILE)
    ys = _moe_experts(xs, sched, w_gate_up, b_gate_up, w_down, b_down, layer)
    return _combine(ys, route, tcnt, trun, counts, x1, mod, final_w, seq, final_norm)


def _gdn_conv_kernel(x_ref, mod_ref, nw_ref, win_ref, cw_ref, q_ref, k_ref, v_ref, buf_ref, *, per_seq):
    tm = x_ref.shape[0]
    nqk = q_ref.shape[1]
    nconv = win_ref.shape[1]
    i = pl.program_id(0)
    mod = mod_ref[0]
    h = _norm_mod(x_ref[...], nw_ref[...], mod[0:1], mod[1:2])

    @pl.when(i % per_seq == 0)
    def _():
        buf_ref[0:SUBLANES, :] = jnp.zeros((SUBLANES, nconv), F32)

    buf_ref[SUBLANES:SUBLANES + tm, :] = _dot(h.astype(BF16), win_ref[...])
    cw = cw_ref[...]
    acc = jnp.zeros((tm, nconv), F32)
    for j in range(GDN_CONV):
        o = SUBLANES - (GDN_CONV - 1) + j
        acc = acc + cw[j:j + 1, :] * buf_ref[o:o + tm, :]
    buf_ref[0:SUBLANES, :] = buf_ref[tm:tm + SUBLANES, :]
    y = acc * _sigmoid(acc)

    def l2(a):
        return a * lax.rsqrt(jnp.sum(a * a, axis=-1, keepdims=True) + 1e-6)

    for hh in range(nqk // GDN_HEAD_DIM):
        s = slice(hh * GDN_HEAD_DIM, (hh + 1) * GDN_HEAD_DIM)
        q_ref[:, s] = (l2(y[:, s]) * GDN_HEAD_DIM ** -0.5).astype(BF16)
        k_ref[:, s] = l2(y[:, nqk + hh * GDN_HEAD_DIM:nqk + (hh + 1) * GDN_HEAD_DIM]).astype(BF16)
    v_ref[...] = y[:, 2 * nqk:].astype(BF16)


def _gdn_gate_kernel(x_ref, mod_ref, nw_ref, win_ref, alog_ref, dtb_ref, z_ref, bg_ref):
    nv = z_ref.shape[1]
    mod = mod_ref[0]
    h = _norm_mod(x_ref[...], nw_ref[...], mod[0:1], mod[1:2])
    pr = _dot(h.astype(BF16), win_ref[...])
    z_ref[...] = pr[:, :nv].astype(BF16)
    ba = pr[:, nv:]
    sp = ba + dtb_ref[...]
    softplus = jnp.maximum(sp, 0.0) + jnp.log(1.0 + jnp.exp(-jnp.abs(sp)))
    lane = lax.broadcasted_iota(I32, (1, LANES), 1)
    bg_ref[...] = jnp.where(lane < GDN_V_HEADS, _sigmoid(ba), -jnp.exp(alog_ref[...]) * softplus)


def _gdn_proj(x, mod, norm_w, w_in, conv_w, a_log, dt_bias, seq):
    t, d = x.shape
    tm = TOKEN_TILE
    per_seq = seq // tm
    nqk = GDN_K_HEADS * GDN_HEAD_DIM
    nv = GDN_V_HEADS * GDN_HEAD_DIM
    nconv = 2 * nqk + nv
    w_conv = w_in[:, :nconv].astype(BF16)
    w_gate = jnp.zeros((d, nv + LANES), BF16).at[:, :w_in.shape[1] - nconv].set(w_in[:, nconv:].astype(BF16))
    pad = jnp.zeros((1, LANES), F32)
    alog = pad.at[0, GDN_V_HEADS:2 * GDN_V_HEADS].set(a_log)
    dtb = pad.at[0, GDN_V_HEADS:2 * GDN_V_HEADS].set(dt_bias)
    row = lambda n: pl.BlockSpec((tm, n), lambda i: (i, 0))
    const = lambda a: pl.BlockSpec(a.shape, lambda i: (0,) * a.ndim)
    modspec = pl.BlockSpec((1, N_MOD, d), lambda i: (i // per_seq, 0, 0))
    nw = norm_w.reshape(1, d)
    q, k, v = pl.pallas_call(
        functools.partial(_gdn_conv_kernel, per_seq=per_seq),
        grid=(t // tm,),
        in_specs=[row(d), modspec, const(nw), const(w_conv), const(conv_w)],
        out_specs=[row(nqk), row(nqk), row(nv)],
        out_shape=[jax.ShapeDtypeStruct((t, nqk), BF16), jax.ShapeDtypeStruct((t, nqk), BF16),
                   jax.ShapeDtypeStruct((t, nv), BF16)],
        scratch_shapes=[pltpu.VMEM((tm + 2 * SUBLANES, nconv), F32)],
        compiler_params=_cparams("arbitrary"),
        name="gdn_conv_proj",
    )(x, mod, nw, w_conv, conv_w)
    z, bg = pl.pallas_call(
        _gdn_gate_kernel,
        grid=(t // tm,),
        in_specs=[row(d), modspec, const(nw), const(w_gate), const(alog), const(dtb)],
        out_specs=[row(nv), row(LANES)],
        out_shape=[jax.ShapeDtypeStruct((t, nv), BF16), jax.ShapeDtypeStruct((t, LANES), F32)],
        compiler_params=_cparams("arbitrary"),
        name="gdn_gate_proj",
    )(x, mod, nw, w_gate, alog, dtb)
    return q, k, v, z, bg


def _head_col(block, lane_idx):
    lane = lax.broadcasted_iota(I32, (1, LANES), 1)
    col = jnp.sum(jnp.where(lane == lane_idx, block, 0.0), axis=-1, keepdims=True)
    return jnp.broadcast_to(col, block.shape)


def _chunk_cumsum(x):
    rin = lax.broadcasted_iota(I32, x.shape, 0) & (GDN_CHUNK - 1)
    s = 1
    while s < GDN_CHUNK:
        x = x + jnp.where(rin >= s, pltpu.roll(x, s, 0), 0.0)
        s *= 2
    return x


def _diag_blocks(m):
    c = GDN_CHUNK
    return jnp.concatenate([m[b * c:(b + 1) * c, b * c:(b + 1) * c] for b in range(m.shape[0] // c)], axis=0)


def _chunk_decay(gc):
    n = gc.shape[0]
    c = GDN_CHUNK
    gt = gc.T
    gj = jnp.concatenate([jnp.broadcast_to(gt[0:1, b * c:(b + 1) * c], (c, c)) for b in range(n // c)], axis=0)
    i_in = lax.broadcasted_iota(I32, (n, c), 0) & (c - 1)
    col = lax.broadcasted_iota(I32, (n, c), 1)
    lower = col <= i_in
    decay = jnp.where(lower, jnp.exp(jnp.where(lower, gc[:, :c] - gj, 0.0)), 0.0)
    return decay, col < i_in


def _gdn_l_kernel(k_ref, bg_ref, l_ref):
    kh = pl.program_id(1)
    sc = GDN_SUPER
    c = GDN_CHUNK

    def body(s, carry):
        r0 = pl.multiple_of(s * sc, sc)
        k = k_ref[pl.ds(r0, sc), :]
        bg = bg_ref[pl.ds(r0, sc), :]
        kk = _diag_blocks(_dot_nt(k, k))
        for r in range(GDN_REP):
            hd = kh * GDN_REP + r
            beta = _head_col(bg, hd)
            decay, strict = _chunk_decay(_chunk_cumsum(_head_col(bg, GDN_V_HEADS + hd)))
            l_ref[0, r, pl.ds(r0, sc), :] = jnp.where(strict, kk * beta[:, :c] * decay, 0.0)
        return carry
    lax.fori_loop(0, k_ref.shape[0] // sc, body, 0)


def _gdn_l(k, bg, batch, seq):
    return pl.pallas_call(
        _gdn_l_kernel,
        grid=(batch, GDN_K_HEADS),
        in_specs=[pl.BlockSpec((seq, GDN_HEAD_DIM), lambda b, h: (b, h)),
                  pl.BlockSpec((seq, LANES), lambda b, h: (b, 0))],
        out_specs=pl.BlockSpec((1, GDN_REP, seq, GDN_CHUNK), lambda b, h: (b, h, 0, 0)),
        out_shape=jax.ShapeDtypeStruct((batch, GDN_V_HEADS, seq, GDN_CHUNK), F32),
        compiler_params=_cparams("arbitrary", "arbitrary"),
        name="gdn_chunk_l",
    )(k, bg)


def _tri_inv_kernel(a_ref, x_ref):
    c, n = a_ref.shape[1], a_ref.shape[2]
    col = lax.broadcasted_iota(I32, (c, n), 0)
    for i in range(a_ref.shape[0]):
        def sub(j, acc, i=i):
            return acc - a_ref[i, pl.ds(j, 1), :] * x_ref[j]
        row = jnp.where(col == i, 1.0, 0.0)
        if i:
            row = lax.fori_loop(0, i, sub, row, unroll=min(i, 4))
        x_ref[i] = row


def _tri_inv(a):
    c, _, n = a.shape
    spec = pl.BlockSpec((c, c, INV_LANES), lambda i: (0, 0, i))
    return pl.pallas_call(
        _tri_inv_kernel,
        grid=(n // INV_LANES,),
        in_specs=[spec],
        out_specs=spec,
        out_shape=jax.ShapeDtypeStruct(a.shape, F32),
        compiler_params=_cparams("arbitrary"),
        name="gdn_tri_inv",
    )(a)


def _gdn_scan_kernel(q_ref, k_ref, v_ref, z_ref, bg_ref, ti_ref, nw_ref, o_ref):
    grp = pl.program_id(1)
    sc = GDN_SUPER
    nc = sc // GDN_CHUNK
    hd = GDN_HEAD_DIM
    n_super = q_ref.shape[0] // sc
    nw = nw_ref[...]
    r_blk = lax.broadcasted_iota(I32, (sc, sc), 0) // GDN_CHUNK
    c_blk = lax.broadcasted_iota(I32, (sc, sc), 1) // GDN_CHUNK
    same = r_blk == c_blk

    heads = range(GDN_SCAN_HEADS)

    def prepare_head(s, j, shared):
        r0 = pl.multiple_of(s * sc, sc)
        kh = j // GDN_REP
        if kh not in shared:
            qb = q_ref[pl.ds(r0, sc), kh * hd:(kh + 1) * hd]
            kb = k_ref[pl.ds(r0, sc), kh * hd:(kh + 1) * hd]
            shared[kh] = (qb.astype(F32), kb.astype(F32), _diag_blocks(_dot_nt(qb, kb)))
        q, k, qk = shared[kh]
        head = grp * GDN_SCAN_HEADS + j
        bg = bg_ref[pl.ds(r0, sc), :]
        v = v_ref[pl.ds(r0, sc), j * hd:(j + 1) * hd].astype(F32)
        beta = _head_col(bg, head)
        gc = _chunk_cumsum(_head_col(bg, GDN_V_HEADS + head))
        egc = jnp.exp(gc)
        decay, _ = _chunk_decay(gc)
        a_intra = (qk * decay).astype(BF16)
        ti = ti_ref[0, j, pl.ds(r0, sc), :]
        t_inv = jnp.where(same, jnp.concatenate([ti] * nc, axis=1), 0.0).astype(BF16)
        uw = _dot(t_inv, jnp.concatenate([v * beta, k * beta * egc], axis=1).astype(BF16))
        g_last = [gc[(c + 1) * GDN_CHUNK - 1:(c + 1) * GDN_CHUNK, :] for c in range(nc)]
        gl_rows = jnp.concatenate([jnp.broadcast_to(g, (GDN_CHUNK, LANES)) for g in g_last], axis=0)
        k_dec_t = (k * jnp.exp(gl_rows - gc)).T.astype(BF16)
        eg_last = jnp.concatenate([jnp.exp(g) for g in g_last], axis=0)
        return uw[:, :hd], uw[:, hd:].astype(BF16), (q * egc).astype(BF16), k_dec_t, a_intra, eg_last

    def prepare(s):
        shared = {}
        return tuple(prepare_head(s, j, shared) for j in heads)

    def body(s, carry):
        states, prepared = carry
        states = list(states)
        r0 = pl.multiple_of(s * sc, sc)
        nxt = jnp.minimum(s + 1, n_super - 1)
        shared = {}
        following = []
        outs = [[] for _ in heads]
        for c in range(nc):
            rs = slice(c * GDN_CHUNK, (c + 1) * GDN_CHUNK)
            sb = [states[j].astype(BF16) for j in heads]
            v_new = [prepared[j][0][rs] - _dot(prepared[j][1][rs], sb[j]) for j in heads]
            vb = [v.astype(BF16) for v in v_new]
            for j in heads:
                outs[j].append(_dot(prepared[j][2][rs], sb[j]) + _dot(prepared[j][4][rs], vb[j]))
            states = [states[j] * prepared[j][5][c:c + 1, :] + _dot(prepared[j][3][:, rs], vb[j]) for j in heads]
            following.append(prepare_head(nxt, c, shared))
        for j in heads:
            o = jnp.concatenate(outs[j], axis=0)
            z = z_ref[pl.ds(r0, sc), j * hd:(j + 1) * hd].astype(F32)
            o_ref[pl.ds(r0, sc), j * hd:(j + 1) * hd] = (_rms(o, nw) * (z * _sigmoid(z))).astype(o_ref.dtype)
        return tuple(states), tuple(following)

    assert GDN_SCAN_HEADS == nc
    init = tuple(jnp.zeros((hd, hd), F32) for _ in heads)
    lax.fori_loop(0, n_super, body, (init, prepare(0)))


def _gdn_scan(q, k, v, z, bg, t_inv, norm_w, batch, seq):
    hd = GDN_HEAD_DIM
    nh = GDN_SCAN_HEADS
    kspec = pl.BlockSpec((seq, nh // GDN_REP * hd), lambda b, g: (b, g))
    vspec = pl.BlockSpec((seq, nh * hd), lambda b, g: (b, g))
    return pl.pallas_call(
        _gdn_scan_kernel,
        grid=(batch, GDN_V_HEADS // nh),
        in_specs=[kspec, kspec, vspec, vspec, pl.BlockSpec((seq, LANES), lambda b, g: (b, 0)),
                  pl.BlockSpec((1, nh, seq, GDN_CHUNK), lambda b, g: (b, g, 0, 0)),
                  pl.BlockSpec((1, hd), lambda b, g: (0, 0))],
        out_specs=vspec,
        out_shape=jax.ShapeDtypeStruct((batch * seq, GDN_V_HEADS * hd), BF16),
        compiler_params=_cparams("arbitrary", "arbitrary"),
        name="gdn_scan",
    )(q, k, v, z, bg, t_inv, norm_w.reshape(1, hd))


def _gdn_mixer(x, mod, norm_w, w_in, conv_w, a_log, dt_bias, gnorm_w, batch, seq):
    q, k, v, z, bg = _gdn_proj(x, mod, norm_w, w_in, conv_w, a_log, dt_bias, seq)
    c = GDN_CHUNK
    l_c = _gdn_l(k, bg, batch, seq)
    n_chunks = batch * GDN_V_HEADS * (seq // c)
    a = l_c.reshape(n_chunks, c, c).transpose(1, 2, 0)
    t_inv = _tri_inv(a).transpose(2, 0, 1).reshape(batch, GDN_V_HEADS, seq, c)
    return _gdn_scan(q, k, v, z, bg, t_inv, gnorm_w, batch, seq)


def kernel(x, c, positions, norm_mix, norm_ffn, ada_w, ada_b, mla_w_in, mla_q_norm, mla_w_uq,
           mla_kv_norm, mla_w_ukv, mla_w_o, gdn_w_in, gdn_conv_w, gdn_a_log, gdn_dt_bias, gdn_norm_w,
           gdn_w_o, router_w, router_b, moe_w_gate_up, moe_b_gate_up, moe_w_down, moe_b_down, final_norm):
    batch, seq, d = x.shape
    depth = ada_w.shape[0]
    xt = x.reshape(batch * seq, d)
    mod = _adaln_mod(c, ada_w, ada_b)
    cos, sin = _rope_tables(positions)
    for layer in range(depth):
        j = layer // 2
        if layer % 2 == 0:
            weights = _mla_weights(mla_w_in[j], mla_w_uq[j], mla_w_ukv[j])
            q, k, v = _mla_proj(xt, mod[layer], norm_mix[layer], weights, mla_q_norm[j],
                                mla_kv_norm[j], cos, sin, seq)
            o = _attention(q, k, v, batch, seq)
            w_o = mla_w_o[j]
        else:
            o = _gdn_mixer(xt, mod[layer], norm_mix[layer], gdn_w_in[j], gdn_conv_w[j], gdn_a_log[j],
                           gdn_dt_bias[j], gdn_norm_w[j], batch, seq)
            w_o = gdn_w_o[j]
        xt = _moe_block(o, xt, mod[layer], w_o, norm_ffn[layer], router_w[layer], router_b[layer],
                        moe_w_gate_up, moe_b_gate_up, moe_w_down, moe_b_down, final_norm, seq, layer,
                        layer == depth - 1)
    return xt.reshape(batch, seq, d)
```

```python
import functools
import math

import jax
import jax.numpy as jnp
from jax import lax
from jax.experimental import pallas as pl
from jax.experimental.pallas import tpu as pltpu

F32, BF16, I32, U32 = jnp.float32, jnp.bfloat16, jnp.int32, jnp.uint32

NORM_EPS = 1e-6
N_MOD = 6
MLA_HEADS = 8
QK_NOPE_DIM = 128
QK_ROPE_DIM = 64
V_HEAD_DIM = 128
Q_LORA_RANK = 384
KV_LORA_RANK = 256
ROPE_THETA = 10000.0
GDN_K_HEADS = 8
GDN_V_HEADS = 16
GDN_HEAD_DIM = 128
GDN_CONV = 4
GDN_CHUNK = 64
N_EXPERTS = 32
TOP_K = 4
SWIGLU_LIMIT = 7.0
SWIGLU_ALPHA = 1.702

LANES = 128
SUBLANES = 8
VMEM_LIMIT = 56 * 1024 * 1024

TOKEN_TILE = 256
ROUTE_TILE = 512
ATTN_TILE = 256
ATTN_HEADS = 4
MOE_TILE = 256
GDN_SUPER = 4 * GDN_CHUNK
GDN_REP = GDN_V_HEADS // GDN_K_HEADS
GDN_SCAN_HEADS = 4
INV_LANES = 512
NEG_BIG = -1e30
SEG_PIECES = tuple(1 << b for b in range(ROUTE_TILE.bit_length() - 1, SUBLANES.bit_length() - 2, -1))
SEG_ROWS = ROUTE_TILE * TOP_K + N_EXPERTS * SUBLANES


def _cparams(*sem):
    return pltpu.CompilerParams(dimension_semantics=sem, vmem_limit_bytes=VMEM_LIMIT)


def _sigmoid(x):
    return 1.0 / (1.0 + jnp.exp(-x))


def _rms(x, w):
    return x * lax.rsqrt(jnp.mean(x * x, axis=-1, keepdims=True) + NORM_EPS) * w


def _norm_mod(x, w, shift, scale):
    return _rms(x, w) * (1.0 + scale) + shift


def _dot(a, b):
    return jnp.dot(a, b, preferred_element_type=F32)


def _dot_nt(a, b):
    return lax.dot_general(a, b, (((1,), (1,)), ((), ())), preferred_element_type=F32)


def _dot_f32(a, b):
    return jnp.dot(a, b, preferred_element_type=F32, precision=lax.Precision.HIGHEST)


def _pack_pair(lo, hi):
    ulo = lax.bitcast_convert_type(lo.astype(BF16).astype(F32), U32) >> 16
    uhi = lax.bitcast_convert_type(hi.astype(BF16).astype(F32), U32) & jnp.uint32(0xFFFF0000)
    return ulo | uhi


def _unpack_pair(p):
    lo = lax.bitcast_convert_type(p << 16, F32)
    hi = lax.bitcast_convert_type(p & jnp.uint32(0xFFFF0000), F32)
    return lo, hi


def _mod_kernel(c_ref, w_ref, b_ref, o_ref):
    c = c_ref[...]
    cond = c * _sigmoid(c)
    o_ref[0] = _dot(cond.astype(BF16), w_ref[0].astype(BF16)) + b_ref[0]


def _adaln_mod(c, ada_w, ada_b):
    depth, d, n = ada_w.shape
    b = c.shape[0]
    tn = 1024
    out = pl.pallas_call(
        _mod_kernel,
        grid=(depth, n // tn),
        in_specs=[pl.BlockSpec((b, d), lambda l, j: (0, 0)),
                  pl.BlockSpec((1, d, tn), lambda l, j: (l, 0, j)),
                  pl.BlockSpec((1, 1, tn), lambda l, j: (l, 0, j))],
        out_specs=pl.BlockSpec((1, b, tn), lambda l, j: (l, 0, j)),
        out_shape=jax.ShapeDtypeStruct((depth, b, n), F32),
        compiler_params=_cparams("arbitrary", "arbitrary"),
        name="adaln_mod",
    )(c, ada_w, ada_b.reshape(depth, 1, n))
    return out.reshape(depth, b, N_MOD, d)


def _rope_kernel(pos_ref, cos_ref, sin_ref):
    pos = pos_ref[...].astype(F32)
    lane = lax.broadcasted_iota(I32, (1, LANES), 1)
    j = (lane & (QK_ROPE_DIM // 2 - 1)).astype(F32)
    inv = jnp.exp(j * (-2.0 / QK_ROPE_DIM * math.log(ROPE_THETA)))
    ang = pos * inv
    cos_ref[...] = jnp.cos(ang)
    sin_ref[...] = jnp.sin(ang)


def _rope_tables(positions):
    t = positions.size
    tm = 1024
    spec = pl.BlockSpec((tm, LANES), lambda i: (i, 0))
    return pl.pallas_call(
        _rope_kernel,
        grid=(t // tm,),
        in_specs=[pl.BlockSpec((tm, 1), lambda i: (i, 0))],
        out_specs=[spec, spec],
        out_shape=[jax.ShapeDtypeStruct((t, LANES), F32)] * 2,
        compiler_params=_cparams("arbitrary"),
        name="rope_tables",
    )(positions.reshape(t, 1))


def _mla_proj_kernel(x_ref, mod_ref, nw_ref, win_ref, qn_ref, wqa_ref, wqb_ref, kvn_ref, wkv_ref,
                     cos_ref, sin_ref, q_ref, k_ref, v_ref):
    mod = mod_ref[0]
    h = _norm_mod(x_ref[...], nw_ref[...], mod[0:1], mod[1:2])
    lat = _dot(h.astype(BF16), win_ref[...])
    q_lat = lat[:, :Q_LORA_RANK]
    kv_lat = lat[:, Q_LORA_RANK:Q_LORA_RANK + KV_LORA_RANK]
    kp = lat[:, Q_LORA_RANK + KV_LORA_RANK:]
    cos = cos_ref[...]
    sin = sin_ref[...]
    scale = (QK_NOPE_DIM + QK_ROPE_DIM) ** -0.5

    qn = _rms(q_lat, qn_ref[...]).astype(BF16)
    qa = _dot(qn, wqa_ref[...])
    qb = _dot(qn, wqb_ref[...])
    for hh in range(MLA_HEADS):
        o = hh * 2 * LANES
        q_ref[:, o:o + LANES] = (qa[:, o:o + LANES] * scale).astype(BF16)
        pe = qa[:, o + LANES:o + 2 * LANES] * cos + qb[:, hh * LANES:(hh + 1) * LANES] * sin
        q_ref[:, o + LANES:o + 2 * LANES] = (pe * scale).astype(BF16)

    kvn = _rms(kv_lat, kvn_ref[...]).astype(BF16)
    kv = _dot(kvn, wkv_ref[...])
    lane = lax.broadcasted_iota(I32, (1, LANES), 1)
    first = lane < QK_ROPE_DIM
    u = kp * jnp.where(first, cos, sin)
    kr = jnp.where(first, u + pltpu.roll(u, QK_ROPE_DIM, 1), 0.0).astype(BF16)
    nk = MLA_HEADS * QK_NOPE_DIM
    for hh in range(MLA_HEADS):
        o = hh * 2 * LANES
        k_ref[:, o:o + LANES] = kv[:, hh * LANES:(hh + 1) * LANES].astype(BF16)
        k_ref[:, o + LANES:o + 2 * LANES] = kr
    v_ref[...] = kv[:, nk:].astype(BF16)


def _rotate_half_cols(w):
    half = w.shape[-1] // 2
    return jnp.concatenate([-w[..., half:], w[..., :half]], axis=-1)


def _mla_weights(w_in, w_uq, w_ukv):
    kpe = w_in[:, Q_LORA_RANK + KV_LORA_RANK:]
    w_in_ext = jnp.concatenate([w_in, _rotate_half_cols(kpe)], axis=1).astype(BF16)
    wq = w_uq.reshape(Q_LORA_RANK, MLA_HEADS, QK_NOPE_DIM + QK_ROPE_DIM)
    zeros = jnp.zeros((Q_LORA_RANK, MLA_HEADS, LANES - QK_ROPE_DIM), w_uq.dtype)
    wq_pe = wq[:, :, QK_NOPE_DIM:]
    wqa = jnp.concatenate([wq[:, :, :QK_NOPE_DIM], wq_pe, zeros], axis=2)
    wqb = jnp.concatenate([_rotate_half_cols(wq_pe), zeros], axis=2)
    wkv = w_ukv.reshape(KV_LORA_RANK, MLA_HEADS, QK_NOPE_DIM + V_HEAD_DIM)
    wkv = jnp.concatenate([wkv[:, :, :QK_NOPE_DIM].reshape(KV_LORA_RANK, -1),
                           wkv[:, :, QK_NOPE_DIM:].reshape(KV_LORA_RANK, -1)], axis=1)
    return (w_in_ext, wqa.reshape(Q_LORA_RANK, -1).astype(BF16),
            wqb.reshape(Q_LORA_RANK, -1).astype(BF16), wkv.astype(BF16))


def _mla_proj(x, mod, norm_w, weights, q_norm, kv_norm, cos, sin, seq):
    t, d = x.shape
    tm = TOKEN_TILE
    w_in_ext, wqa, wqb, wkv = weights
    per_seq = seq // tm
    full = lambda a: pl.BlockSpec(a.shape, lambda i: (0,) * a.ndim)
    row = lambda n: pl.BlockSpec((tm, n), lambda i: (i, 0))
    nq = MLA_HEADS * 2 * LANES
    nv = MLA_HEADS * V_HEAD_DIM
    args = (x, mod, norm_w.reshape(1, d), w_in_ext, q_norm.reshape(1, -1), wqa, wqb,
            kv_norm.reshape(1, -1), wkv, cos, sin)
    in_specs = [row(d), pl.BlockSpec((1, N_MOD, d), lambda i: (i // per_seq, 0, 0)), full(args[2]),
                full(w_in_ext), full(args[4]), full(wqa), full(wqb), full(args[7]), full(wkv),
                row(LANES), row(LANES)]
    return pl.pallas_call(
        _mla_proj_kernel,
        grid=(t // tm,),
        in_specs=in_specs,
        out_specs=[row(nq), row(nq), row(nv)],
        out_shape=[jax.ShapeDtypeStruct((t, nq), BF16), jax.ShapeDtypeStruct((t, nq), BF16),
                   jax.ShapeDtypeStruct((t, nv), BF16)],
        compiler_params=_cparams("arbitrary"),
        name="mla_proj",
    )(*args)


def _attn_kernel(q_ref, k_ref, v_ref, o_ref):
    tq = q_ref.shape[0]
    dq = 2 * LANES
    dv = V_HEAD_DIM
    qi = pl.program_id(2)
    heads = range(ATTN_HEADS)
    q = [q_ref[:, h * dq:(h + 1) * dq] for h in heads]

    def step(r0, carry, mask):
        s = [_dot_nt(q[h], k_ref[pl.ds(r0, tq), h * dq:(h + 1) * dq]) for h in heads]
        if mask is not None:
            s = [jnp.where(mask, sh, NEG_BIG) for sh in s]
        m_new = [jnp.maximum(carry[h][0], jnp.max(s[h], axis=-1, keepdims=True)) for h in heads]
        p = [jnp.exp(s[h] - m_new[h]) for h in heads]
        alpha = [jnp.exp(carry[h][0] - m_new[h]) for h in heads]
        l = [alpha[h] * carry[h][1] + jnp.sum(p[h], axis=-1, keepdims=True) for h in heads]
        acc = [alpha[h] * carry[h][2] + _dot(p[h].astype(BF16), v_ref[pl.ds(r0, tq), h * dv:(h + 1) * dv])
               for h in heads]
        return tuple((m_new[h], l[h], acc[h]) for h in heads)

    init = tuple((jnp.full((tq, 1), NEG_BIG, F32), jnp.zeros((tq, 1), F32), jnp.zeros((tq, dv), F32))
                 for _ in heads)
    carry = lax.fori_loop(0, qi, lambda j, c: step(pl.multiple_of(j * tq, tq), c, None), init)
    causal = (lax.broadcasted_iota(I32, (tq, tq), 1) <= lax.broadcasted_iota(I32, (tq, tq), 0))
    carry = step(pl.multiple_of(qi * tq, tq), carry, causal)
    for h in heads:
        _, l, acc = carry[h]
        o_ref[:, h * dv:(h + 1) * dv] = (acc / l).astype(o_ref.dtype)


def _attention(q, k, v, batch, seq):
    t = q.shape[0]
    tq = ATTN_TILE
    nq = seq // tq
    dq = ATTN_HEADS * 2 * LANES
    dv = ATTN_HEADS * V_HEAD_DIM
    return pl.pallas_call(
        _attn_kernel,
        grid=(batch, MLA_HEADS // ATTN_HEADS, nq),
        in_specs=[pl.BlockSpec((tq, dq), lambda b, h, i: (b * nq + i, h)),
                  pl.BlockSpec((seq, dq), lambda b, h, i: (b, h)),
                  pl.BlockSpec((seq, dv), lambda b, h, i: (b, h))],
        out_specs=pl.BlockSpec((tq, dv), lambda b, h, i: (b * nq + i, h)),
        out_shape=jax.ShapeDtypeStruct((t, MLA_HEADS * V_HEAD_DIM), BF16),
        compiler_params=_cparams("arbitrary", "arbitrary", "arbitrary"),
        name="mla_attention",
    )(q, k, v)


def _post_mixer_kernel(o_ref, x_ref, mod_ref, wo_ref, nw_ref, rw_ref, rb_ref,
                       x1_ref, h_ref, route_ref, tcnt_ref, trun_ref, cnt_ref, run_ref):
    tm, d = x_ref.shape
    i = pl.program_id(0)

    @pl.when(i == 0)
    def _():
        run_ref[...] = jnp.zeros_like(run_ref)

    mod = mod_ref[0]
    x1 = x_ref[...] + mod[2:3] * _dot(o_ref[...], wo_ref[...])
    x1_ref[...] = x1
    h = _norm_mod(x1, nw_ref[...], mod[3:4], mod[4:5])
    h_ref[...] = h.astype(BF16)

    h_hi = h.astype(BF16)
    h_lo = (h - h_hi.astype(F32)).astype(BF16)
    parts = _dot(h_hi, rw_ref[...]) + _dot(h_lo, rw_ref[...])
    lane = lax.broadcasted_iota(I32, (tm, LANES), 1)
    logits = jnp.where(lane < N_EXPERTS, parts + pltpu.roll(parts, LANES - N_EXPERTS, 1) + rb_ref[...], NEG_BIG)
    lane_f = lane.astype(F32)
    work = logits
    val, hot = [], []
    for _ in range(TOP_K):
        m = jnp.max(work, axis=-1, keepdims=True)
        a = jnp.min(jnp.where(work == m, lane_f, float(LANES)), axis=-1, keepdims=True)
        hot.append(lane_f == a)
        val.append(m)
        work = jnp.where(hot[-1], -jnp.inf, work)
    ex = [jnp.exp(v - val[0]) for v in val]
    den = ex[0] + ex[1] + ex[2] + ex[3]

    onehot = jnp.where(hot[0] | hot[1] | hot[2] | hot[3], 1.0, 0.0)
    row = lax.broadcasted_iota(I32, (tm, tm), 0)
    col = lax.broadcasted_iota(I32, (tm, tm), 1)
    tri = jnp.where(col < row, 1.0, 0.0).astype(BF16)
    rank = _dot(tri, onehot.astype(BF16))
    cnt = jnp.sum(onehot, axis=0, keepdims=True)
    cnt = jnp.floor((cnt + (SUBLANES - 1)) * (1.0 / SUBLANES)) * SUBLANES
    e_row = lax.broadcasted_iota(I32, (LANES, LANES), 0)
    e_col = lax.broadcasted_iota(I32, (LANES, LANES), 1)
    before = jnp.where(e_row < e_col, 1.0, 0.0)
    off = _dot_f32(jnp.broadcast_to(cnt, (SUBLANES, LANES)), before)[0:1]
    local = off + rank

    route = jnp.zeros((tm, LANES), F32)
    for kk in range(TOP_K):
        pos = jnp.sum(jnp.where(hot[kk], local, 0.0), axis=-1, keepdims=True)
        route = jnp.where(lane == kk, pos, route)
        route = jnp.where(lane == TOP_K + kk, ex[kk] / den, route)
    route_ref[...] = route
    tcnt_ref[0] = cnt.astype(I32)
    trun_ref[0] = run_ref[...].astype(I32)
    run = run_ref[...] + cnt
    run_ref[...] = run
    cnt_ref[...] = run.astype(I32)


def _post_mixer(o, x, mod, w_o, norm_w, router_w, router_b, seq):
    t, d = x.shape
    ko = o.shape[1]
    tm = ROUTE_TILE
    per_seq = seq // tm
    n_tiles = t // tm
    rw_hi = router_w.astype(BF16)
    rw_lo = (router_w - rw_hi.astype(F32)).astype(BF16)
    rw = jnp.zeros((d, LANES), BF16).at[:, :N_EXPERTS].set(rw_hi).at[:, N_EXPERTS:2 * N_EXPERTS].set(rw_lo)
    rb = jnp.zeros((1, LANES), F32).at[0, :N_EXPERTS].set(router_b)
    row = lambda n: pl.BlockSpec((tm, n), lambda i: (i, 0))
    const = lambda r, c: pl.BlockSpec((r, c), lambda i: (0, 0))
    per_tile = pl.BlockSpec((1, 1, LANES), lambda i: (i, 0, 0))
    return pl.pallas_call(
        _post_mixer_kernel,
        grid=(n_tiles,),
        in_specs=[row(ko), row(d), pl.BlockSpec((1, N_MOD, d), lambda i: (i // per_seq, 0, 0)),
                  const(ko, d), const(1, d), const(d, LANES), const(1, LANES)],
        out_specs=[row(d), row(d), row(LANES), per_tile, per_tile, const(1, LANES)],
        out_shape=[jax.ShapeDtypeStruct((t, d), F32), jax.ShapeDtypeStruct((t, d), BF16),
                   jax.ShapeDtypeStruct((t, LANES), F32),
                   jax.ShapeDtypeStruct((n_tiles, 1, LANES), I32),
                   jax.ShapeDtypeStruct((n_tiles, 1, LANES), I32),
                   jax.ShapeDtypeStruct((1, LANES), I32)],
        scratch_shapes=[pltpu.VMEM((1, LANES), F32)],
        compiler_params=_cparams("arbitrary"),
        name="post_mixer_router",
    )(o, x, mod, w_o.astype(BF16), norm_w.reshape(1, d), rw, rb)


def _expert_starts(counts_ref, starts_ref):
    def body(e, acc):
        starts_ref[e] = acc
        return acc + counts_ref[e]
    lax.fori_loop(0, N_EXPERTS, body, jnp.int32(0))


def _segment_copies(tile, tcnt_ref, trun_ref, starts_ref, visit):
    def per_expert(e, off):
        n = tcnt_ref[tile, e]
        base = starts_ref[e] + trun_ref[tile, e]
        for piece in SEG_PIECES:
            done = n & (-2 * piece)

            @pl.when((n & piece) != 0)
            def _():
                visit(pl.multiple_of(off + done, SUBLANES), pl.multiple_of(base + done, SUBLANES), piece)
        return off + n
    lax.fori_loop(0, N_EXPERTS, per_expert, jnp.int32(0))


def _dispatch_kernel(tcnt_ref, trun_ref, counts_ref, route_ref, h_ref, xs_ref, g_ref, zero_ref, starts_ref, sem):
    i = pl.program_id(0)
    n = pl.num_programs(0)
    tm, d = h_ref.shape
    slot = lax.rem(i, 2)

    def copies(tile, s, act):
        def visit(src, dst, nrows):
            act(pltpu.make_async_copy(g_ref.at[s, pl.ds(src, nrows)], xs_ref.at[pl.ds(dst, nrows)], sem.at[s]))
        _segment_copies(tile, tcnt_ref, trun_ref, starts_ref, visit)

    start = lambda c: c.start()
    wait = lambda c: c.wait()

    @pl.when(i == 0)
    def _():
        _expert_starts(counts_ref, starts_ref)

    @pl.when(i >= 2)
    def _():
        copies(i - 2, slot, wait)

    rt = route_ref[...].T
    r = lax.broadcasted_iota(I32, (SEG_ROWS, tm), 0).astype(F32)
    hit = (r == rt[0:1]) | (r == rt[1:2]) | (r == rt[2:3]) | (r == rt[3:4])
    g = _dot(jnp.where(hit, 1.0, 0.0).astype(BF16), h_ref[...])
    g_ref[slot] = _pack_pair(g[:, :d // 2], g[:, d // 2:])
    copies(i, slot, start)

    @pl.when(i == n - 1)
    def _():
        copies(i, slot, wait)

        @pl.when(i >= 1)
        def _():
            copies(i - 1, 1 - slot, wait)
        total = starts_ref[N_EXPERTS - 1] + counts_ref[N_EXPERTS - 1]
        zero_ref[...] = jnp.zeros_like(zero_ref)
        alloc = xs_ref.shape[0]

        def zero_tile(row):
            return pltpu.make_async_copy(zero_ref, xs_ref.at[pl.ds(pl.multiple_of(row, SUBLANES), MOE_TILE)],
                                         sem.at[slot])
        n_full = (alloc - total) // MOE_TILE

        def start_fill(kk, c):
            zero_tile(total + kk * MOE_TILE).start()
            return c

        def wait_fill(kk, c):
            zero_tile(total + kk * MOE_TILE).wait()
            return c
        lax.fori_loop(0, n_full, start_fill, 0)
        lax.fori_loop(0, n_full, wait_fill, 0)
        last = zero_tile(jnp.int32(alloc - MOE_TILE))
        last.start()
        last.wait()


def _sorted_rows(t):
    bound = t * TOP_K + (t // ROUTE_TILE) * N_EXPERTS * (SUBLANES - 1)
    return (bound + MOE_TILE - 1) // MOE_TILE * MOE_TILE + MOE_TILE


def _dispatch(h, route, tcnt, trun, counts):
    t, d = h.shape
    tm = ROUTE_TILE
    smem = pl.BlockSpec(memory_space=pltpu.SMEM)
    return pl.pallas_call(
        _dispatch_kernel,
        grid=(t // tm,),
        in_specs=[smem, smem, smem, pl.BlockSpec((tm, LANES), lambda i: (i, 0)),
                  pl.BlockSpec((tm, d), lambda i: (i, 0))],
        out_specs=pl.BlockSpec(memory_space=pl.ANY),
        out_shape=jax.ShapeDtypeStruct((_sorted_rows(t), d // 2), U32),
        scratch_shapes=[pltpu.VMEM((2, SEG_ROWS, d // 2), U32), pltpu.VMEM((MOE_TILE, d // 2), U32),
                        pltpu.SMEM((N_EXPERTS,), I32), pltpu.SemaphoreType.DMA((2,))],
        compiler_params=_cparams("arbitrary"),
        name="moe_dispatch",
    )(tcnt, trun, counts, route, h)


def _moe_schedule(counts, n_tiles, tile):
    n_visits = n_tiles + N_EXPERTS - 1
    ids = jnp.arange(N_EXPERTS, dtype=I32)
    ends = jnp.cumsum(counts)
    starts = ends - counts
    first_tile = starts // tile
    n_vis = jnp.where(counts > 0, (ends - 1) // tile - first_tile + 1, 0)
    vis_end = jnp.cumsum(n_vis)
    vis_start = vis_end - n_vis
    total = vis_end[-1]
    later = jnp.where((ids[None, :] > ids[:, None]) & (counts[None, :] > 0), ids[None, :], N_EXPERTS)
    next_e = jnp.min(later, axis=1)
    next_e = jnp.where(next_e == N_EXPERTS, -1, next_e)
    v = jnp.minimum(jnp.arange(n_visits, dtype=I32), total - 1)
    e = jnp.minimum(jnp.sum(vis_end[None, :] <= v[:, None], axis=1), N_EXPERTS - 1).astype(I32)
    pick = e[:, None] == ids[None, :]
    of_e = lambda a: jnp.sum(jnp.where(pick, a[None, :], 0), axis=1)
    tile_id = (of_e(first_tile) + v - of_e(vis_start)).astype(I32)
    lo = jnp.maximum(of_e(starts) - tile_id * tile, 0).astype(I32)
    hi = jnp.minimum(of_e(ends) - tile_id * tile, tile).astype(I32)
    step = jnp.arange(n_visits, dtype=I32)
    live = step < total
    prev_tile = jnp.concatenate([jnp.full((1,), -1, I32), tile_id[:-1]])
    new_expert = (live & (v == of_e(vis_start))).astype(I32)
    new_tile = (live & (tile_id != prev_tile)).astype(I32)
    last_tile = (ends[-1] - 1) // tile
    tile_id = jnp.where(live, tile_id, jnp.minimum(last_tile + step - (total - 1), n_tiles - 1)).astype(I32)
    fill = (~live & (tile_id > last_tile)).astype(I32)
    return tile_id, e, lo, hi, new_expert, new_tile, live.astype(I32), of_e(next_e).astype(I32), fill


def _moe_kernel(tile_ref, exp_ref, lo_ref, hi_ref, newe_ref, newt_ref, live_ref, next_ref, fill_ref,
                x_ref, wgu_hbm, bgu_ref, wd_hbm, bd_ref, y_ref,
                wgu_f32, wd_f32, wgu_bf, wd_bf, sem, *, layer):
    v = pl.program_id(0)
    f = wd_bf.shape[0]
    half = x_ref.shape[1]

    def weight_copies(e):
        return (pltpu.make_async_copy(wgu_hbm.at[layer, e], wgu_f32, sem.at[0]),
                pltpu.make_async_copy(wd_hbm.at[layer, e], wd_f32, sem.at[1]))

    @pl.when(newe_ref[v] == 1)
    def _():
        e = exp_ref[v]

        @pl.when(v == 0)
        def _():
            for c in weight_copies(e):
                c.start()
        for c in weight_copies(e):
            c.wait()
        wgu_bf[...] = wgu_f32[...].astype(BF16)
        wd_bf[...] = wd_f32[...].astype(BF16)
        nxt = next_ref[v]

        @pl.when(nxt >= 0)
        def _():
            for c in weight_copies(nxt):
                c.start()

    def compute():
        x_lo, x_hi = _unpack_pair(x_ref[...])
        gu = (_dot(x_lo.astype(BF16), wgu_bf[:half, :]) + _dot(x_hi.astype(BF16), wgu_bf[half:, :])
              + bgu_ref[0])
        gate = jnp.minimum(gu[:, :f], SWIGLU_LIMIT)
        lin = jnp.clip(gu[:, f:], -SWIGLU_LIMIT, SWIGLU_LIMIT)
        act = gate * _sigmoid(SWIGLU_ALPHA * gate) * (lin + 1.0)
        y = _dot(act.astype(BF16), wd_bf[...]) + bd_ref[0]
        return _pack_pair(y[:, :half], y[:, half:])

    @pl.when((live_ref[v] == 1) & (newt_ref[v] == 1))
    def _():
        y_ref[...] = compute()

    @pl.when((live_ref[v] == 1) & (newt_ref[v] == 0))
    def _():
        rows = lax.broadcasted_iota(I32, y_ref.shape, 0)
        mine = (rows >= lo_ref[v]) & (rows < hi_ref[v])
        y_ref[...] = jnp.where(mine, compute(), y_ref[...])

    @pl.when(fill_ref[v] == 1)
    def _():
        y_ref[...] = jnp.zeros_like(y_ref)


def _moe_experts(xs, sched, w_gate_up, b_gate_up, w_down, b_down, layer):
    a, half = xs.shape
    depth, e, d, f2 = w_gate_up.shape
    f = f2 // 2
    tile = MOE_TILE
    n_visits = a // tile + N_EXPERTS - 1
    grid_spec = pltpu.PrefetchScalarGridSpec(
        num_scalar_prefetch=9,
        grid=(n_visits,),
        in_specs=[pl.BlockSpec((tile, half), lambda v, t, ex, *_: (t[v], 0)),
                  pl.BlockSpec(memory_space=pl.ANY),
                  pl.BlockSpec((None, 1, 1, f2), lambda v, t, ex, *_: (layer, ex[v], 0, 0)),
                  pl.BlockSpec(memory_space=pl.ANY),
                  pl.BlockSpec((None, 1, 1, d), lambda v, t, ex, *_: (layer, ex[v], 0, 0))],
        out_specs=pl.BlockSpec((tile, half), lambda v, t, ex, *_: (t[v], 0)),
        scratch_shapes=[pltpu.VMEM((d, f2), F32), pltpu.VMEM((f, d), F32),
                        pltpu.VMEM((d, f2), BF16), pltpu.VMEM((f, d), BF16),
                        pltpu.SemaphoreType.DMA((2,))],
    )
    return pl.pallas_call(
        functools.partial(_moe_kernel, layer=layer),
        grid_spec=grid_spec,
        out_shape=jax.ShapeDtypeStruct((a, half), U32),
        compiler_params=_cparams("arbitrary"),
        name="moe_experts",
    )(*sched, xs, w_gate_up, b_gate_up.reshape(depth, e, 1, f2), w_down, b_down.reshape(depth, e, 1, d))


def _combine_kernel(tcnt_ref, trun_ref, counts_ref, ys_ref, route_ref, x_ref, mod_ref, fw_ref,
                    o_ref, y_buf, starts_ref, sem, *, final_norm):
    i = pl.program_id(0)
    n = pl.num_programs(0)
    tm, d = x_ref.shape
    slot = lax.rem(i, 2)

    def copies(tile, s, act):
        def visit(dst, src, nrows):
            act(pltpu.make_async_copy(ys_ref.at[pl.ds(src, nrows)], y_buf.at[s, pl.ds(dst, nrows)], sem.at[s]))
        _segment_copies(tile, tcnt_ref, trun_ref, starts_ref, visit)

    @pl.when(i == 0)
    def _():
        _expert_starts(counts_ref, starts_ref)
        y_buf[...] = jnp.zeros_like(y_buf)
        copies(0, 0, lambda c: c.start())

    @pl.when(i + 1 < n)
    def _():
        copies(i + 1, 1 - slot, lambda c: c.start())

    copies(i, slot, lambda c: c.wait())
    y_lo, y_hi = _unpack_pair(y_buf[slot])
    route = route_ref[...]
    r = lax.broadcasted_iota(I32, (tm, SEG_ROWS), 1).astype(F32)
    w = jnp.zeros((tm, SEG_ROWS), F32)
    for kk in range(TOP_K):
        w = jnp.where(r == route[:, kk:kk + 1], route[:, TOP_K + kk:TOP_K + kk + 1], w)
    w = w.astype(BF16)
    y = jnp.concatenate([_dot(w, y_lo.astype(BF16)), _dot(w, y_hi.astype(BF16))], axis=1)
    out = x_ref[...] + mod_ref[0][5:6] * y
    if final_norm:
        out = _rms(out, fw_ref[...])
    o_ref[...] = out


def _combine(ys, route, tcnt, trun, counts, x1, mod, final_w, seq, final_norm):
    t, d = x1.shape
    tm = ROUTE_TILE
    per_seq = seq // tm
    smem = pl.BlockSpec(memory_space=pltpu.SMEM)
    row = lambda n: pl.BlockSpec((tm, n), lambda i: (i, 0))
    return pl.pallas_call(
        functools.partial(_combine_kernel, final_norm=final_norm),
        grid=(t // tm,),
        in_specs=[smem, smem, smem, pl.BlockSpec(memory_space=pl.ANY), row(LANES), row(d),
                  pl.BlockSpec((1, N_MOD, d), lambda i: (i // per_seq, 0, 0)),
                  pl.BlockSpec((1, d), lambda i: (0, 0))],
        out_specs=row(d),
        out_shape=jax.ShapeDtypeStruct((t, d), F32),
        scratch_shapes=[pltpu.VMEM((2, SEG_ROWS, d // 2), U32), pltpu.SMEM((N_EXPERTS,), I32),
                        pltpu.SemaphoreType.DMA((2,))],
        compiler_params=_cparams("arbitrary"),
        name="moe_combine",
    )(tcnt, trun, counts, ys, route, x1, mod, final_w.reshape(1, d))


def _moe_block(o, x, mod, w_o, norm_w, router_w, router_b, w_gate_up, b_gate_up, w_down, b_down,
               final_w, seq, layer, final_norm):
    x1, h, route, tcnt, trun, cnt = _post_mixer(o, x, mod, w_o, norm_w, router_w, router_b, seq)
    n_tiles = tcnt.shape[0]
    tcnt = tcnt.reshape(n_tiles, LANES)
    trun = trun.reshape(n_tiles, LANES)
    counts = cnt[0, :N_EXPERTS]
    xs = _dispatch(h, route, tcnt, trun, counts)
    sched = _moe_schedule(counts, xs.shape[0] // MOE_TILE, MOE_TILE)
    ys = _moe_experts(xs, sched, w_gate_up, b_gate_up, w_down, b_down, layer)
    return _combine(ys, route, tcnt, trun, counts, x1, mod, final_w, seq, final_norm)


def _gdn_conv_kernel(x_ref, mod_ref, nw_ref, win_ref, cw_ref, q_ref, k_ref, v_ref, buf_ref, *, per_seq):
    tm = x_ref.shape[0]
    nqk = q_ref.shape[1]
    nconv = win_ref.shape[1]
    i = pl.program_id(0)
    mod = mod_ref[0]
    h = _norm_mod(x_ref[...], nw_ref[...], mod[0:1], mod[1:2])

    @pl.when(i % per_seq == 0)
    def _():
        buf_ref[0:SUBLANES, :] = jnp.zeros((SUBLANES, nconv), F32)

    buf_ref[SUBLANES:SUBLANES + tm, :] = _dot(h.astype(BF16), win_ref[...])
    cw = cw_ref[...]
    window = buf_ref[0:SUBLANES + tm, :]
    acc = cw[GDN_CONV - 1:GDN_CONV, :] * window[SUBLANES:, :]
    for back in range(1, GDN_CONV):
        acc = acc + cw[GDN_CONV - 1 - back:GDN_CONV - back, :] * pltpu.roll(window, back, 0)[SUBLANES:, :]
    buf_ref[0:SUBLANES, :] = buf_ref[tm:tm + SUBLANES, :]
    y = acc * _sigmoid(acc)

    def l2(a):
        return a * lax.rsqrt(jnp.sum(a * a, axis=-1, keepdims=True) + 1e-6)

    for hh in range(nqk // GDN_HEAD_DIM):
        s = slice(hh * GDN_HEAD_DIM, (hh + 1) * GDN_HEAD_DIM)
        q_ref[:, s] = (l2(y[:, s]) * GDN_HEAD_DIM ** -0.5).astype(BF16)
        k_ref[:, s] = l2(y[:, nqk + hh * GDN_HEAD_DIM:nqk + (hh + 1) * GDN_HEAD_DIM]).astype(BF16)
    v_ref[...] = y[:, 2 * nqk:].astype(BF16)


def _gdn_gate_kernel(x_ref, mod_ref, nw_ref, win_ref, alog_ref, dtb_ref, z_ref, bg_ref):
    nv = z_ref.shape[1]
    mod = mod_ref[0]
    h = _norm_mod(x_ref[...], nw_ref[...], mod[0:1], mod[1:2])
    pr = _dot(h.astype(BF16), win_ref[...])
    z_ref[...] = pr[:, :nv].astype(BF16)
    ba = pr[:, nv:]
    sp = ba + dtb_ref[...]
    softplus = jnp.maximum(sp, 0.0) + jnp.log(1.0 + jnp.exp(-jnp.abs(sp)))
    lane = lax.broadcasted_iota(I32, (1, LANES), 1)
    bg_ref[...] = jnp.where(lane < GDN_V_HEADS, _sigmoid(ba), -jnp.exp(alog_ref[...]) * softplus)


def _gdn_proj(x, mod, norm_w, w_in, conv_w, a_log, dt_bias, seq):
    t, d = x.shape
    tm = TOKEN_TILE
    per_seq = seq // tm
    nqk = GDN_K_HEADS * GDN_HEAD_DIM
    nv = GDN_V_HEADS * GDN_HEAD_DIM
    nconv = 2 * nqk + nv
    w_conv = w_in[:, :nconv].astype(BF16)
    w_gate = jnp.zeros((d, nv + LANES), BF16).at[:, :w_in.shape[1] - nconv].set(w_in[:, nconv:].astype(BF16))
    pad = jnp.zeros((1, LANES), F32)
    alog = pad.at[0, GDN_V_HEADS:2 * GDN_V_HEADS].set(a_log)
    dtb = pad.at[0, GDN_V_HEADS:2 * GDN_V_HEADS].set(dt_bias)
    row = lambda n: pl.BlockSpec((tm, n), lambda i: (i, 0))
    const = lambda a: pl.BlockSpec(a.shape, lambda i: (0,) * a.ndim)
    modspec = pl.BlockSpec((1, N_MOD, d), lambda i: (i // per_seq, 0, 0))
    nw = norm_w.reshape(1, d)
    q, k, v = pl.pallas_call(
        functools.partial(_gdn_conv_kernel, per_seq=per_seq),
        grid=(t // tm,),
        in_specs=[row(d), modspec, const(nw), const(w_conv), const(conv_w)],
        out_specs=[row(nqk), row(nqk), row(nv)],
        out_shape=[jax.ShapeDtypeStruct((t, nqk), BF16), jax.ShapeDtypeStruct((t, nqk), BF16),
                   jax.ShapeDtypeStruct((t, nv), BF16)],
        scratch_shapes=[pltpu.VMEM((tm + 2 * SUBLANES, nconv), F32)],
        compiler_params=_cparams("arbitrary"),
        name="gdn_conv_proj",
    )(x, mod, nw, w_conv, conv_w)
    z, bg = pl.pallas_call(
        _gdn_gate_kernel,
        grid=(t // tm,),
        in_specs=[row(d), modspec, const(nw), const(w_gate), const(alog), const(dtb)],
        out_specs=[row(nv), row(LANES)],
        out_shape=[jax.ShapeDtypeStruct((t, nv), BF16), jax.ShapeDtypeStruct((t, LANES), F32)],
        compiler_params=_cparams("arbitrary"),
        name="gdn_gate_proj",
    )(x, mod, nw, w_gate, alog, dtb)
    return q, k, v, z, bg


def _head_col(block, lane_idx):
    lane = lax.broadcasted_iota(I32, (1, LANES), 1)
    col = jnp.sum(jnp.where(lane == lane_idx, block, 0.0), axis=-1, keepdims=True)
    return jnp.broadcast_to(col, block.shape)


def _chunk_cumsum(x):
    rin = lax.broadcasted_iota(I32, x.shape, 0) & (GDN_CHUNK - 1)
    s = 1
    while s < GDN_CHUNK:
        x = x + jnp.where(rin >= s, pltpu.roll(x, s, 0), 0.0)
        s *= 2
    return x


def _diag_blocks(m):
    c = GDN_CHUNK
    return jnp.concatenate([m[b * c:(b + 1) * c, b * c:(b + 1) * c] for b in range(m.shape[0] // c)], axis=0)


def _chunk_decay(gc):
    n = gc.shape[0]
    c = GDN_CHUNK
    gt = gc.T
    gj = jnp.concatenate([jnp.broadcast_to(gt[0:1, b * c:(b + 1) * c], (c, c)) for b in range(n // c)], axis=0)
    i_in = lax.broadcasted_iota(I32, (n, c), 0) & (c - 1)
    col = lax.broadcasted_iota(I32, (n, c), 1)
    lower = col <= i_in
    decay = jnp.where(lower, jnp.exp(jnp.where(lower, gc[:, :c] - gj, 0.0)), 0.0)
    return decay, col < i_in


def _gdn_l_kernel(k_ref, bg_ref, l_ref):
    kh = pl.program_id(1)
    sc = GDN_SUPER
    c = GDN_CHUNK

    def body(s, carry):
        r0 = pl.multiple_of(s * sc, sc)
        k = k_ref[pl.ds(r0, sc), :]
        bg = bg_ref[pl.ds(r0, sc), :]
        kk = _diag_blocks(_dot_nt(k, k))
        reps = range(GDN_REP)
        beta = [_head_col(bg, kh * GDN_REP + r) for r in reps]
        g = [_head_col(bg, GDN_V_HEADS + kh * GDN_REP + r) for r in reps]
        gc = [_chunk_cumsum(x) for x in g]
        dec = [_chunk_decay(x) for x in gc]
        for r in reps:
            decay, strict = dec[r]
            l_ref[0, r, pl.ds(r0, sc), :] = jnp.where(strict, kk * beta[r][:, :c] * decay, 0.0)
        return carry
    lax.fori_loop(0, k_ref.shape[0] // sc, body, 0)


def _gdn_l(k, bg, batch, seq):
    return pl.pallas_call(
        _gdn_l_kernel,
        grid=(batch, GDN_K_HEADS),
        in_specs=[pl.BlockSpec((seq, GDN_HEAD_DIM), lambda b, h: (b, h)),
                  pl.BlockSpec((seq, LANES), lambda b, h: (b, 0))],
        out_specs=pl.BlockSpec((1, GDN_REP, seq, GDN_CHUNK), lambda b, h: (b, h, 0, 0)),
        out_shape=jax.ShapeDtypeStruct((batch, GDN_V_HEADS, seq, GDN_CHUNK), F32),
        compiler_params=_cparams("arbitrary", "arbitrary"),
        name="gdn_chunk_l",
    )(k, bg)


def _tri_inv_kernel(a_ref, x_ref):
    c, n = a_ref.shape[1], a_ref.shape[2]
    sub_row = lax.broadcasted_iota(I32, (SUBLANES, n), 0)
    for c0 in range(0, c, SUBLANES):
        cols = slice(c0, c0 + SUBLANES)
        for i in range(c):
            if i < c0:
                x_ref[i, cols, :] = jnp.zeros((SUBLANES, n), F32)
                continue

            def sub(j, acc, i=i, cols=cols):
                return acc - a_ref[i, pl.ds(j, 1), :] * x_ref[j, cols, :]
            row = jnp.where(sub_row == i - c0, 1.0, 0.0)
            if i > c0:
                row = lax.fori_loop(c0, i, sub, row, unroll=min(i - c0, 8))
            x_ref[i, cols, :] = row


def _tri_inv(a):
    c, _, n = a.shape
    lanes = min(INV_LANES, n)
    spec = pl.BlockSpec((c, c, lanes), lambda i: (0, 0, i))
    return pl.pallas_call(
        _tri_inv_kernel,
        grid=(n // lanes,),
        in_specs=[spec],
        out_specs=spec,
        out_shape=jax.ShapeDtypeStruct(a.shape, F32),
        compiler_params=_cparams("arbitrary"),
        name="gdn_tri_inv",
    )(a)


def _gdn_scan_kernel(q_ref, k_ref, v_ref, z_ref, bg_ref, ti_ref, nw_ref, o_ref):
    grp = pl.program_id(1)
    sc = GDN_SUPER
    nc = sc // GDN_CHUNK
    hd = GDN_HEAD_DIM
    n_super = q_ref.shape[0] // sc
    nw = nw_ref[...]
    r_blk = lax.broadcasted_iota(I32, (sc, sc), 0) // GDN_CHUNK
    c_blk = lax.broadcasted_iota(I32, (sc, sc), 1) // GDN_CHUNK
    same = r_blk == c_blk

    heads = range(GDN_SCAN_HEADS)

    def prepare_head(s, j, shared):
        r0 = pl.multiple_of(s * sc, sc)
        kh = j // GDN_REP
        if kh not in shared:
            qb = q_ref[pl.ds(r0, sc), kh * hd:(kh + 1) * hd]
            kb = k_ref[pl.ds(r0, sc), kh * hd:(kh + 1) * hd]
            shared[kh] = (qb.astype(F32), kb.astype(F32), _diag_blocks(_dot_nt(qb, kb)))
        q, k, qk = shared[kh]
        head = grp * GDN_SCAN_HEADS + j
        bg = bg_ref[pl.ds(r0, sc), :]
        v = v_ref[pl.ds(r0, sc), j * hd:(j + 1) * hd].astype(F32)
        beta = _head_col(bg, head)
        gc = _chunk_cumsum(_head_col(bg, GDN_V_HEADS + head))
        egc = jnp.exp(gc)
        decay, _ = _chunk_decay(gc)
        a_intra = (qk * decay).astype(BF16)
        ti = ti_ref[0, j, pl.ds(r0, sc), :]
        t_inv = jnp.where(same, jnp.concatenate([ti] * nc, axis=1), 0.0).astype(BF16)
        uw = _dot(t_inv, jnp.concatenate([v * beta, k * beta * egc], axis=1).astype(BF16))
        g_last = [gc[(c + 1) * GDN_CHUNK - 1:(c + 1) * GDN_CHUNK, :] for c in range(nc)]
        gl_rows = jnp.concatenate([jnp.broadcast_to(g, (GDN_CHUNK, LANES)) for g in g_last], axis=0)
        k_dec_t = (k * jnp.exp(gl_rows - gc)).T.astype(BF16)
        eg_last = jnp.concatenate([jnp.exp(g) for g in g_last], axis=0)
        return uw[:, :hd], uw[:, hd:].astype(BF16), (q * egc).astype(BF16), k_dec_t, a_intra, eg_last

    def prepare(s):
        shared = {}
        return tuple(prepare_head(s, j, shared) for j in heads)

    def body(s, carry):
        states, prepared = carry
        states = list(states)
        r0 = pl.multiple_of(s * sc, sc)
        nxt = jnp.minimum(s + 1, n_super - 1)
        shared = {}
        following = []
        outs = [[] for _ in heads]
        for c in range(nc):
            rs = slice(c * GDN_CHUNK, (c + 1) * GDN_CHUNK)
            sb = [states[j].astype(BF16) for j in heads]
            v_new = [prepared[j][0][rs] - _dot(prepared[j][1][rs], sb[j]) for j in heads]
            vb = [v.astype(BF16) for v in v_new]
            for j in heads:
                outs[j].append(_dot(prepared[j][2][rs], sb[j]) + _dot(prepared[j][4][rs], vb[j]))
            states = [states[j] * prepared[j][5][c:c + 1, :] + _dot(prepared[j][3][:, rs], vb[j]) for j in heads]
            following.append(prepare_head(nxt, c, shared))
        for j in heads:
            o = jnp.concatenate(outs[j], axis=0)
            z = z_ref[pl.ds(r0, sc), j * hd:(j + 1) * hd].astype(F32)
            o_ref[pl.ds(r0, sc), j * hd:(j + 1) * hd] = (_rms(o, nw) * (z * _sigmoid(z))).astype(o_ref.dtype)
        return tuple(states), tuple(following)

    assert GDN_SCAN_HEADS == nc
    init = tuple(jnp.zeros((hd, hd), F32) for _ in heads)
    lax.fori_loop(0, n_super, body, (init, prepare(0)))


def _gdn_scan(q, k, v, z, bg, t_inv, norm_w, batch, seq):
    hd = GDN_HEAD_DIM
    nh = GDN_SCAN_HEADS
    kspec = pl.BlockSpec((seq, nh // GDN_REP * hd), lambda b, g: (b, g))
    vspec = pl.BlockSpec((seq, nh * hd), lambda b, g: (b, g))
    return pl.pallas_call(
        _gdn_scan_kernel,
        grid=(batch, GDN_V_HEADS // nh),
        in_specs=[kspec, kspec, vspec, vspec, pl.BlockSpec((seq, LANES), lambda b, g: (b, 0)),
                  pl.BlockSpec((1, nh, seq, GDN_CHUNK), lambda b, g: (b, g, 0, 0)),
                  pl.BlockSpec((1, hd), lambda b, g: (0, 0))],
        out_specs=vspec,
        out_shape=jax.ShapeDtypeStruct((batch * seq, GDN_V_HEADS * hd), BF16),
        compiler_params=_cparams("arbitrary", "arbitrary"),
        name="gdn_scan",
    )(q, k, v, z, bg, t_inv, norm_w.reshape(1, hd))


def _gdn_mixer(x, mod, norm_w, w_in, conv_w, a_log, dt_bias, gnorm_w, batch, seq):
    q, k, v, z, bg = _gdn_proj(x, mod, norm_w, w_in, conv_w, a_log, dt_bias, seq)
    c = GDN_CHUNK
    l_c = _gdn_l(k, bg, batch, seq)
    n_chunks = batch * GDN_V_HEADS * (seq // c)
    a = l_c.reshape(n_chunks, c, c).transpose(1, 2, 0)
    t_inv = _tri_inv(a).transpose(2, 0, 1).reshape(batch, GDN_V_HEADS, seq, c)
    return _gdn_scan(q, k, v, z, bg, t_inv, gnorm_w, batch, seq)


def kernel(x, c, positions, norm_mix, norm_ffn, ada_w, ada_b, mla_w_in, mla_q_norm, mla_w_uq,
           mla_kv_norm, mla_w_ukv, mla_w_o, gdn_w_in, gdn_conv_w, gdn_a_log, gdn_dt_bias, gdn_norm_w,
           gdn_w_o, router_w, router_b, moe_w_gate_up, moe_b_gate_up, moe_w_down, moe_b_down, final_norm):
    batch, seq, d = x.shape
    depth = ada_w.shape[0]
    xt = x.reshape(batch * seq, d)
    mod = _adaln_mod(c, ada_w, ada_b)
    cos, sin = _rope_tables(positions)
    for layer in range(depth):
        j = layer // 2
        if layer % 2 == 0:
            weights = _mla_weights(mla_w_in[j], mla_w_uq[j], mla_w_ukv[j])
            q, k, v = _mla_proj(xt, mod[layer], norm_mix[layer], weights, mla_q_norm[j],
                                mla_kv_norm[j], cos, sin, seq)
            o = _attention(q, k, v, batch, seq)
            w_o = mla_w_o[j]
        else:
            o = _gdn_mixer(xt, mod[layer], norm_mix[layer], gdn_w_in[j], gdn_conv_w[j], gdn_a_log[j],
                           gdn_dt_bias[j], gdn_norm_w[j], batch, seq)
            w_o = gdn_w_o[j]
        xt = _moe_block(o, xt, mod[layer], w_o, norm_ffn[layer], router_w[layer], router_b[layer],
                        moe_w_gate_up, moe_b_gate_up, moe_w_down, moe_b_down, final_norm, seq, layer,
                        layer == depth - 1)
    return xt.reshape(batch, seq, d)
```

```python
import functools
import math

import jax
import jax.numpy as jnp
from jax import lax
from jax.experimental import pallas as pl
from jax.experimental.pallas import tpu as pltpu

F32, BF16, I32, U32 = jnp.float32, jnp.bfloat16, jnp.int32, jnp.uint32

NORM_EPS = 1e-6
N_MOD = 6
MLA_HEADS = 8
QK_NOPE_DIM = 128
QK_ROPE_DIM = 64
V_HEAD_DIM = 128
Q_LORA_RANK = 384
KV_LORA_RANK = 256
ROPE_THETA = 10000.0
GDN_K_HEADS = 8
GDN_V_HEADS = 16
GDN_HEAD_DIM = 128
GDN_CONV = 4
GDN_CHUNK = 64
N_EXPERTS = 32
TOP_K = 4
SWIGLU_LIMIT = 7.0
SWIGLU_ALPHA = 1.702

LANES = 128
SUBLANES = 8
VMEM_LIMIT = 56 * 1024 * 1024

TOKEN_TILE = 256
ROUTE_TILE = 512
ATTN_TILE = 256
ATTN_HEADS = 4
MOE_TILE = 512
GDN_SUPER = 4 * GDN_CHUNK
GDN_REP = GDN_V_HEADS // GDN_K_HEADS
GDN_SCAN_HEADS = 4
INV_LANES = 512
NEG_BIG = -1e30
SEG_PIECES = tuple(1 << b for b in range(ROUTE_TILE.bit_length() - 1, SUBLANES.bit_length() - 2, -1))
SEG_ROWS = ROUTE_TILE * TOP_K + N_EXPERTS * SUBLANES
GAP_PIECES = tuple(1 << b for b in range(MOE_TILE.bit_length() - 2, SUBLANES.bit_length() - 2, -1))


def _cparams(*sem):
    return pltpu.CompilerParams(dimension_semantics=sem, vmem_limit_bytes=VMEM_LIMIT)


def _sigmoid(x):
    return 1.0 / (1.0 + jnp.exp(-x))


def _rms(x, w):
    return x * lax.rsqrt(jnp.mean(x * x, axis=-1, keepdims=True) + NORM_EPS) * w


def _norm_mod(x, w, shift, scale):
    return _rms(x, w) * (1.0 + scale) + shift


def _dot(a, b):
    return jnp.dot(a, b, preferred_element_type=F32)


def _dot_nt(a, b):
    return lax.dot_general(a, b, (((1,), (1,)), ((), ())), preferred_element_type=F32)


def _dot_f32(a, b):
    return jnp.dot(a, b, preferred_element_type=F32, precision=lax.Precision.HIGHEST)


def _pack_pair(lo, hi):
    ulo = lax.bitcast_convert_type(lo.astype(BF16).astype(F32), U32) >> 16
    uhi = lax.bitcast_convert_type(hi.astype(BF16).astype(F32), U32) & jnp.uint32(0xFFFF0000)
    return ulo | uhi


def _unpack_pair(p):
    lo = lax.bitcast_convert_type(p << 16, F32)
    hi = lax.bitcast_convert_type(p & jnp.uint32(0xFFFF0000), F32)
    return lo, hi


def _mod_kernel(c_ref, w_ref, b_ref, o_ref):
    c = c_ref[...]
    cond = c * _sigmoid(c)
    o_ref[0] = _dot(cond.astype(BF16), w_ref[0].astype(BF16)) + b_ref[0]


def _adaln_mod(c, ada_w, ada_b):
    depth, d, n = ada_w.shape
    b = c.shape[0]
    tn = 1024
    out = pl.pallas_call(
        _mod_kernel,
        grid=(depth, n // tn),
        in_specs=[pl.BlockSpec((b, d), lambda l, j: (0, 0)),
                  pl.BlockSpec((1, d, tn), lambda l, j: (l, 0, j)),
                  pl.BlockSpec((1, 1, tn), lambda l, j: (l, 0, j))],
        out_specs=pl.BlockSpec((1, b, tn), lambda l, j: (l, 0, j)),
        out_shape=jax.ShapeDtypeStruct((depth, b, n), F32),
        compiler_params=_cparams("arbitrary", "arbitrary"),
        name="adaln_mod",
    )(c, ada_w, ada_b.reshape(depth, 1, n))
    return out.reshape(depth, b, N_MOD, d)


def _rope_kernel(pos_ref, cos_ref, sin_ref):
    pos = pos_ref[...].astype(F32)
    lane = lax.broadcasted_iota(I32, (1, LANES), 1)
    j = (lane & (QK_ROPE_DIM // 2 - 1)).astype(F32)
    inv = jnp.exp(j * (-2.0 / QK_ROPE_DIM * math.log(ROPE_THETA)))
    ang = pos * inv
    cos_ref[...] = jnp.cos(ang)
    sin_ref[...] = jnp.sin(ang)


def _rope_tables(positions):
    t = positions.size
    tm = 1024
    spec = pl.BlockSpec((tm, LANES), lambda i: (i, 0))
    return pl.pallas_call(
        _rope_kernel,
        grid=(t // tm,),
        in_specs=[pl.BlockSpec((tm, 1), lambda i: (i, 0))],
        out_specs=[spec, spec],
        out_shape=[jax.ShapeDtypeStruct((t, LANES), F32)] * 2,
        compiler_params=_cparams("arbitrary"),
        name="rope_tables",
    )(positions.reshape(t, 1))


def _mla_proj_kernel(x_ref, mod_ref, nw_ref, win_ref, qn_ref, wqa_ref, wqb_ref, kvn_ref, wkv_ref,
                     cos_ref, sin_ref, q_ref, k_ref, v_ref):
    mod = mod_ref[0]
    h = _norm_mod(x_ref[...], nw_ref[...], mod[0:1], mod[1:2])
    lat = _dot(h.astype(BF16), win_ref[...])
    q_lat = lat[:, :Q_LORA_RANK]
    kv_lat = lat[:, Q_LORA_RANK:Q_LORA_RANK + KV_LORA_RANK]
    kp = lat[:, Q_LORA_RANK + KV_LORA_RANK:]
    cos = cos_ref[...]
    sin = sin_ref[...]
    scale = (QK_NOPE_DIM + QK_ROPE_DIM) ** -0.5

    qn = _rms(q_lat, qn_ref[...]).astype(BF16)
    qa = _dot(qn, wqa_ref[...])
    qb = _dot(qn, wqb_ref[...])
    for hh in range(MLA_HEADS):
        o = hh * 2 * LANES
        q_ref[:, o:o + LANES] = (qa[:, o:o + LANES] * scale).astype(BF16)
        pe = qa[:, o + LANES:o + 2 * LANES] * cos + qb[:, hh * LANES:(hh + 1) * LANES] * sin
        q_ref[:, o + LANES:o + 2 * LANES] = (pe * scale).astype(BF16)

    kvn = _rms(kv_lat, kvn_ref[...]).astype(BF16)
    kv = _dot(kvn, wkv_ref[...])
    lane = lax.broadcasted_iota(I32, (1, LANES), 1)
    first = lane < QK_ROPE_DIM
    u = kp * jnp.where(first, cos, sin)
    kr = jnp.where(first, u + pltpu.roll(u, QK_ROPE_DIM, 1), 0.0).astype(BF16)
    nk = MLA_HEADS * QK_NOPE_DIM
    for hh in range(MLA_HEADS):
        o = hh * 2 * LANES
        k_ref[:, o:o + LANES] = kv[:, hh * LANES:(hh + 1) * LANES].astype(BF16)
        k_ref[:, o + LANES:o + 2 * LANES] = kr
    v_ref[...] = kv[:, nk:].astype(BF16)


def _rotate_half_cols(w):
    half = w.shape[-1] // 2
    return jnp.concatenate([-w[..., half:], w[..., :half]], axis=-1)


def _mla_weights(w_in, w_uq, w_ukv):
    kpe = w_in[:, Q_LORA_RANK + KV_LORA_RANK:]
    w_in_ext = jnp.concatenate([w_in, _rotate_half_cols(kpe)], axis=1).astype(BF16)
    wq = w_uq.reshape(Q_LORA_RANK, MLA_HEADS, QK_NOPE_DIM + QK_ROPE_DIM)
    zeros = jnp.zeros((Q_LORA_RANK, MLA_HEADS, LANES - QK_ROPE_DIM), w_uq.dtype)
    wq_pe = wq[:, :, QK_NOPE_DIM:]
    wqa = jnp.concatenate([wq[:, :, :QK_NOPE_DIM], wq_pe, zeros], axis=2)
    wqb = jnp.concatenate([_rotate_half_cols(wq_pe), zeros], axis=2)
    wkv = w_ukv.reshape(KV_LORA_RANK, MLA_HEADS, QK_NOPE_DIM + V_HEAD_DIM)
    wkv = jnp.concatenate([wkv[:, :, :QK_NOPE_DIM].reshape(KV_LORA_RANK, -1),
                           wkv[:, :, QK_NOPE_DIM:].reshape(KV_LORA_RANK, -1)], axis=1)
    return (w_in_ext, wqa.reshape(Q_LORA_RANK, -1).astype(BF16),
            wqb.reshape(Q_LORA_RANK, -1).astype(BF16), wkv.astype(BF16))


def _mla_proj(x, mod, norm_w, weights, q_norm, kv_norm, cos, sin, seq):
    t, d = x.shape
    tm = TOKEN_TILE
    w_in_ext, wqa, wqb, wkv = weights
    per_seq = seq // tm
    full = lambda a: pl.BlockSpec(a.shape, lambda i: (0,) * a.ndim)
    row = lambda n: pl.BlockSpec((tm, n), lambda i: (i, 0))
    nq = MLA_HEADS * 2 * LANES
    nv = MLA_HEADS * V_HEAD_DIM
    args = (x, mod, norm_w.reshape(1, d), w_in_ext, q_norm.reshape(1, -1), wqa, wqb,
            kv_norm.reshape(1, -1), wkv, cos, sin)
    in_specs = [row(d), pl.BlockSpec((1, N_MOD, d), lambda i: (i // per_seq, 0, 0)), full(args[2]),
                full(w_in_ext), full(args[4]), full(wqa), full(wqb), full(args[7]), full(wkv),
                row(LANES), row(LANES)]
    return pl.pallas_call(
        _mla_proj_kernel,
        grid=(t // tm,),
        in_specs=in_specs,
        out_specs=[row(nq), row(nq), row(nv)],
        out_shape=[jax.ShapeDtypeStruct((t, nq), BF16), jax.ShapeDtypeStruct((t, nq), BF16),
                   jax.ShapeDtypeStruct((t, nv), BF16)],
        compiler_params=_cparams("arbitrary"),
        name="mla_proj",
    )(*args)


def _attn_kernel(q_ref, k_ref, v_ref, o_ref):
    tq = q_ref.shape[0]
    dq = 2 * LANES
    dv = V_HEAD_DIM
    qi = pl.program_id(2)
    heads = range(ATTN_HEADS)
    q = [q_ref[:, h * dq:(h + 1) * dq] for h in heads]

    def step(r0, carry, mask):
        s = [_dot_nt(q[h], k_ref[pl.ds(r0, tq), h * dq:(h + 1) * dq]) for h in heads]
        if mask is not None:
            s = [jnp.where(mask, sh, NEG_BIG) for sh in s]
        m_new = [jnp.maximum(carry[h][0], jnp.max(s[h], axis=-1, keepdims=True)) for h in heads]
        p = [jnp.exp(s[h] - m_new[h]) for h in heads]
        alpha = [jnp.exp(carry[h][0] - m_new[h]) for h in heads]
        l = [alpha[h] * carry[h][1] + jnp.sum(p[h], axis=-1, keepdims=True) for h in heads]
        acc = [alpha[h] * carry[h][2] + _dot(p[h].astype(BF16), v_ref[pl.ds(r0, tq), h * dv:(h + 1) * dv])
               for h in heads]
        return tuple((m_new[h], l[h], acc[h]) for h in heads)

    init = tuple((jnp.full((tq, 1), NEG_BIG, F32), jnp.zeros((tq, 1), F32), jnp.zeros((tq, dv), F32))
                 for _ in heads)
    carry = lax.fori_loop(0, qi, lambda j, c: step(pl.multiple_of(j * tq, tq), c, None), init)
    causal = (lax.broadcasted_iota(I32, (tq, tq), 1) <= lax.broadcasted_iota(I32, (tq, tq), 0))
    carry = step(pl.multiple_of(qi * tq, tq), carry, causal)
    for h in heads:
        _, l, acc = carry[h]
        o_ref[:, h * dv:(h + 1) * dv] = (acc / l).astype(o_ref.dtype)


def _attention(q, k, v, batch, seq):
    t = q.shape[0]
    tq = ATTN_TILE
    nq = seq // tq
    dq = ATTN_HEADS * 2 * LANES
    dv = ATTN_HEADS * V_HEAD_DIM
    return pl.pallas_call(
        _attn_kernel,
        grid=(batch, MLA_HEADS // ATTN_HEADS, nq),
        in_specs=[pl.BlockSpec((tq, dq), lambda b, h, i: (b * nq + i, h)),
                  pl.BlockSpec((seq, dq), lambda b, h, i: (b, h)),
                  pl.BlockSpec((seq, dv), lambda b, h, i: (b, h))],
        out_specs=pl.BlockSpec((tq, dv), lambda b, h, i: (b * nq + i, h)),
        out_shape=jax.ShapeDtypeStruct((t, MLA_HEADS * V_HEAD_DIM), BF16),
        compiler_params=_cparams("arbitrary", "arbitrary", "arbitrary"),
        name="mla_attention",
    )(q, k, v)


def _post_mixer_kernel(o_ref, x_ref, mod_ref, wo_ref, nw_ref, rw_ref, rb_ref,
                       x1_ref, h_ref, route_ref, tcnt_ref, trun_ref, cnt_ref, run_ref):
    tm, d = x_ref.shape
    i = pl.program_id(0)

    @pl.when(i == 0)
    def _():
        run_ref[...] = jnp.zeros_like(run_ref)

    mod = mod_ref[0]
    x1 = x_ref[...] + mod[2:3] * _dot(o_ref[...], wo_ref[...])
    x1_ref[...] = x1
    h = _norm_mod(x1, nw_ref[...], mod[3:4], mod[4:5])
    h_ref[...] = h.astype(BF16)

    h_hi = h.astype(BF16)
    h_lo = (h - h_hi.astype(F32)).astype(BF16)
    parts = _dot(h_hi, rw_ref[...]) + _dot(h_lo, rw_ref[...])
    lane = lax.broadcasted_iota(I32, (tm, LANES), 1)
    logits = jnp.where(lane < N_EXPERTS, parts + pltpu.roll(parts, LANES - N_EXPERTS, 1) + rb_ref[...], NEG_BIG)
    lane_f = lane.astype(F32)
    work = logits
    val, hot = [], []
    for _ in range(TOP_K):
        m = jnp.max(work, axis=-1, keepdims=True)
        a = jnp.min(jnp.where(work == m, lane_f, float(LANES)), axis=-1, keepdims=True)
        hot.append(lane_f == a)
        val.append(m)
        work = jnp.where(hot[-1], -jnp.inf, work)
    ex = [jnp.exp(v - val[0]) for v in val]
    den = ex[0] + ex[1] + ex[2] + ex[3]

    onehot = jnp.where(hot[0] | hot[1] | hot[2] | hot[3], 1.0, 0.0)
    row = lax.broadcasted_iota(I32, (tm, tm), 0)
    col = lax.broadcasted_iota(I32, (tm, tm), 1)
    tri = jnp.where(col < row, 1.0, 0.0).astype(BF16)
    rank = _dot(tri, onehot.astype(BF16))
    cnt = jnp.sum(onehot, axis=0, keepdims=True)
    cnt = jnp.floor((cnt + (SUBLANES - 1)) * (1.0 / SUBLANES)) * SUBLANES
    e_row = lax.broadcasted_iota(I32, (LANES, LANES), 0)
    e_col = lax.broadcasted_iota(I32, (LANES, LANES), 1)
    before = jnp.where(e_row < e_col, 1.0, 0.0)
    off = _dot_f32(jnp.broadcast_to(cnt, (SUBLANES, LANES)), before)[0:1]
    local = off + rank

    route = jnp.zeros((tm, LANES), F32)
    for kk in range(TOP_K):
        pos = jnp.sum(jnp.where(hot[kk], local, 0.0), axis=-1, keepdims=True)
        route = jnp.where(lane == kk, pos, route)
        route = jnp.where(lane == TOP_K + kk, ex[kk] / den, route)
    route_ref[...] = route
    tcnt_ref[0] = cnt.astype(I32)
    trun_ref[0] = run_ref[...].astype(I32)
    run = run_ref[...] + cnt
    run_ref[...] = run
    cnt_ref[...] = run.astype(I32)


def _post_mixer(o, x, mod, w_o, norm_w, router_w, router_b, seq):
    t, d = x.shape
    ko = o.shape[1]
    tm = ROUTE_TILE
    per_seq = seq // tm
    n_tiles = t // tm
    rw_hi = router_w.astype(BF16)
    rw_lo = (router_w - rw_hi.astype(F32)).astype(BF16)
    rw = jnp.zeros((d, LANES), BF16).at[:, :N_EXPERTS].set(rw_hi).at[:, N_EXPERTS:2 * N_EXPERTS].set(rw_lo)
    rb = jnp.zeros((1, LANES), F32).at[0, :N_EXPERTS].set(router_b)
    row = lambda n: pl.BlockSpec((tm, n), lambda i: (i, 0))
    const = lambda r, c: pl.BlockSpec((r, c), lambda i: (0, 0))
    per_tile = pl.BlockSpec((1, 1, LANES), lambda i: (i, 0, 0))
    return pl.pallas_call(
        _post_mixer_kernel,
        grid=(n_tiles,),
        in_specs=[row(ko), row(d), pl.BlockSpec((1, N_MOD, d), lambda i: (i // per_seq, 0, 0)),
                  const(ko, d), const(1, d), const(d, LANES), const(1, LANES)],
        out_specs=[row(d), row(d), row(LANES), per_tile, per_tile, const(1, LANES)],
        out_shape=[jax.ShapeDtypeStruct((t, d), F32), jax.ShapeDtypeStruct((t, d), BF16),
                   jax.ShapeDtypeStruct((t, LANES), F32),
                   jax.ShapeDtypeStruct((n_tiles, 1, LANES), I32),
                   jax.ShapeDtypeStruct((n_tiles, 1, LANES), I32),
                   jax.ShapeDtypeStruct((1, LANES), I32)],
        scratch_shapes=[pltpu.VMEM((1, LANES), F32)],
        compiler_params=_cparams("arbitrary"),
        name="post_mixer_router",
    )(o, x, mod, w_o.astype(BF16), norm_w.reshape(1, d), rw, rb)


def _tile_ceil(n):
    return (n + (MOE_TILE - 1)) // MOE_TILE * MOE_TILE


def _expert_starts(counts_ref, starts_ref):
    def body(e, acc):
        starts_ref[e] = acc
        return acc + _tile_ceil(counts_ref[e])
    return lax.fori_loop(0, N_EXPERTS, body, jnp.int32(0))


def _segment_copies(tile, tcnt_ref, trun_ref, starts_ref, visit):
    def per_expert(e, off):
        n = tcnt_ref[tile, e]
        base = starts_ref[e] + trun_ref[tile, e]
        for piece in SEG_PIECES:
            done = n & (-2 * piece)

            @pl.when((n & piece) != 0)
            def _():
                visit(pl.multiple_of(off + done, SUBLANES), pl.multiple_of(base + done, SUBLANES), piece)
        return off + n
    lax.fori_loop(0, N_EXPERTS, per_expert, jnp.int32(0))


def _dispatch_kernel(tcnt_ref, trun_ref, counts_ref, route_ref, h_ref, xs_ref, g_ref, zero_ref, starts_ref, sem):
    i = pl.program_id(0)
    n = pl.num_programs(0)
    tm, d = h_ref.shape
    slot = lax.rem(i, 2)

    def copies(tile, s, act):
        def visit(src, dst, nrows):
            act(pltpu.make_async_copy(g_ref.at[s, pl.ds(src, nrows)], xs_ref.at[pl.ds(dst, nrows)], sem.at[s]))
        _segment_copies(tile, tcnt_ref, trun_ref, starts_ref, visit)

    start = lambda c: c.start()
    wait = lambda c: c.wait()

    @pl.when(i == 0)
    def _():
        _expert_starts(counts_ref, starts_ref)

    @pl.when(i >= 2)
    def _():
        copies(i - 2, slot, wait)

    rt = route_ref[...].T
    r = lax.broadcasted_iota(I32, (SEG_ROWS, tm), 0).astype(F32)
    hit = (r == rt[0:1]) | (r == rt[1:2]) | (r == rt[2:3]) | (r == rt[3:4])
    g = _dot(jnp.where(hit, 1.0, 0.0).astype(BF16), h_ref[...])
    g_ref[slot] = _pack_pair(g[:, :d // 2], g[:, d // 2:])
    copies(i, slot, start)

    @pl.when(i == n - 1)
    def _():
        copies(i, slot, wait)

        @pl.when(i >= 1)
        def _():
            copies(i - 1, 1 - slot, wait)
        zero_ref[...] = jnp.zeros_like(zero_ref)

        def gap_fill(act):
            def per_expert(e, c):
                cnt = counts_ref[e]
                gap = _tile_ceil(cnt) - cnt
                base = starts_ref[e] + cnt
                for piece in GAP_PIECES:
                    done = gap & (-2 * piece)

                    @pl.when((gap & piece) != 0)
                    def _():
                        act(pltpu.make_async_copy(
                            zero_ref.at[pl.ds(0, piece)],
                            xs_ref.at[pl.ds(pl.multiple_of(base + done, SUBLANES), piece)], sem.at[slot]))
                return c
            lax.fori_loop(0, N_EXPERTS, per_expert, 0)

        used = starts_ref[N_EXPERTS - 1] + _tile_ceil(counts_ref[N_EXPERTS - 1])
        n_unused = (xs_ref.shape[0] - used) // MOE_TILE

        def tile_fill(act):
            def per_tile(kk, c):
                row = pl.multiple_of(used + kk * MOE_TILE, MOE_TILE)
                act(pltpu.make_async_copy(zero_ref, xs_ref.at[pl.ds(row, MOE_TILE)], sem.at[slot]))
                return c
            lax.fori_loop(0, n_unused, per_tile, 0)

        gap_fill(start)
        tile_fill(start)
        gap_fill(wait)
        tile_fill(wait)


def _sorted_rows(t):
    bound = t * TOP_K + (t // ROUTE_TILE) * N_EXPERTS * (SUBLANES - 1) + N_EXPERTS * (MOE_TILE - SUBLANES)
    return (bound + MOE_TILE - 1) // MOE_TILE * MOE_TILE


def _dispatch(h, route, tcnt, trun, counts):
    t, d = h.shape
    tm = ROUTE_TILE
    smem = pl.BlockSpec(memory_space=pltpu.SMEM)
    return pl.pallas_call(
        _dispatch_kernel,
        grid=(t // tm,),
        in_specs=[smem, smem, smem, pl.BlockSpec((tm, LANES), lambda i: (i, 0)),
                  pl.BlockSpec((tm, d), lambda i: (i, 0))],
        out_specs=pl.BlockSpec(memory_space=pl.ANY),
        out_shape=jax.ShapeDtypeStruct((_sorted_rows(t), d // 2), U32),
        scratch_shapes=[pltpu.VMEM((2, SEG_ROWS, d // 2), U32), pltpu.VMEM((MOE_TILE, d // 2), U32),
                        pltpu.SMEM((N_EXPERTS,), I32), pltpu.SemaphoreType.DMA((2,))],
        compiler_params=_cparams("arbitrary"),
        name="moe_dispatch",
    )(tcnt, trun, counts, route, h)


def _moe_schedule(counts, n_tiles, tile):
    ids = jnp.arange(N_EXPERTS, dtype=I32)
    n_own = (counts + tile - 1) // tile
    t_end = jnp.cumsum(n_own)
    t_start = t_end - n_own
    total = t_end[-1]
    later = jnp.where((ids[None, :] > ids[:, None]) & (counts[None, :] > 0), ids[None, :], N_EXPERTS)
    next_e = jnp.min(later, axis=1)
    next_e = jnp.where(next_e == N_EXPERTS, -1, next_e)
    step = jnp.arange(n_tiles, dtype=I32)
    live = step < total
    v = jnp.minimum(step, total - 1)
    e = jnp.minimum(jnp.sum(t_end[None, :] <= v[:, None], axis=1), N_EXPERTS - 1).astype(I32)
    pick = e[:, None] == ids[None, :]
    of_e = lambda a: jnp.sum(jnp.where(pick, a[None, :], 0), axis=1)
    new_expert = (live & (step == of_e(t_start))).astype(I32)
    return e, new_expert, live.astype(I32), of_e(next_e).astype(I32)


def _moe_kernel(exp_ref, newe_ref, live_ref, next_ref,
                x_ref, wgu_hbm, bgu_ref, wd_hbm, bd_ref, y_ref,
                wgu_f32, wd_f32, wgu_bf, wd_bf, sem, *, layer):
    v = pl.program_id(0)
    f = wd_bf.shape[0]
    half = x_ref.shape[1]

    def weight_copies(e):
        return (pltpu.make_async_copy(wgu_hbm.at[layer, e], wgu_f32, sem.at[0]),
                pltpu.make_async_copy(wd_hbm.at[layer, e], wd_f32, sem.at[1]))

    @pl.when(newe_ref[v] == 1)
    def _():
        e = exp_ref[v]

        @pl.when(v == 0)
        def _():
            for c in weight_copies(e):
                c.start()
        for c in weight_copies(e):
            c.wait()
        wgu_bf[...] = wgu_f32[...].astype(BF16)
        wd_bf[...] = wd_f32[...].astype(BF16)
        nxt = next_ref[v]

        @pl.when(nxt >= 0)
        def _():
            for c in weight_copies(nxt):
                c.start()

    @pl.when(live_ref[v] == 1)
    def _():
        x_lo, x_hi = _unpack_pair(x_ref[...])
        gu = (_dot(x_lo.astype(BF16), wgu_bf[:half, :]) + _dot(x_hi.astype(BF16), wgu_bf[half:, :])
              + bgu_ref[0])
        gate = jnp.minimum(gu[:, :f], SWIGLU_LIMIT)
        lin = jnp.clip(gu[:, f:], -SWIGLU_LIMIT, SWIGLU_LIMIT)
        act = gate * _sigmoid(SWIGLU_ALPHA * gate) * (lin + 1.0)
        y = _dot(act.astype(BF16), wd_bf[...]) + bd_ref[0]
        y_ref[...] = _pack_pair(y[:, :half], y[:, half:])

    @pl.when(live_ref[v] == 0)
    def _():
        y_ref[...] = jnp.zeros_like(y_ref)


def _moe_experts(xs, sched, w_gate_up, b_gate_up, w_down, b_down, layer):
    a, half = xs.shape
    depth, e, d, f2 = w_gate_up.shape
    f = f2 // 2
    tile = MOE_TILE
    grid_spec = pltpu.PrefetchScalarGridSpec(
        num_scalar_prefetch=4,
        grid=(a // tile,),
        in_specs=[pl.BlockSpec((tile, half), lambda v, ex, *_: (v, 0)),
                  pl.BlockSpec(memory_space=pl.ANY),
                  pl.BlockSpec((None, 1, 1, f2), lambda v, ex, *_: (layer, ex[v], 0, 0)),
                  pl.BlockSpec(memory_space=pl.ANY),
                  pl.BlockSpec((None, 1, 1, d), lambda v, ex, *_: (layer, ex[v], 0, 0))],
        out_specs=pl.BlockSpec((tile, half), lambda v, ex, *_: (v, 0)),
        scratch_shapes=[pltpu.VMEM((d, f2), F32), pltpu.VMEM((f, d), F32),
                        pltpu.VMEM((d, f2), BF16), pltpu.VMEM((f, d), BF16),
                        pltpu.SemaphoreType.DMA((2,))],
    )
    return pl.pallas_call(
        functools.partial(_moe_kernel, layer=layer),
        grid_spec=grid_spec,
        out_shape=jax.ShapeDtypeStruct((a, half), U32),
        compiler_params=_cparams("arbitrary"),
        name="moe_experts",
    )(*sched, xs, w_gate_up, b_gate_up.reshape(depth, e, 1, f2), w_down, b_down.reshape(depth, e, 1, d))


def _combine_kernel(tcnt_ref, trun_ref, counts_ref, ys_ref, route_ref, x_ref, mod_ref, fw_ref,
                    o_ref, y_buf, starts_ref, sem, *, final_norm):
    i = pl.program_id(0)
    n = pl.num_programs(0)
    tm, d = x_ref.shape
    slot = lax.rem(i, 2)

    def copies(tile, s, act):
        def visit(dst, src, nrows):
            act(pltpu.make_async_copy(ys_ref.at[pl.ds(src, nrows)], y_buf.at[s, pl.ds(dst, nrows)], sem.at[s]))
        _segment_copies(tile, tcnt_ref, trun_ref, starts_ref, visit)

    @pl.when(i == 0)
    def _():
        _expert_starts(counts_ref, starts_ref)
        y_buf[...] = jnp.zeros_like(y_buf)
        copies(0, 0, lambda c: c.start())

    @pl.when(i + 1 < n)
    def _():
        copies(i + 1, 1 - slot, lambda c: c.start())

    copies(i, slot, lambda c: c.wait())
    y_lo, y_hi = _unpack_pair(y_buf[slot])
    route = route_ref[...]
    r = lax.broadcasted_iota(I32, (tm, SEG_ROWS), 1).astype(F32)
    w = jnp.zeros((tm, SEG_ROWS), F32)
    for kk in range(TOP_K):
        w = jnp.where(r == route[:, kk:kk + 1], route[:, TOP_K + kk:TOP_K + kk + 1], w)
    w = w.astype(BF16)
    y = jnp.concatenate([_dot(w, y_lo.astype(BF16)), _dot(w, y_hi.astype(BF16))], axis=1)
    out = x_ref[...] + mod_ref[0][5:6] * y
    if final_norm:
        out = _rms(out, fw_ref[...])
    o_ref[...] = out


def _combine(ys, route, tcnt, trun, counts, x1, mod, final_w, seq, final_norm):
    t, d = x1.shape
    tm = ROUTE_TILE
    per_seq = seq // tm
    smem = pl.BlockSpec(memory_space=pltpu.SMEM)
    row = lambda n: pl.BlockSpec((tm, n), lambda i: (i, 0))
    return pl.pallas_call(
        functools.partial(_combine_kernel, final_norm=final_norm),
        grid=(t // tm,),
        in_specs=[smem, smem, smem, pl.BlockSpec(memory_space=pl.ANY), row(LANES), row(d),
                  pl.BlockSpec((1, N_MOD, d), lambda i: (i // per_seq, 0, 0)),
                  pl.BlockSpec((1, d), lambda i: (0, 0))],
        out_specs=row(d),
        out_shape=jax.ShapeDtypeStruct((t, d), F32),
        scratch_shapes=[pltpu.VMEM((2, SEG_ROWS, d // 2), U32), pltpu.SMEM((N_EXPERTS,), I32),
                        pltpu.SemaphoreType.DMA((2,))],
        compiler_params=_cparams("arbitrary"),
        name="moe_combine",
    )(tcnt, trun, counts, ys, route, x1, mod, final_w.reshape(1, d))


def _moe_block(o, x, mod, w_o, norm_w, router_w, router_b, w_gate_up, b_gate_up, w_down, b_down,
               final_w, seq, layer, final_norm):
    x1, h, route, tcnt, trun, cnt = _post_mixer(o, x, mod, w_o, norm_w, router_w, router_b, seq)
    n_tiles = tcnt.shape[0]
    tcnt = tcnt.reshape(n_tiles, LANES)
    trun = trun.reshape(n_tiles, LANES)
    counts = cnt[0, :N_EXPERTS]
    xs = _dispatch(h, route, tcnt, trun, counts)
    sched = _moe_schedule(counts, xs.shape[0] // MOE_TILE, MOE_TILE)
    ys = _moe_experts(xs, sched, w_gate_up, b_gate_up, w_down, b_down, layer)
    return _combine(ys, route, tcnt, trun, counts, x1, mod, final_w, seq, final_norm)


def _gdn_conv_kernel(x_ref, mod_ref, nw_ref, win_ref, cw_ref, q_ref, k_ref, v_ref, buf_ref, *, per_seq):
    tm = x_ref.shape[0]
    nqk = q_ref.shape[1]
    nconv = win_ref.shape[1]
    i = pl.program_id(0)
    mod = mod_ref[0]
    h = _norm_mod(x_ref[...], nw_ref[...], mod[0:1], mod[1:2])

    @pl.when(i % per_seq == 0)
    def _():
        buf_ref[0:SUBLANES, :] = jnp.zeros((SUBLANES, nconv), F32)

    buf_ref[SUBLANES:SUBLANES + tm, :] = _dot(h.astype(BF16), win_ref[...])
    cw = cw_ref[...]
    window = buf_ref[0:SUBLANES + tm, :]
    acc = cw[GDN_CONV - 1:GDN_CONV, :] * window[SUBLANES:, :]
    for back in range(1, GDN_CONV):
        acc = acc + cw[GDN_CONV - 1 - back:GDN_CONV - back, :] * pltpu.roll(window, back, 0)[SUBLANES:, :]
    buf_ref[0:SUBLANES, :] = buf_ref[tm:tm + SUBLANES, :]
    y = acc * _sigmoid(acc)

    def l2(a):
        return a * lax.rsqrt(jnp.sum(a * a, axis=-1, keepdims=True) + 1e-6)

    for hh in range(nqk // GDN_HEAD_DIM):
        s = slice(hh * GDN_HEAD_DIM, (hh + 1) * GDN_HEAD_DIM)
        q_ref[:, s] = (l2(y[:, s]) * GDN_HEAD_DIM ** -0.5).astype(BF16)
        k_ref[:, s] = l2(y[:, nqk + hh * GDN_HEAD_DIM:nqk + (hh + 1) * GDN_HEAD_DIM]).astype(BF16)
    v_ref[...] = y[:, 2 * nqk:].astype(BF16)


def _gdn_gate_kernel(x_ref, mod_ref, nw_ref, win_ref, alog_ref, dtb_ref, z_ref, bg_ref):
    nv = z_ref.shape[1]
    mod = mod_ref[0]
    h = _norm_mod(x_ref[...], nw_ref[...], mod[0:1], mod[1:2])
    pr = _dot(h.astype(BF16), win_ref[...])
    z_ref[...] = pr[:, :nv].astype(BF16)
    ba = pr[:, nv:]
    sp = ba + dtb_ref[...]
    softplus = jnp.maximum(sp, 0.0) + jnp.log(1.0 + jnp.exp(-jnp.abs(sp)))
    lane = lax.broadcasted_iota(I32, (1, LANES), 1)
    bg_ref[...] = jnp.where(lane < GDN_V_HEADS, _sigmoid(ba), -jnp.exp(alog_ref[...]) * softplus)


def _gdn_proj(x, mod, norm_w, w_in, conv_w, a_log, dt_bias, seq):
    t, d = x.shape
    tm = TOKEN_TILE
    per_seq = seq // tm
    nqk = GDN_K_HEADS * GDN_HEAD_DIM
    nv = GDN_V_HEADS * GDN_HEAD_DIM
    nconv = 2 * nqk + nv
    w_conv = w_in[:, :nconv].astype(BF16)
    w_gate = jnp.zeros((d, nv + LANES), BF16).at[:, :w_in.shape[1] - nconv].set(w_in[:, nconv:].astype(BF16))
    pad = jnp.zeros((1, LANES), F32)
    alog = pad.at[0, GDN_V_HEADS:2 * GDN_V_HEADS].set(a_log)
    dtb = pad.at[0, GDN_V_HEADS:2 * GDN_V_HEADS].set(dt_bias)
    row = lambda n: pl.BlockSpec((tm, n), lambda i: (i, 0))
    const = lambda a: pl.BlockSpec(a.shape, lambda i: (0,) * a.ndim)
    modspec = pl.BlockSpec((1, N_MOD, d), lambda i: (i // per_seq, 0, 0))
    nw = norm_w.reshape(1, d)
    q, k, v = pl.pallas_call(
        functools.partial(_gdn_conv_kernel, per_seq=per_seq),
        grid=(t // tm,),
        in_specs=[row(d), modspec, const(nw), const(w_conv), const(conv_w)],
        out_specs=[row(nqk), row(nqk), row(nv)],
        out_shape=[jax.ShapeDtypeStruct((t, nqk), BF16), jax.ShapeDtypeStruct((t, nqk), BF16),
                   jax.ShapeDtypeStruct((t, nv), BF16)],
        scratch_shapes=[pltpu.VMEM((tm + 2 * SUBLANES, nconv), F32)],
        compiler_params=_cparams("arbitrary"),
        name="gdn_conv_proj",
    )(x, mod, nw, w_conv, conv_w)
    z, bg = pl.pallas_call(
        _gdn_gate_kernel,
        grid=(t // tm,),
        in_specs=[row(d), modspec, const(nw), const(w_gate), const(alog), const(dtb)],
        out_specs=[row(nv), row(LANES)],
        out_shape=[jax.ShapeDtypeStruct((t, nv), BF16), jax.ShapeDtypeStruct((t, LANES), F32)],
        compiler_params=_cparams("arbitrary"),
        name="gdn_gate_proj",
    )(x, mod, nw, w_gate, alog, dtb)
    return q, k, v, z, bg


def _head_col(block, lane_idx):
    lane = lax.broadcasted_iota(I32, (1, LANES), 1)
    col = jnp.sum(jnp.where(lane == lane_idx, block, 0.0), axis=-1, keepdims=True)
    return jnp.broadcast_to(col, block.shape)


def _chunk_cumsum(x):
    rin = lax.broadcasted_iota(I32, x.shape, 0) & (GDN_CHUNK - 1)
    s = 1
    while s < GDN_CHUNK:
        x = x + jnp.where(rin >= s, pltpu.roll(x, s, 0), 0.0)
        s *= 2
    return x


def _diag_blocks(m):
    c = GDN_CHUNK
    return jnp.concatenate([m[b * c:(b + 1) * c, b * c:(b + 1) * c] for b in range(m.shape[0] // c)], axis=0)


def _chunk_decay(gc):
    n = gc.shape[0]
    c = GDN_CHUNK
    gt = gc.T
    gj = jnp.concatenate([jnp.broadcast_to(gt[0:1, b * c:(b + 1) * c], (c, c)) for b in range(n // c)], axis=0)
    i_in = lax.broadcasted_iota(I32, (n, c), 0) & (c - 1)
    col = lax.broadcasted_iota(I32, (n, c), 1)
    lower = col <= i_in
    decay = jnp.where(lower, jnp.exp(jnp.where(lower, gc[:, :c] - gj, 0.0)), 0.0)
    return decay, col < i_in


def _gdn_l_kernel(k_ref, bg_ref, l_ref):
    kh = pl.program_id(1)
    sc = GDN_SUPER
    c = GDN_CHUNK

    def body(s, carry):
        r0 = pl.multiple_of(s * sc, sc)
        k = k_ref[pl.ds(r0, sc), :]
        bg = bg_ref[pl.ds(r0, sc), :]
        kk = _diag_blocks(_dot_nt(k, k))
        reps = range(GDN_REP)
        beta = [_head_col(bg, kh * GDN_REP + r) for r in reps]
        g = [_head_col(bg, GDN_V_HEADS + kh * GDN_REP + r) for r in reps]
        gc = [_chunk_cumsum(x) for x in g]
        dec = [_chunk_decay(x) for x in gc]
        for r in reps:
            decay, strict = dec[r]
            l_ref[0, r, pl.ds(r0, sc), :] = jnp.where(strict, kk * beta[r][:, :c] * decay, 0.0)
        return carry
    lax.fori_loop(0, k_ref.shape[0] // sc, body, 0)


def _gdn_l(k, bg, batch, seq):
    return pl.pallas_call(
        _gdn_l_kernel,
        grid=(batch, GDN_K_HEADS),
        in_specs=[pl.BlockSpec((seq, GDN_HEAD_DIM), lambda b, h: (b, h)),
                  pl.BlockSpec((seq, LANES), lambda b, h: (b, 0))],
        out_specs=pl.BlockSpec((1, GDN_REP, seq, GDN_CHUNK), lambda b, h: (b, h, 0, 0)),
        out_shape=jax.ShapeDtypeStruct((batch, GDN_V_HEADS, seq, GDN_CHUNK), F32),
        compiler_params=_cparams("arbitrary", "arbitrary"),
        name="gdn_chunk_l",
    )(k, bg)


def _tri_inv_kernel(a_ref, x_ref):
    c, n = a_ref.shape[1], a_ref.shape[2]
    sub_row = lax.broadcasted_iota(I32, (SUBLANES, n), 0)
    for c0 in range(0, c, SUBLANES):
        cols = slice(c0, c0 + SUBLANES)
        for i in range(c):
            if i < c0:
                x_ref[i, cols, :] = jnp.zeros((SUBLANES, n), F32)
                continue

            def sub(j, acc, i=i, cols=cols):
                return acc - a_ref[i, pl.ds(j, 1), :] * x_ref[j, cols, :]
            row = jnp.where(sub_row == i - c0, 1.0, 0.0)
            if i > c0:
                row = lax.fori_loop(c0, i, sub, row, unroll=min(i - c0, 8))
            x_ref[i, cols, :] = row


def _tri_inv(a):
    c, _, n = a.shape
    lanes = min(INV_LANES, n)
    spec = pl.BlockSpec((c, c, lanes), lambda i: (0, 0, i))
    return pl.pallas_call(
        _tri_inv_kernel,
        grid=(n // lanes,),
        in_specs=[spec],
        out_specs=spec,
        out_shape=jax.ShapeDtypeStruct(a.shape, F32),
        compiler_params=_cparams("arbitrary"),
        name="gdn_tri_inv",
    )(a)


def _gdn_scan_kernel(q_ref, k_ref, v_ref, z_ref, bg_ref, ti_ref, nw_ref, o_ref):
    grp = pl.program_id(1)
    sc = GDN_SUPER
    nc = sc // GDN_CHUNK
    hd = GDN_HEAD_DIM
    n_super = q_ref.shape[0] // sc
    nw = nw_ref[...]
    r_blk = lax.broadcasted_iota(I32, (sc, sc), 0) // GDN_CHUNK
    c_blk = lax.broadcasted_iota(I32, (sc, sc), 1) // GDN_CHUNK
    same = r_blk == c_blk

    heads = range(GDN_SCAN_HEADS)

    def prepare_head(s, j, shared):
        r0 = pl.multiple_of(s * sc, sc)
        kh = j // GDN_REP
        if kh not in shared:
            qb = q_ref[pl.ds(r0, sc), kh * hd:(kh + 1) * hd]
            kb = k_ref[pl.ds(r0, sc), kh * hd:(kh + 1) * hd]
            shared[kh] = (qb.astype(F32), kb.astype(F32), _diag_blocks(_dot_nt(qb, kb)))
        q, k, qk = shared[kh]
        head = grp * GDN_SCAN_HEADS + j
        bg = bg_ref[pl.ds(r0, sc), :]
        v = v_ref[pl.ds(r0, sc), j * hd:(j + 1) * hd].astype(F32)
        beta = _head_col(bg, head)
        gc = _chunk_cumsum(_head_col(bg, GDN_V_HEADS + head))
        egc = jnp.exp(gc)
        decay, _ = _chunk_decay(gc)
        a_intra = (qk * decay).astype(BF16)
        ti = ti_ref[0, j, pl.ds(r0, sc), :]
        t_inv = jnp.where(same, jnp.concatenate([ti] * nc, axis=1), 0.0).astype(BF16)
        uw = _dot(t_inv, jnp.concatenate([v * beta, k * beta * egc], axis=1).astype(BF16))
        g_last = [gc[(c + 1) * GDN_CHUNK - 1:(c + 1) * GDN_CHUNK, :] for c in range(nc)]
        gl_rows = jnp.concatenate([jnp.broadcast_to(g, (GDN_CHUNK, LANES)) for g in g_last], axis=0)
        k_dec_t = (k * jnp.exp(gl_rows - gc)).T.astype(BF16)
        eg_last = jnp.concatenate([jnp.exp(g) for g in g_last], axis=0)
        return uw[:, :hd], uw[:, hd:].astype(BF16), (q * egc).astype(BF16), k_dec_t, a_intra, eg_last

    def prepare(s):
        shared = {}
        return tuple(prepare_head(s, j, shared) for j in heads)

    def body(s, carry):
        states, prepared = carry
        states = list(states)
        r0 = pl.multiple_of(s * sc, sc)
        nxt = jnp.minimum(s + 1, n_super - 1)
        shared = {}
        following = []
        outs = [[] for _ in heads]
        for c in range(nc):
            rs = slice(c * GDN_CHUNK, (c + 1) * GDN_CHUNK)
            sb = [states[j].astype(BF16) for j in heads]
            v_new = [prepared[j][0][rs] - _dot(prepared[j][1][rs], sb[j]) for j in heads]
            vb = [v.astype(BF16) for v in v_new]
            for j in heads:
                outs[j].append(_dot(prepared[j][2][rs], sb[j]) + _dot(prepared[j][4][rs], vb[j]))
            states = [states[j] * prepared[j][5][c:c + 1, :] + _dot(prepared[j][3][:, rs], vb[j]) for j in heads]
            following.append(prepare_head(nxt, c, shared))
        for j in heads:
            o = jnp.concatenate(outs[j], axis=0)
            z = z_ref[pl.ds(r0, sc), j * hd:(j + 1) * hd].astype(F32)
            o_ref[pl.ds(r0, sc), j * hd:(j + 1) * hd] = (_rms(o, nw) * (z * _sigmoid(z))).astype(o_ref.dtype)
        return tuple(states), tuple(following)

    assert GDN_SCAN_HEADS == nc
    init = tuple(jnp.zeros((hd, hd), F32) for _ in heads)
    lax.fori_loop(0, n_super, body, (init, prepare(0)))


def _gdn_scan(q, k, v, z, bg, t_inv, norm_w, batch, seq):
    hd = GDN_HEAD_DIM
    nh = GDN_SCAN_HEADS
    kspec = pl.BlockSpec((seq, nh // GDN_REP * hd), lambda b, g: (b, g))
    vspec = pl.BlockSpec((seq, nh * hd), lambda b, g: (b, g))
    return pl.pallas_call(
        _gdn_scan_kernel,
        grid=(batch, GDN_V_HEADS // nh),
        in_specs=[kspec, kspec, vspec, vspec, pl.BlockSpec((seq, LANES), lambda b, g: (b, 0)),
                  pl.BlockSpec((1, nh, seq, GDN_CHUNK), lambda b, g: (b, g, 0, 0)),
                  pl.BlockSpec((1, hd), lambda b, g: (0, 0))],
        out_specs=vspec,
        out_shape=jax.ShapeDtypeStruct((batch * seq, GDN_V_HEADS * hd), BF16),
        compiler_params=_cparams("arbitrary", "arbitrary"),
        name="gdn_scan",
    )(q, k, v, z, bg, t_inv, norm_w.reshape(1, hd))


def _gdn_mixer(x, mod, norm_w, w_in, conv_w, a_log, dt_bias, gnorm_w, batch, seq):
    q, k, v, z, bg = _gdn_proj(x, mod, norm_w, w_in, conv_w, a_log, dt_bias, seq)
    c = GDN_CHUNK
    l_c = _gdn_l(k, bg, batch, seq)
    n_chunks = batch * GDN_V_HEADS * (seq // c)
    a = l_c.reshape(n_chunks, c, c).transpose(1, 2, 0)
    t_inv = _tri_inv(a).transpose(2, 0, 1).reshape(batch, GDN_V_HEADS, seq, c)
    return _gdn_scan(q, k, v, z, bg, t_inv, gnorm_w, batch, seq)


def kernel(x, c, positions, norm_mix, norm_ffn, ada_w, ada_b, mla_w_in, mla_q_norm, mla_w_uq,
           mla_kv_norm, mla_w_ukv, mla_w_o, gdn_w_in, gdn_conv_w, gdn_a_log, gdn_dt_bias, gdn_norm_w,
           gdn_w_o, router_w, router_b, moe_w_gate_up, moe_b_gate_up, moe_w_down, moe_b_down, final_norm):
    batch, seq, d = x.shape
    depth = ada_w.shape[0]
    xt = x.reshape(batch * seq, d)
    mod = _adaln_mod(c, ada_w, ada_b)
    cos, sin = _rope_tables(positions)
    for layer in range(depth):
        j = layer // 2
        if layer % 2 == 0:
            weights = _mla_weights(mla_w_in[j], mla_w_uq[j], mla_w_ukv[j])
            q, k, v = _mla_proj(xt, mod[layer], norm_mix[layer], weights, mla_q_norm[j],
                                mla_kv_norm[j], cos, sin, seq)
            o = _attention(q, k, v, batch, seq)
            w_o = mla_w_o[j]
        else:
            o = _gdn_mixer(xt, mod[layer], norm_mix[layer], gdn_w_in[j], gdn_conv_w[j], gdn_a_log[j],
                           gdn_dt_bias[j], gdn_norm_w[j], batch, seq)
            w_o = gdn_w_o[j]
        xt = _moe_block(o, xt, mod[layer], w_o, norm_ffn[layer], router_w[layer], router_b[layer],
                        moe_w_gate_up, moe_b_gate_up, moe_w_down, moe_b_down, final_norm, seq, layer,
                        layer == depth - 1)
    return xt.reshape(batch, seq, d)
```

```python
import functools
import math

import jax
import jax.numpy as jnp
from jax import lax
from jax.experimental import pallas as pl
from jax.experimental.pallas import tpu as pltpu

F32, BF16, I32, U32 = jnp.float32, jnp.bfloat16, jnp.int32, jnp.uint32

NORM_EPS = 1e-6
N_MOD = 6
MLA_HEADS = 8
QK_NOPE_DIM = 128
QK_ROPE_DIM = 64
V_HEAD_DIM = 128
Q_LORA_RANK = 384
KV_LORA_RANK = 256
ROPE_THETA = 10000.0
GDN_K_HEADS = 8
GDN_V_HEADS = 16
GDN_HEAD_DIM = 128
GDN_CONV = 4
GDN_CHUNK = 64
N_EXPERTS = 32
TOP_K = 4
SWIGLU_LIMIT = 7.0
SWIGLU_ALPHA = 1.702

LANES = 128
SUBLANES = 8
VMEM_LIMIT = 56 * 1024 * 1024

TOKEN_TILE = 256
ROUTE_TILE = 512
ATTN_TILE = 256
ATTN_HEADS = 8
MOE_TILE = 512
GDN_SUPER = 4 * GDN_CHUNK
GDN_REP = GDN_V_HEADS // GDN_K_HEADS
GDN_SCAN_HEADS = 4
INV_LANES = 512
NEG_BIG = -1e30
SEG_PIECES = tuple(1 << b for b in range(ROUTE_TILE.bit_length() - 1, SUBLANES.bit_length() - 2, -1))
SEG_ROWS = ROUTE_TILE * TOP_K + N_EXPERTS * SUBLANES
GAP_PIECES = tuple(1 << b for b in range(MOE_TILE.bit_length() - 2, SUBLANES.bit_length() - 2, -1))


def _cparams(*sem):
    return pltpu.CompilerParams(dimension_semantics=sem, vmem_limit_bytes=VMEM_LIMIT)


def _sigmoid(x):
    return 1.0 / (1.0 + jnp.exp(-x))


def _rms(x, w):
    return x * lax.rsqrt(jnp.mean(x * x, axis=-1, keepdims=True) + NORM_EPS) * w


def _norm_mod(x, w, shift, scale):
    return _rms(x, w) * (1.0 + scale) + shift


def _dot(a, b):
    return jnp.dot(a, b, preferred_element_type=F32)


def _dot_nt(a, b):
    return lax.dot_general(a, b, (((1,), (1,)), ((), ())), preferred_element_type=F32)


def _dot_f32(a, b):
    return jnp.dot(a, b, preferred_element_type=F32, precision=lax.Precision.HIGHEST)


def _pack_pair(lo, hi):
    ulo = lax.bitcast_convert_type(lo.astype(BF16).astype(F32), U32) >> 16
    uhi = lax.bitcast_convert_type(hi.astype(BF16).astype(F32), U32) & jnp.uint32(0xFFFF0000)
    return ulo | uhi


def _unpack_pair(p):
    lo = lax.bitcast_convert_type(p << 16, F32)
    hi = lax.bitcast_convert_type(p & jnp.uint32(0xFFFF0000), F32)
    return lo, hi


def _mod_kernel(c_ref, w_ref, b_ref, o_ref):
    c = c_ref[...]
    cond = c * _sigmoid(c)
    o_ref[0] = _dot(cond.astype(BF16), w_ref[0].astype(BF16)) + b_ref[0]


def _adaln_mod(c, ada_w, ada_b):
    depth, d, n = ada_w.shape
    b = c.shape[0]
    tn = 1024
    out = pl.pallas_call(
        _mod_kernel,
        grid=(depth, n // tn),
        in_specs=[pl.BlockSpec((b, d), lambda l, j: (0, 0)),
                  pl.BlockSpec((1, d, tn), lambda l, j: (l, 0, j)),
                  pl.BlockSpec((1, 1, tn), lambda l, j: (l, 0, j))],
        out_specs=pl.BlockSpec((1, b, tn), lambda l, j: (l, 0, j)),
        out_shape=jax.ShapeDtypeStruct((depth, b, n), F32),
        compiler_params=_cparams("arbitrary", "arbitrary"),
        name="adaln_mod",
    )(c, ada_w, ada_b.reshape(depth, 1, n))
    return out.reshape(depth, b, N_MOD, d)


def _rope_kernel(pos_ref, cos_ref, sin_ref):
    pos = pos_ref[...].astype(F32)
    lane = lax.broadcasted_iota(I32, (1, LANES), 1)
    j = (lane & (QK_ROPE_DIM // 2 - 1)).astype(F32)
    inv = jnp.exp(j * (-2.0 / QK_ROPE_DIM * math.log(ROPE_THETA)))
    ang = pos * inv
    cos_ref[...] = jnp.cos(ang)
    sin_ref[...] = jnp.sin(ang)


def _rope_tables(positions):
    t = positions.size
    tm = 1024
    spec = pl.BlockSpec((tm, LANES), lambda i: (i, 0))
    return pl.pallas_call(
        _rope_kernel,
        grid=(t // tm,),
        in_specs=[pl.BlockSpec((tm, 1), lambda i: (i, 0))],
        out_specs=[spec, spec],
        out_shape=[jax.ShapeDtypeStruct((t, LANES), F32)] * 2,
        compiler_params=_cparams("arbitrary"),
        name="rope_tables",
    )(positions.reshape(t, 1))


def _mla_proj_kernel(x_ref, mod_ref, nw_ref, win_ref, qn_ref, wqa_ref, wqb_ref, kvn_ref, wkv_ref,
                     cos_ref, sin_ref, q_ref, k_ref, v_ref):
    mod = mod_ref[0]
    h = _norm_mod(x_ref[...], nw_ref[...], mod[0:1], mod[1:2])
    lat = _dot(h.astype(BF16), win_ref[...])
    q_lat = lat[:, :Q_LORA_RANK]
    kv_lat = lat[:, Q_LORA_RANK:Q_LORA_RANK + KV_LORA_RANK]
    kp = lat[:, Q_LORA_RANK + KV_LORA_RANK:]
    cos = cos_ref[...]
    sin = sin_ref[...]
    scale = (QK_NOPE_DIM + QK_ROPE_DIM) ** -0.5

    qn = _rms(q_lat, qn_ref[...]).astype(BF16)
    qa = _dot(qn, wqa_ref[...])
    qb = _dot(qn, wqb_ref[...])
    for hh in range(MLA_HEADS):
        o = hh * 2 * LANES
        q_ref[:, o:o + LANES] = (qa[:, o:o + LANES] * scale).astype(BF16)
        pe = qa[:, o + LANES:o + 2 * LANES] * cos + qb[:, hh * LANES:(hh + 1) * LANES] * sin
        q_ref[:, o + LANES:o + 2 * LANES] = (pe * scale).astype(BF16)

    kvn = _rms(kv_lat, kvn_ref[...]).astype(BF16)
    kv = _dot(kvn, wkv_ref[...])
    lane = lax.broadcasted_iota(I32, (1, LANES), 1)
    first = lane < QK_ROPE_DIM
    u = kp * jnp.where(first, cos, sin)
    kr = jnp.where(first, u + pltpu.roll(u, QK_ROPE_DIM, 1), 0.0).astype(BF16)
    nk = MLA_HEADS * QK_NOPE_DIM
    for hh in range(MLA_HEADS):
        o = hh * 2 * LANES
        k_ref[:, o:o + LANES] = kv[:, hh * LANES:(hh + 1) * LANES].astype(BF16)
        k_ref[:, o + LANES:o + 2 * LANES] = kr
    v_ref[...] = kv[:, nk:].T.astype(BF16)


def _rotate_half_cols(w):
    half = w.shape[-1] // 2
    return jnp.concatenate([-w[..., half:], w[..., :half]], axis=-1)


def _mla_weights(w_in, w_uq, w_ukv):
    kpe = w_in[:, Q_LORA_RANK + KV_LORA_RANK:]
    w_in_ext = jnp.concatenate([w_in, _rotate_half_cols(kpe)], axis=1).astype(BF16)
    wq = w_uq.reshape(Q_LORA_RANK, MLA_HEADS, QK_NOPE_DIM + QK_ROPE_DIM)
    zeros = jnp.zeros((Q_LORA_RANK, MLA_HEADS, LANES - QK_ROPE_DIM), w_uq.dtype)
    wq_pe = wq[:, :, QK_NOPE_DIM:]
    wqa = jnp.concatenate([wq[:, :, :QK_NOPE_DIM], wq_pe, zeros], axis=2)
    wqb = jnp.concatenate([_rotate_half_cols(wq_pe), zeros], axis=2)
    wkv = w_ukv.reshape(KV_LORA_RANK, MLA_HEADS, QK_NOPE_DIM + V_HEAD_DIM)
    wkv = jnp.concatenate([wkv[:, :, :QK_NOPE_DIM].reshape(KV_LORA_RANK, -1),
                           wkv[:, :, QK_NOPE_DIM:].reshape(KV_LORA_RANK, -1)], axis=1)
    return (w_in_ext, wqa.reshape(Q_LORA_RANK, -1).astype(BF16),
            wqb.reshape(Q_LORA_RANK, -1).astype(BF16), wkv.astype(BF16))


def _mla_proj(x, mod, norm_w, weights, q_norm, kv_norm, cos, sin, seq):
    t, d = x.shape
    tm = TOKEN_TILE
    w_in_ext, wqa, wqb, wkv = weights
    per_seq = seq // tm
    full = lambda a: pl.BlockSpec(a.shape, lambda i: (0,) * a.ndim)
    row = lambda n: pl.BlockSpec((tm, n), lambda i: (i, 0))
    nq = MLA_HEADS * 2 * LANES
    nv = MLA_HEADS * V_HEAD_DIM
    args = (x, mod, norm_w.reshape(1, d), w_in_ext, q_norm.reshape(1, -1), wqa, wqb,
            kv_norm.reshape(1, -1), wkv, cos, sin)
    in_specs = [row(d), pl.BlockSpec((1, N_MOD, d), lambda i: (i // per_seq, 0, 0)), full(args[2]),
                full(w_in_ext), full(args[4]), full(wqa), full(wqb), full(args[7]), full(wkv),
                row(LANES), row(LANES)]
    return pl.pallas_call(
        _mla_proj_kernel,
        grid=(t // tm,),
        in_specs=in_specs,
        out_specs=[row(nq), row(nq), pl.BlockSpec((nv, tm), lambda i: (0, i))],
        out_shape=[jax.ShapeDtypeStruct((t, nq), BF16), jax.ShapeDtypeStruct((t, nq), BF16),
                   jax.ShapeDtypeStruct((nv, t), BF16)],
        compiler_params=_cparams("arbitrary"),
        name="mla_proj",
    )(*args)


def _attn_kernel(q_ref, k_ref, vt_ref, o_ref):
    tq = q_ref.shape[0]
    dq = 2 * LANES
    dv = V_HEAD_DIM
    qi = pl.program_id(2)
    heads = range(ATTN_HEADS)
    q = [q_ref[:, h * dq:(h + 1) * dq] for h in heads]

    def step(r0, carry, mask):
        s = [_dot_nt(k_ref[pl.ds(r0, tq), h * dq:(h + 1) * dq], q[h]) for h in heads]
        if mask is not None:
            s = [jnp.where(mask, sh, NEG_BIG) for sh in s]
        m_new = [jnp.maximum(carry[h][0], jnp.max(s[h], axis=0, keepdims=True)) for h in heads]
        p = [jnp.exp(s[h] - m_new[h]) for h in heads]
        alpha = [jnp.exp(carry[h][0] - m_new[h]) for h in heads]
        l = [alpha[h] * carry[h][1] + jnp.sum(p[h], axis=0, keepdims=True) for h in heads]
        acc = [alpha[h] * carry[h][2] + _dot(vt_ref[h * dv:(h + 1) * dv, pl.ds(r0, tq)], p[h].astype(BF16))
               for h in heads]
        return tuple((m_new[h], l[h], acc[h]) for h in heads)

    init = tuple((jnp.full((1, tq), NEG_BIG, F32), jnp.zeros((1, tq), F32), jnp.zeros((dv, tq), F32))
                 for _ in heads)
    carry = lax.fori_loop(0, qi, lambda j, c: step(pl.multiple_of(j * tq, tq), c, None), init)
    causal = (lax.broadcasted_iota(I32, (tq, tq), 0) <= lax.broadcasted_iota(I32, (tq, tq), 1))
    carry = step(pl.multiple_of(qi * tq, tq), carry, causal)
    for h in heads:
        _, l, acc = carry[h]
        o_ref[:, h * dv:(h + 1) * dv] = (acc / l).T.astype(o_ref.dtype)


def _attention(q, k, v, batch, seq):
    t = q.shape[0]
    tq = ATTN_TILE
    nq = seq // tq
    dq = ATTN_HEADS * 2 * LANES
    dv = ATTN_HEADS * V_HEAD_DIM
    return pl.pallas_call(
        _attn_kernel,
        grid=(batch, MLA_HEADS // ATTN_HEADS, nq),
        in_specs=[pl.BlockSpec((tq, dq), lambda b, h, i: (b * nq + i, h)),
                  pl.BlockSpec((seq, dq), lambda b, h, i: (b, h)),
                  pl.BlockSpec((dv, seq), lambda b, h, i: (h, b))],
        out_specs=pl.BlockSpec((tq, dv), lambda b, h, i: (b * nq + i, h)),
        out_shape=jax.ShapeDtypeStruct((t, MLA_HEADS * V_HEAD_DIM), BF16),
        compiler_params=_cparams("arbitrary", "arbitrary", "arbitrary"),
        name="mla_attention",
    )(q, k, v)


def _post_mixer_kernel(o_ref, x_ref, mod_ref, wo_ref, nw_ref, rw_ref, rb_ref,
                       x1_ref, h_ref, route_ref, tcnt_ref, trun_ref, cnt_ref, run_ref):
    tm, d = x_ref.shape
    i = pl.program_id(0)

    @pl.when(i == 0)
    def _():
        run_ref[...] = jnp.zeros_like(run_ref)

    mod = mod_ref[0]
    x1 = x_ref[...] + mod[2:3] * _dot(o_ref[...], wo_ref[...])
    x1_ref[...] = x1
    h = _norm_mod(x1, nw_ref[...], mod[3:4], mod[4:5])
    h_ref[...] = h.astype(BF16)

    h_hi = h.astype(BF16)
    h_lo = (h - h_hi.astype(F32)).astype(BF16)
    parts = _dot(h_hi, rw_ref[...]) + _dot(h_lo, rw_ref[...])
    lane = lax.broadcasted_iota(I32, (tm, LANES), 1)
    logits = jnp.where(lane < N_EXPERTS, parts + pltpu.roll(parts, LANES - N_EXPERTS, 1) + rb_ref[...], NEG_BIG)
    lane_f = lane.astype(F32)
    work = logits
    val, hot = [], []
    for _ in range(TOP_K):
        m = jnp.max(work, axis=-1, keepdims=True)
        a = jnp.min(jnp.where(work == m, lane_f, float(LANES)), axis=-1, keepdims=True)
        hot.append(lane_f == a)
        val.append(m)
        work = jnp.where(hot[-1], -jnp.inf, work)
    ex = [jnp.exp(v - val[0]) for v in val]
    den = ex[0] + ex[1] + ex[2] + ex[3]

    onehot = jnp.where(hot[0] | hot[1] | hot[2] | hot[3], 1.0, 0.0)
    row = lax.broadcasted_iota(I32, (tm, tm), 0)
    col = lax.broadcasted_iota(I32, (tm, tm), 1)
    tri = jnp.where(col < row, 1.0, 0.0).astype(BF16)
    rank = _dot(tri, onehot.astype(BF16))
    cnt = jnp.sum(onehot, axis=0, keepdims=True)
    cnt = jnp.floor((cnt + (SUBLANES - 1)) * (1.0 / SUBLANES)) * SUBLANES
    e_row = lax.broadcasted_iota(I32, (LANES, LANES), 0)
    e_col = lax.broadcasted_iota(I32, (LANES, LANES), 1)
    before = jnp.where(e_row < e_col, 1.0, 0.0)
    off = _dot_f32(jnp.broadcast_to(cnt, (SUBLANES, LANES)), before)[0:1]
    local = off + rank

    route = jnp.zeros((tm, LANES), F32)
    for kk in range(TOP_K):
        pos = jnp.sum(jnp.where(hot[kk], local, 0.0), axis=-1, keepdims=True)
        route = jnp.where(lane == kk, pos, route)
        route = jnp.where(lane == TOP_K + kk, ex[kk] / den, route)
    route_ref[...] = route
    tcnt_ref[0] = cnt.astype(I32)
    trun_ref[0] = run_ref[...].astype(I32)
    run = run_ref[...] + cnt
    run_ref[...] = run
    cnt_ref[...] = run.astype(I32)


def _post_mixer(o, x, mod, w_o, norm_w, router_w, router_b, seq):
    t, d = x.shape
    ko = o.shape[1]
    tm = ROUTE_TILE
    per_seq = seq // tm
    n_tiles = t // tm
    rw_hi = router_w.astype(BF16)
    rw_lo = (router_w - rw_hi.astype(F32)).astype(BF16)
    rw = jnp.zeros((d, LANES), BF16).at[:, :N_EXPERTS].set(rw_hi).at[:, N_EXPERTS:2 * N_EXPERTS].set(rw_lo)
    rb = jnp.zeros((1, LANES), F32).at[0, :N_EXPERTS].set(router_b)
    row = lambda n: pl.BlockSpec((tm, n), lambda i: (i, 0))
    const = lambda r, c: pl.BlockSpec((r, c), lambda i: (0, 0))
    per_tile = pl.BlockSpec((1, 1, LANES), lambda i: (i, 0, 0))
    return pl.pallas_call(
        _post_mixer_kernel,
        grid=(n_tiles,),
        in_specs=[row(ko), row(d), pl.BlockSpec((1, N_MOD, d), lambda i: (i // per_seq, 0, 0)),
                  const(ko, d), const(1, d), const(d, LANES), const(1, LANES)],
        out_specs=[row(d), row(d), row(LANES), per_tile, per_tile, const(1, LANES)],
        out_shape=[jax.ShapeDtypeStruct((t, d), F32), jax.ShapeDtypeStruct((t, d), BF16),
                   jax.ShapeDtypeStruct((t, LANES), F32),
                   jax.ShapeDtypeStruct((n_tiles, 1, LANES), I32),
                   jax.ShapeDtypeStruct((n_tiles, 1, LANES), I32),
                   jax.ShapeDtypeStruct((1, LANES), I32)],
        scratch_shapes=[pltpu.VMEM((1, LANES), F32)],
        compiler_params=_cparams("arbitrary"),
        name="post_mixer_router",
    )(o, x, mod, w_o.astype(BF16), norm_w.reshape(1, d), rw, rb)


def _tile_ceil(n):
    return (n + (MOE_TILE - 1)) // MOE_TILE * MOE_TILE


def _expert_starts(counts_ref, starts_ref):
    def body(e, acc):
        starts_ref[e] = acc
        return acc + _tile_ceil(counts_ref[e])
    return lax.fori_loop(0, N_EXPERTS, body, jnp.int32(0))


def _segment_copies(tile, tcnt_ref, trun_ref, starts_ref, visit):
    def per_expert(e, off):
        n = tcnt_ref[tile, e]
        base = starts_ref[e] + trun_ref[tile, e]
        for piece in SEG_PIECES:
            done = n & (-2 * piece)

            @pl.when((n & piece) != 0)
            def _():
                visit(pl.multiple_of(off + done, SUBLANES), pl.multiple_of(base + done, SUBLANES), piece)
        return off + n
    lax.fori_loop(0, N_EXPERTS, per_expert, jnp.int32(0))


def _dispatch_kernel(tcnt_ref, trun_ref, counts_ref, route_ref, h_ref, xs_ref, g_ref, zero_ref, starts_ref, sem):
    i = pl.program_id(0)
    n = pl.num_programs(0)
    tm, d = h_ref.shape
    slot = lax.rem(i, 2)

    def copies(tile, s, act):
        def visit(src, dst, nrows):
            act(pltpu.make_async_copy(g_ref.at[s, pl.ds(src, nrows)], xs_ref.at[pl.ds(dst, nrows)], sem.at[s]))
        _segment_copies(tile, tcnt_ref, trun_ref, starts_ref, visit)

    start = lambda c: c.start()
    wait = lambda c: c.wait()

    @pl.when(i == 0)
    def _():
        _expert_starts(counts_ref, starts_ref)

    @pl.when(i >= 2)
    def _():
        copies(i - 2, slot, wait)

    rt = route_ref[...].T
    r = lax.broadcasted_iota(I32, (SEG_ROWS, tm), 0).astype(F32)
    hit = (r == rt[0:1]) | (r == rt[1:2]) | (r == rt[2:3]) | (r == rt[3:4])
    g = _dot(jnp.where(hit, 1.0, 0.0).astype(BF16), h_ref[...])
    g_ref[slot] = _pack_pair(g[:, :d // 2], g[:, d // 2:])
    copies(i, slot, start)

    @pl.when(i == n - 1)
    def _():
        copies(i, slot, wait)

        @pl.when(i >= 1)
        def _():
            copies(i - 1, 1 - slot, wait)
        zero_ref[...] = jnp.zeros_like(zero_ref)

        def gap_fill(act):
            def per_expert(e, c):
                cnt = counts_ref[e]
                gap = _tile_ceil(cnt) - cnt
                base = starts_ref[e] + cnt
                for piece in GAP_PIECES:
                    done = gap & (-2 * piece)

                    @pl.when((gap & piece) != 0)
                    def _():
                        act(pltpu.make_async_copy(
                            zero_ref.at[pl.ds(0, piece)],
                            xs_ref.at[pl.ds(pl.multiple_of(base + done, SUBLANES), piece)], sem.at[slot]))
                return c
            lax.fori_loop(0, N_EXPERTS, per_expert, 0)

        used = starts_ref[N_EXPERTS - 1] + _tile_ceil(counts_ref[N_EXPERTS - 1])
        n_unused = (xs_ref.shape[0] - used) // MOE_TILE

        def tile_fill(act):
            def per_tile(kk, c):
                row = pl.multiple_of(used + kk * MOE_TILE, MOE_TILE)
                act(pltpu.make_async_copy(zero_ref, xs_ref.at[pl.ds(row, MOE_TILE)], sem.at[slot]))
                return c
            lax.fori_loop(0, n_unused, per_tile, 0)

        gap_fill(start)
        tile_fill(start)
        gap_fill(wait)
        tile_fill(wait)


def _sorted_rows(t):
    bound = t * TOP_K + (t // ROUTE_TILE) * N_EXPERTS * (SUBLANES - 1) + N_EXPERTS * (MOE_TILE - SUBLANES)
    return (bound + MOE_TILE - 1) // MOE_TILE * MOE_TILE


def _dispatch(h, route, tcnt, trun, counts):
    t, d = h.shape
    tm = ROUTE_TILE
    smem = pl.BlockSpec(memory_space=pltpu.SMEM)
    return pl.pallas_call(
        _dispatch_kernel,
        grid=(t // tm,),
        in_specs=[smem, smem, smem, pl.BlockSpec((tm, LANES), lambda i: (i, 0)),
                  pl.BlockSpec((tm, d), lambda i: (i, 0))],
        out_specs=pl.BlockSpec(memory_space=pl.ANY),
        out_shape=jax.ShapeDtypeStruct((_sorted_rows(t), d // 2), U32),
        scratch_shapes=[pltpu.VMEM((2, SEG_ROWS, d // 2), U32), pltpu.VMEM((MOE_TILE, d // 2), U32),
                        pltpu.SMEM((N_EXPERTS,), I32), pltpu.SemaphoreType.DMA((2,))],
        compiler_params=_cparams("arbitrary"),
        name="moe_dispatch",
    )(tcnt, trun, counts, route, h)


def _moe_schedule(counts, n_tiles, tile):
    ids = jnp.arange(N_EXPERTS, dtype=I32)
    n_own = (counts + tile - 1) // tile
    t_end = jnp.cumsum(n_own)
    t_start = t_end - n_own
    total = t_end[-1]
    later = jnp.where((ids[None, :] > ids[:, None]) & (counts[None, :] > 0), ids[None, :], N_EXPERTS)
    next_e = jnp.min(later, axis=1)
    next_e = jnp.where(next_e == N_EXPERTS, -1, next_e)
    step = jnp.arange(n_tiles, dtype=I32)
    live = step < total
    v = jnp.minimum(step, total - 1)
    e = jnp.minimum(jnp.sum(t_end[None, :] <= v[:, None], axis=1), N_EXPERTS - 1).astype(I32)
    pick = e[:, None] == ids[None, :]
    of_e = lambda a: jnp.sum(jnp.where(pick, a[None, :], 0), axis=1)
    new_expert = (live & (step == of_e(t_start))).astype(I32)
    return e, new_expert, live.astype(I32), of_e(next_e).astype(I32)


def _moe_kernel(exp_ref, newe_ref, live_ref, next_ref,
                x_ref, wgu_hbm, bgu_ref, wd_hbm, bd_ref, y_ref,
                wgu_f32, wd_f32, wgu_bf, wd_bf, sem, *, layer):
    v = pl.program_id(0)
    f = wd_bf.shape[0]
    half = x_ref.shape[1]

    def weight_copies(e):
        return (pltpu.make_async_copy(wgu_hbm.at[layer, e], wgu_f32, sem.at[0]),
                pltpu.make_async_copy(wd_hbm.at[layer, e], wd_f32, sem.at[1]))

    @pl.when(newe_ref[v] == 1)
    def _():
        e = exp_ref[v]

        @pl.when(v == 0)
        def _():
            for c in weight_copies(e):
                c.start()
        for c in weight_copies(e):
            c.wait()
        wgu_bf[...] = wgu_f32[...].astype(BF16)
        wd_bf[...] = wd_f32[...].astype(BF16)
        nxt = next_ref[v]

        @pl.when(nxt >= 0)
        def _():
            for c in weight_copies(nxt):
                c.start()

    @pl.when(live_ref[v] == 1)
    def _():
        x_lo, x_hi = _unpack_pair(x_ref[...])
        gu = (_dot(x_lo.astype(BF16), wgu_bf[:half, :]) + _dot(x_hi.astype(BF16), wgu_bf[half:, :])
              + bgu_ref[0])
        gate = jnp.minimum(gu[:, :f], SWIGLU_LIMIT)
        lin = jnp.clip(gu[:, f:], -SWIGLU_LIMIT, SWIGLU_LIMIT)
        act = gate * _sigmoid(SWIGLU_ALPHA * gate) * (lin + 1.0)
        y = _dot(act.astype(BF16), wd_bf[...]) + bd_ref[0]
        y_ref[...] = _pack_pair(y[:, :half], y[:, half:])

    @pl.when(live_ref[v] == 0)
    def _():
        y_ref[...] = jnp.zeros_like(y_ref)


def _moe_experts(xs, sched, w_gate_up, b_gate_up, w_down, b_down, layer):
    a, half = xs.shape
    depth, e, d, f2 = w_gate_up.shape
    f = f2 // 2
    tile = MOE_TILE
    grid_spec = pltpu.PrefetchScalarGridSpec(
        num_scalar_prefetch=4,
        grid=(a // tile,),
        in_specs=[pl.BlockSpec((tile, half), lambda v, ex, *_: (v, 0)),
                  pl.BlockSpec(memory_space=pl.ANY),
                  pl.BlockSpec((None, 1, 1, f2), lambda v, ex, *_: (layer, ex[v], 0, 0)),
                  pl.BlockSpec(memory_space=pl.ANY),
                  pl.BlockSpec((None, 1, 1, d), lambda v, ex, *_: (layer, ex[v], 0, 0))],
        out_specs=pl.BlockSpec((tile, half), lambda v, ex, *_: (v, 0)),
        scratch_shapes=[pltpu.VMEM((d, f2), F32), pltpu.VMEM((f, d), F32),
                        pltpu.VMEM((d, f2), BF16), pltpu.VMEM((f, d), BF16),
                        pltpu.SemaphoreType.DMA((2,))],
    )
    return pl.pallas_call(
        functools.partial(_moe_kernel, layer=layer),
        grid_spec=grid_spec,
        out_shape=jax.ShapeDtypeStruct((a, half), U32),
        compiler_params=_cparams("arbitrary"),
        name="moe_experts",
    )(*sched, xs, w_gate_up, b_gate_up.reshape(depth, e, 1, f2), w_down, b_down.reshape(depth, e, 1, d))


def _combine_kernel(tcnt_ref, trun_ref, counts_ref, ys_ref, route_ref, x_ref, mod_ref, fw_ref,
                    o_ref, y_buf, starts_ref, sem, *, final_norm):
    i = pl.program_id(0)
    n = pl.num_programs(0)
    tm, d = x_ref.shape
    slot = lax.rem(i, 2)

    def copies(tile, s, act):
        def visit(dst, src, nrows):
            act(pltpu.make_async_copy(ys_ref.at[pl.ds(src, nrows)], y_buf.at[s, pl.ds(dst, nrows)], sem.at[s]))
        _segment_copies(tile, tcnt_ref, trun_ref, starts_ref, visit)

    @pl.when(i == 0)
    def _():
        _expert_starts(counts_ref, starts_ref)
        y_buf[...] = jnp.zeros_like(y_buf)
        copies(0, 0, lambda c: c.start())

    @pl.when(i + 1 < n)
    def _():
        copies(i + 1, 1 - slot, lambda c: c.start())

    copies(i, slot, lambda c: c.wait())
    y_lo, y_hi = _unpack_pair(y_buf[slot])
    route = route_ref[...]
    r = lax.broadcasted_iota(I32, (tm, SEG_ROWS), 1).astype(F32)
    w = jnp.zeros((tm, SEG_ROWS), F32)
    for kk in range(TOP_K):
        w = jnp.where(r == route[:, kk:kk + 1], route[:, TOP_K + kk:TOP_K + kk + 1], w)
    w = w.astype(BF16)
    y = jnp.concatenate([_dot(w, y_lo.astype(BF16)), _dot(w, y_hi.astype(BF16))], axis=1)
    out = x_ref[...] + mod_ref[0][5:6] * y
    if final_norm:
        out = _rms(out, fw_ref[...])
    o_ref[...] = out


def _combine(ys, route, tcnt, trun, counts, x1, mod, final_w, seq, final_norm):
    t, d = x1.shape
    tm = ROUTE_TILE
    per_seq = seq // tm
    smem = pl.BlockSpec(memory_space=pltpu.SMEM)
    row = lambda n: pl.BlockSpec((tm, n), lambda i: (i, 0))
    return pl.pallas_call(
        functools.partial(_combine_kernel, final_norm=final_norm),
        grid=(t // tm,),
        in_specs=[smem, smem, smem, pl.BlockSpec(memory_space=pl.ANY), row(LANES), row(d),
                  pl.BlockSpec((1, N_MOD, d), lambda i: (i // per_seq, 0, 0)),
                  pl.BlockSpec((1, d), lambda i: (0, 0))],
        out_specs=row(d),
        out_shape=jax.ShapeDtypeStruct((t, d), F32),
        scratch_shapes=[pltpu.VMEM((2, SEG_ROWS, d // 2), U32), pltpu.SMEM((N_EXPERTS,), I32),
                        pltpu.SemaphoreType.DMA((2,))],
        compiler_params=_cparams("arbitrary"),
        name="moe_combine",
    )(tcnt, trun, counts, ys, route, x1, mod, final_w.reshape(1, d))


def _moe_block(o, x, mod, w_o, norm_w, router_w, router_b, w_gate_up, b_gate_up, w_down, b_down,
               final_w, seq, layer, final_norm):
    x1, h, route, tcnt, trun, cnt = _post_mixer(o, x, mod, w_o, norm_w, router_w, router_b, seq)
    n_tiles = tcnt.shape[0]
    tcnt = tcnt.reshape(n_tiles, LANES)
    trun = trun.reshape(n_tiles, LANES)
    counts = cnt[0, :N_EXPERTS]
    xs = _dispatch(h, route, tcnt, trun, counts)
    sched = _moe_schedule(counts, xs.shape[0] // MOE_TILE, MOE_TILE)
    ys = _moe_experts(xs, sched, w_gate_up, b_gate_up, w_down, b_down, layer)
    return _combine(ys, route, tcnt, trun, counts, x1, mod, final_w, seq, final_norm)


def _gdn_conv_kernel(x_ref, mod_ref, nw_ref, win_ref, cw_ref, q_ref, k_ref, v_ref, buf_ref, *, per_seq):
    tm = x_ref.shape[0]
    nqk = q_ref.shape[1]
    nconv = win_ref.shape[1]
    i = pl.program_id(0)
    mod = mod_ref[0]
    h = _norm_mod(x_ref[...], nw_ref[...], mod[0:1], mod[1:2])

    @pl.when(i % per_seq == 0)
    def _():
        buf_ref[0:SUBLANES, :] = jnp.zeros((SUBLANES, nconv), F32)

    buf_ref[SUBLANES:SUBLANES + tm, :] = _dot(h.astype(BF16), win_ref[...])
    cw = cw_ref[...]
    window = buf_ref[0:SUBLANES + tm, :]
    acc = cw[GDN_CONV - 1:GDN_CONV, :] * window[SUBLANES:, :]
    for back in range(1, GDN_CONV):
        acc = acc + cw[GDN_CONV - 1 - back:GDN_CONV - back, :] * pltpu.roll(window, back, 0)[SUBLANES:, :]
    buf_ref[0:SUBLANES, :] = buf_ref[tm:tm + SUBLANES, :]
    y = acc * _sigmoid(acc)

    def l2(a):
        return a * lax.rsqrt(jnp.sum(a * a, axis=-1, keepdims=True) + 1e-6)

    for hh in range(nqk // GDN_HEAD_DIM):
        s = slice(hh * GDN_HEAD_DIM, (hh + 1) * GDN_HEAD_DIM)
        q_ref[:, s] = (l2(y[:, s]) * GDN_HEAD_DIM ** -0.5).astype(BF16)
        k_ref[:, s] = l2(y[:, nqk + hh * GDN_HEAD_DIM:nqk + (hh + 1) * GDN_HEAD_DIM]).astype(BF16)
    v_ref[...] = y[:, 2 * nqk:].astype(BF16)


def _gdn_gate_kernel(x_ref, mod_ref, nw_ref, win_ref, alog_ref, dtb_ref, z_ref, bg_ref):
    nv = z_ref.shape[1]
    mod = mod_ref[0]
    h = _norm_mod(x_ref[...], nw_ref[...], mod[0:1], mod[1:2])
    pr = _dot(h.astype(BF16), win_ref[...])
    z_ref[...] = pr[:, :nv].astype(BF16)
    ba = pr[:, nv:]
    sp = ba + dtb_ref[...]
    softplus = jnp.maximum(sp, 0.0) + jnp.log(1.0 + jnp.exp(-jnp.abs(sp)))
    lane = lax.broadcasted_iota(I32, (1, LANES), 1)
    log_decay = -jnp.exp(alog_ref[...]) * softplus
    bg_ref[...] = jnp.where(lane < GDN_V_HEADS, _sigmoid(ba), _chunk_cumsum(log_decay))


def _gdn_proj(x, mod, norm_w, w_in, conv_w, a_log, dt_bias, seq):
    t, d = x.shape
    tm = TOKEN_TILE
    per_seq = seq // tm
    nqk = GDN_K_HEADS * GDN_HEAD_DIM
    nv = GDN_V_HEADS * GDN_HEAD_DIM
    nconv = 2 * nqk + nv
    w_conv = w_in[:, :nconv].astype(BF16)
    w_gate = jnp.zeros((d, nv + LANES), BF16).at[:, :w_in.shape[1] - nconv].set(w_in[:, nconv:].astype(BF16))
    pad = jnp.zeros((1, LANES), F32)
    alog = pad.at[0, GDN_V_HEADS:2 * GDN_V_HEADS].set(a_log)
    dtb = pad.at[0, GDN_V_HEADS:2 * GDN_V_HEADS].set(dt_bias)
    row = lambda n: pl.BlockSpec((tm, n), lambda i: (i, 0))
    const = lambda a: pl.BlockSpec(a.shape, lambda i: (0,) * a.ndim)
    modspec = pl.BlockSpec((1, N_MOD, d), lambda i: (i // per_seq, 0, 0))
    nw = norm_w.reshape(1, d)
    q, k, v = pl.pallas_call(
        functools.partial(_gdn_conv_kernel, per_seq=per_seq),
        grid=(t // tm,),
        in_specs=[row(d), modspec, const(nw), const(w_conv), const(conv_w)],
        out_specs=[row(nqk), row(nqk), row(nv)],
        out_shape=[jax.ShapeDtypeStruct((t, nqk), BF16), jax.ShapeDtypeStruct((t, nqk), BF16),
                   jax.ShapeDtypeStruct((t, nv), BF16)],
        scratch_shapes=[pltpu.VMEM((tm + 2 * SUBLANES, nconv), F32)],
        compiler_params=_cparams("arbitrary"),
        name="gdn_conv_proj",
    )(x, mod, nw, w_conv, conv_w)
    z, bg = pl.pallas_call(
        _gdn_gate_kernel,
        grid=(t // tm,),
        in_specs=[row(d), modspec, const(nw), const(w_gate), const(alog), const(dtb)],
        out_specs=[row(nv), row(LANES)],
        out_shape=[jax.ShapeDtypeStruct((t, nv), BF16), jax.ShapeDtypeStruct((t, LANES), F32)],
        compiler_params=_cparams("arbitrary"),
        name="gdn_gate_proj",
    )(x, mod, nw, w_gate, alog, dtb)
    return q, k, v, z, bg


def _head_col(block, lane_idx):
    lane = lax.broadcasted_iota(I32, (1, LANES), 1)
    col = jnp.sum(jnp.where(lane == lane_idx, block, 0.0), axis=-1, keepdims=True)
    return jnp.broadcast_to(col, block.shape)


def _chunk_cumsum(x):
    rin = lax.broadcasted_iota(I32, x.shape, 0) & (GDN_CHUNK - 1)
    s = 1
    while s < GDN_CHUNK:
        x = x + jnp.where(rin >= s, pltpu.roll(x, s, 0), 0.0)
        s *= 2
    return x


def _diag_blocks(m):
    c = GDN_CHUNK
    return jnp.concatenate([m[b * c:(b + 1) * c, b * c:(b + 1) * c] for b in range(m.shape[0] // c)], axis=0)


def _chunk_decay(gc):
    n = gc.shape[0]
    c = GDN_CHUNK
    gt = gc.T
    gj = jnp.concatenate([jnp.broadcast_to(gt[0:1, b * c:(b + 1) * c], (c, c)) for b in range(n // c)], axis=0)
    i_in = lax.broadcasted_iota(I32, (n, c), 0) & (c - 1)
    col = lax.broadcasted_iota(I32, (n, c), 1)
    lower = col <= i_in
    decay = jnp.where(lower, jnp.exp(jnp.where(lower, gc[:, :c] - gj, 0.0)), 0.0)
    return decay, col < i_in


def _gdn_l_kernel(k_ref, bg_ref, l_ref):
    kh = pl.program_id(1)
    sc = GDN_SUPER
    c = GDN_CHUNK

    def body(s, carry):
        r0 = pl.multiple_of(s * sc, sc)
        k = k_ref[pl.ds(r0, sc), :]
        bg = bg_ref[pl.ds(r0, sc), :]
        kk = _diag_blocks(_dot_nt(k, k))
        reps = range(GDN_REP)
        beta = [_head_col(bg, kh * GDN_REP + r) for r in reps]
        gc = [_head_col(bg, GDN_V_HEADS + kh * GDN_REP + r) for r in reps]
        dec = [_chunk_decay(x) for x in gc]
        for r in reps:
            decay, strict = dec[r]
            l_ref[0, r, pl.ds(r0, sc), :] = jnp.where(strict, kk * beta[r][:, :c] * decay, 0.0)
        return carry
    lax.fori_loop(0, k_ref.shape[0] // sc, body, 0)


def _gdn_l(k, bg, batch, seq):
    return pl.pallas_call(
        _gdn_l_kernel,
        grid=(batch, GDN_K_HEADS),
        in_specs=[pl.BlockSpec((seq, GDN_HEAD_DIM), lambda b, h: (b, h)),
                  pl.BlockSpec((seq, LANES), lambda b, h: (b, 0))],
        out_specs=pl.BlockSpec((1, GDN_REP, seq, GDN_CHUNK), lambda b, h: (b, h, 0, 0)),
        out_shape=jax.ShapeDtypeStruct((batch, GDN_V_HEADS, seq, GDN_CHUNK), F32),
        compiler_params=_cparams("arbitrary", "arbitrary"),
        name="gdn_chunk_l",
    )(k, bg)


def _tri_inv_kernel(a_ref, x_ref):
    c, n = a_ref.shape[1], a_ref.shape[2]
    sub_row = lax.broadcasted_iota(I32, (SUBLANES, n), 0)
    for c0 in range(0, c, SUBLANES):
        cols = slice(c0, c0 + SUBLANES)
        for i in range(c):
            if i < c0:
                x_ref[i, cols, :] = jnp.zeros((SUBLANES, n), F32)
                continue

            def sub(j, acc, i=i, cols=cols):
                return acc - a_ref[i, pl.ds(j, 1), :] * x_ref[j, cols, :]
            row = jnp.where(sub_row == i - c0, 1.0, 0.0)
            if i > c0:
                row = lax.fori_loop(c0, i, sub, row, unroll=min(i - c0, 8))
            x_ref[i, cols, :] = row


def _tri_inv(a):
    c, _, n = a.shape
    lanes = min(INV_LANES, n)
    spec = pl.BlockSpec((c, c, lanes), lambda i: (0, 0, i))
    return pl.pallas_call(
        _tri_inv_kernel,
        grid=(n // lanes,),
        in_specs=[spec],
        out_specs=spec,
        out_shape=jax.ShapeDtypeStruct(a.shape, F32),
        compiler_params=_cparams("arbitrary"),
        name="gdn_tri_inv",
    )(a)


def _gdn_scan_kernel(q_ref, k_ref, v_ref, z_ref, bg_ref, ti_ref, nw_ref, o_ref):
    grp = pl.program_id(1)
    sc = GDN_SUPER
    nc = sc // GDN_CHUNK
    hd = GDN_HEAD_DIM
    n_super = q_ref.shape[0] // sc
    nw = nw_ref[...]
    r_blk = lax.broadcasted_iota(I32, (sc, sc), 0) // GDN_CHUNK
    c_blk = lax.broadcasted_iota(I32, (sc, sc), 1) // GDN_CHUNK
    same = r_blk == c_blk

    heads = range(GDN_SCAN_HEADS)

    def prepare_head(s, j, shared):
        r0 = pl.multiple_of(s * sc, sc)
        kh = j // GDN_REP
        if kh not in shared:
            qb = q_ref[pl.ds(r0, sc), kh * hd:(kh + 1) * hd]
            kb = k_ref[pl.ds(r0, sc), kh * hd:(kh + 1) * hd]
            shared[kh] = (qb.astype(F32), kb.astype(F32), _diag_blocks(_dot_nt(qb, kb)))
        q, k, qk = shared[kh]
        head = grp * GDN_SCAN_HEADS + j
        bg = bg_ref[pl.ds(r0, sc), :]
        v = v_ref[pl.ds(r0, sc), j * hd:(j + 1) * hd].astype(F32)
        beta = _head_col(bg, head)
        gc = _head_col(bg, GDN_V_HEADS + head)
        egc = jnp.exp(gc)
        decay, _ = _chunk_decay(gc)
        a_intra = (qk * decay).astype(BF16)
        ti = ti_ref[0, j, pl.ds(r0, sc), :]
        t_inv = jnp.where(same, jnp.concatenate([ti] * nc, axis=1), 0.0).astype(BF16)
        uw = _dot(t_inv, jnp.concatenate([v * beta, k * beta * egc], axis=1).astype(BF16))
        g_last = [gc[(c + 1) * GDN_CHUNK - 1:(c + 1) * GDN_CHUNK, :] for c in range(nc)]
        gl_rows = jnp.concatenate([jnp.broadcast_to(g, (GDN_CHUNK, LANES)) for g in g_last], axis=0)
        k_dec_t = (k * jnp.exp(gl_rows - gc)).T.astype(BF16)
        eg_last = jnp.concatenate([jnp.exp(g) for g in g_last], axis=0)
        return uw[:, :hd], uw[:, hd:].astype(BF16), (q * egc).astype(BF16), k_dec_t, a_intra, eg_last

    def prepare(s):
        shared = {}
        return tuple(prepare_head(s, j, shared) for j in heads)

    def body(s, carry):
        states, prepared = carry
        states = list(states)
        r0 = pl.multiple_of(s * sc, sc)
        nxt = jnp.minimum(s + 1, n_super - 1)
        shared = {}
        following = []
        outs = [[] for _ in heads]
        for c in range(nc):
            rs = slice(c * GDN_CHUNK, (c + 1) * GDN_CHUNK)
            sb = [states[j].astype(BF16) for j in heads]
            v_new = [prepared[j][0][rs] - _dot(prepared[j][1][rs], sb[j]) for j in heads]
            vb = [v.astype(BF16) for v in v_new]
            for j in heads:
                outs[j].append(_dot(prepared[j][2][rs], sb[j]) + _dot(prepared[j][4][rs], vb[j]))
            states = [states[j] * prepared[j][5][c:c + 1, :] + _dot(prepared[j][3][:, rs], vb[j]) for j in heads]
            following.append(prepare_head(nxt, c, shared))
        for j in heads:
            o = jnp.concatenate(outs[j], axis=0)
            z = z_ref[pl.ds(r0, sc), j * hd:(j + 1) * hd].astype(F32)
            o_ref[pl.ds(r0, sc), j * hd:(j + 1) * hd] = (_rms(o, nw) * (z * _sigmoid(z))).astype(o_ref.dtype)
        return tuple(states), tuple(following)

    assert GDN_SCAN_HEADS == nc
    init = tuple(jnp.zeros((hd, hd), F32) for _ in heads)
    lax.fori_loop(0, n_super, body, (init, prepare(0)))


def _gdn_scan(q, k, v, z, bg, t_inv, norm_w, batch, seq):
    hd = GDN_HEAD_DIM
    nh = GDN_SCAN_HEADS
    kspec = pl.BlockSpec((seq, nh // GDN_REP * hd), lambda b, g: (b, g))
    vspec = pl.BlockSpec((seq, nh * hd), lambda b, g: (b, g))
    return pl.pallas_call(
        _gdn_scan_kernel,
        grid=(batch, GDN_V_HEADS // nh),
        in_specs=[kspec, kspec, vspec, vspec, pl.BlockSpec((seq, LANES), lambda b, g: (b, 0)),
                  pl.BlockSpec((1, nh, seq, GDN_CHUNK), lambda b, g: (b, g, 0, 0)),
                  pl.BlockSpec((1, hd), lambda b, g: (0, 0))],
        out_specs=vspec,
        out_shape=jax.ShapeDtypeStruct((batch * seq, GDN_V_HEADS * hd), BF16),
        compiler_params=_cparams("arbitrary", "arbitrary"),
        name="gdn_scan",
    )(q, k, v, z, bg, t_inv, norm_w.reshape(1, hd))


def _gdn_mixer(x, mod, norm_w, w_in, conv_w, a_log, dt_bias, gnorm_w, batch, seq):
    q, k, v, z, bg = _gdn_proj(x, mod, norm_w, w_in, conv_w, a_log, dt_bias, seq)
    c = GDN_CHUNK
    l_c = _gdn_l(k, bg, batch, seq)
    n_chunks = batch * GDN_V_HEADS * (seq // c)
    a = l_c.reshape(n_chunks, c, c).transpose(1, 2, 0)
    t_inv = _tri_inv(a).transpose(2, 0, 1).reshape(batch, GDN_V_HEADS, seq, c)
    return _gdn_scan(q, k, v, z, bg, t_inv, gnorm_w, batch, seq)


def kernel(x, c, positions, norm_mix, norm_ffn, ada_w, ada_b, mla_w_in, mla_q_norm, mla_w_uq,
           mla_kv_norm, mla_w_ukv, mla_w_o, gdn_w_in, gdn_conv_w, gdn_a_log, gdn_dt_bias, gdn_norm_w,
           gdn_w_o, router_w, router_b, moe_w_gate_up, moe_b_gate_up, moe_w_down, moe_b_down, final_norm):
    batch, seq, d = x.shape
    depth = ada_w.shape[0]
    xt = x.reshape(batch * seq, d)
    mod = _adaln_mod(c, ada_w, ada_b)
    cos, sin = _rope_tables(positions)
    for layer in range(depth):
        j = layer // 2
        if layer % 2 == 0:
            weights = _mla_weights(mla_w_in[j], mla_w_uq[j], mla_w_ukv[j])
            q, k, v = _mla_proj(xt, mod[layer], norm_mix[layer], weights, mla_q_norm[j],
                                mla_kv_norm[j], cos, sin, seq)
            o = _attention(q, k, v, batch, seq)
            w_o = mla_w_o[j]
        else:
            o = _gdn_mixer(xt, mod[layer], norm_mix[layer], gdn_w_in[j], gdn_conv_w[j], gdn_a_log[j],
                           gdn_dt_bias[j], gdn_norm_w[j], batch, seq)
            w_o = gdn_w_o[j]
        xt = _moe_block(o, xt, mod[layer], w_o, norm_ffn[layer], router_w[layer], router_b[layer],
                        moe_w_gate_up, moe_b_gate_up, moe_w_down, moe_b_down, final_norm, seq, layer,
                        layer == depth - 1)
    return xt.reshape(batch, seq, d)
```

```python
import functools
import math

import jax
import jax.numpy as jnp
from jax import lax
from jax.experimental import pallas as pl
from jax.experimental.pallas import tpu as pltpu

F32, BF16, I32, U32 = jnp.float32, jnp.bfloat16, jnp.int32, jnp.uint32

NORM_EPS = 1e-6
N_MOD = 6
MLA_HEADS = 8
QK_NOPE_DIM = 128
QK_ROPE_DIM = 64
V_HEAD_DIM = 128
Q_LORA_RANK = 384
KV_LORA_RANK = 256
ROPE_THETA = 10000.0
GDN_K_HEADS = 8
GDN_V_HEADS = 16
GDN_HEAD_DIM = 128
GDN_CONV = 4
GDN_CHUNK = 64
N_EXPERTS = 32
TOP_K = 4
SWIGLU_LIMIT = 7.0
SWIGLU_ALPHA = 1.702

LANES = 128
SUBLANES = 8
VMEM_LIMIT = 56 * 1024 * 1024

TOKEN_TILE = 256
ROUTE_TILE = 512
ATTN_TILE = 256
ATTN_HEADS = 8
MOE_TILE = 512
GDN_SUPER = 4 * GDN_CHUNK
GDN_REP = GDN_V_HEADS // GDN_K_HEADS
GDN_SCAN_HEADS = 4
INV_LANES = 512
NEG_BIG = -1e30
SEG_PIECES = tuple(1 << b for b in range(ROUTE_TILE.bit_length() - 1, SUBLANES.bit_length() - 2, -1))
SEG_ROWS = ROUTE_TILE * TOP_K + N_EXPERTS * SUBLANES
GAP_PIECES = tuple(1 << b for b in range(MOE_TILE.bit_length() - 2, SUBLANES.bit_length() - 2, -1))


def _cparams(*sem):
    return pltpu.CompilerParams(dimension_semantics=sem, vmem_limit_bytes=VMEM_LIMIT)


def _sigmoid(x):
    return 1.0 / (1.0 + jnp.exp(-x))


def _rms(x, w):
    return x * lax.rsqrt(jnp.mean(x * x, axis=-1, keepdims=True) + NORM_EPS) * w


def _norm_mod(x, w, shift, scale):
    return _rms(x, w) * (1.0 + scale) + shift


def _dot(a, b):
    return jnp.dot(a, b, preferred_element_type=F32)


def _dot_nt(a, b):
    return lax.dot_general(a, b, (((1,), (1,)), ((), ())), preferred_element_type=F32)


def _dot_f32(a, b):
    return jnp.dot(a, b, preferred_element_type=F32, precision=lax.Precision.HIGHEST)


def _pack_pair(lo, hi):
    ulo = lax.bitcast_convert_type(lo.astype(BF16).astype(F32), U32) >> 16
    uhi = lax.bitcast_convert_type(hi.astype(BF16).astype(F32), U32) & jnp.uint32(0xFFFF0000)
    return ulo | uhi


def _unpack_pair(p):
    lo = lax.bitcast_convert_type(p << 16, F32)
    hi = lax.bitcast_convert_type(p & jnp.uint32(0xFFFF0000), F32)
    return lo, hi


def _mod_kernel(c_ref, w_ref, b_ref, o_ref):
    c = c_ref[...]
    cond = c * _sigmoid(c)
    o_ref[0] = _dot(cond.astype(BF16), w_ref[0].astype(BF16)) + b_ref[0]


def _adaln_mod(c, ada_w, ada_b):
    depth, d, n = ada_w.shape
    b = c.shape[0]
    tn = 1024
    out = pl.pallas_call(
        _mod_kernel,
        grid=(depth, n // tn),
        in_specs=[pl.BlockSpec((b, d), lambda l, j: (0, 0)),
                  pl.BlockSpec((1, d, tn), lambda l, j: (l, 0, j)),
                  pl.BlockSpec((1, 1, tn), lambda l, j: (l, 0, j))],
        out_specs=pl.BlockSpec((1, b, tn), lambda l, j: (l, 0, j)),
        out_shape=jax.ShapeDtypeStruct((depth, b, n), F32),
        compiler_params=_cparams("arbitrary", "arbitrary"),
        name="adaln_mod",
    )(c, ada_w, ada_b.reshape(depth, 1, n))
    return out.reshape(depth, b, N_MOD, d)


def _rope_kernel(pos_ref, cos_ref, sin_ref):
    pos = pos_ref[...].astype(F32)
    lane = lax.broadcasted_iota(I32, (1, LANES), 1)
    j = (lane & (QK_ROPE_DIM // 2 - 1)).astype(F32)
    inv = jnp.exp(j * (-2.0 / QK_ROPE_DIM * math.log(ROPE_THETA)))
    ang = pos * inv
    cos_ref[...] = jnp.cos(ang)
    sin_ref[...] = jnp.sin(ang)


def _rope_tables(positions):
    t = positions.size
    tm = 1024
    spec = pl.BlockSpec((tm, LANES), lambda i: (i, 0))
    return pl.pallas_call(
        _rope_kernel,
        grid=(t // tm,),
        in_specs=[pl.BlockSpec((tm, 1), lambda i: (i, 0))],
        out_specs=[spec, spec],
        out_shape=[jax.ShapeDtypeStruct((t, LANES), F32)] * 2,
        compiler_params=_cparams("arbitrary"),
        name="rope_tables",
    )(positions.reshape(t, 1))


def _mla_proj_kernel(x_ref, mod_ref, nw_ref, win_ref, qn_ref, wqa_ref, wqb_ref, kvn_ref, wkv_ref,
                     cos_ref, sin_ref, q_ref, k_ref, v_ref):
    mod = mod_ref[0]
    h = _norm_mod(x_ref[...], nw_ref[...], mod[0:1], mod[1:2])
    lat = _dot(h.astype(BF16), win_ref[...])
    q_lat = lat[:, :Q_LORA_RANK]
    kv_lat = lat[:, Q_LORA_RANK:Q_LORA_RANK + KV_LORA_RANK]
    kp = lat[:, Q_LORA_RANK + KV_LORA_RANK:]
    cos = cos_ref[...]
    sin = sin_ref[...]
    scale = (QK_NOPE_DIM + QK_ROPE_DIM) ** -0.5

    qn = _rms(q_lat, qn_ref[...]).astype(BF16)
    qa = _dot(qn, wqa_ref[...])
    qb = _dot(qn, wqb_ref[...])
    for hh in range(MLA_HEADS):
        o = hh * 2 * LANES
        q_ref[:, o:o + LANES] = (qa[:, o:o + LANES] * scale).astype(BF16)
        pe = qa[:, o + LANES:o + 2 * LANES] * cos + qb[:, hh * LANES:(hh + 1) * LANES] * sin
        q_ref[:, o + LANES:o + 2 * LANES] = (pe * scale).astype(BF16)

    kvn = _rms(kv_lat, kvn_ref[...]).astype(BF16)
    kv = _dot(kvn, wkv_ref[...])
    lane = lax.broadcasted_iota(I32, (1, LANES), 1)
    first = lane < QK_ROPE_DIM
    u = kp * jnp.where(first, cos, sin)
    kr = jnp.where(first, u + pltpu.roll(u, QK_ROPE_DIM, 1), 0.0).astype(BF16)
    nk = MLA_HEADS * QK_NOPE_DIM
    for hh in range(MLA_HEADS):
        o = hh * 2 * LANES
        k_ref[:, o:o + LANES] = kv[:, hh * LANES:(hh + 1) * LANES].astype(BF16)
        k_ref[:, o + LANES:o + 2 * LANES] = kr
    v_ref[...] = kv[:, nk:].T.astype(BF16)


def _rotate_half_cols(w):
    half = w.shape[-1] // 2
    return jnp.concatenate([-w[..., half:], w[..., :half]], axis=-1)


def _mla_weights(w_in, w_uq, w_ukv):
    kpe = w_in[:, Q_LORA_RANK + KV_LORA_RANK:]
    w_in_ext = jnp.concatenate([w_in, _rotate_half_cols(kpe)], axis=1).astype(BF16)
    wq = w_uq.reshape(Q_LORA_RANK, MLA_HEADS, QK_NOPE_DIM + QK_ROPE_DIM)
    zeros = jnp.zeros((Q_LORA_RANK, MLA_HEADS, LANES - QK_ROPE_DIM), w_uq.dtype)
    wq_pe = wq[:, :, QK_NOPE_DIM:]
    wqa = jnp.concatenate([wq[:, :, :QK_NOPE_DIM], wq_pe, zeros], axis=2)
    wqb = jnp.concatenate([_rotate_half_cols(wq_pe), zeros], axis=2)
    wkv = w_ukv.reshape(KV_LORA_RANK, MLA_HEADS, QK_NOPE_DIM + V_HEAD_DIM)
    wkv = jnp.concatenate([wkv[:, :, :QK_NOPE_DIM].reshape(KV_LORA_RANK, -1),
                           wkv[:, :, QK_NOPE_DIM:].reshape(KV_LORA_RANK, -1)], axis=1)
    return (w_in_ext, wqa.reshape(Q_LORA_RANK, -1).astype(BF16),
            wqb.reshape(Q_LORA_RANK, -1).astype(BF16), wkv.astype(BF16))


def _mla_proj(x, mod, norm_w, weights, q_norm, kv_norm, cos, sin, seq):
    t, d = x.shape
    tm = TOKEN_TILE
    w_in_ext, wqa, wqb, wkv = weights
    per_seq = seq // tm
    full = lambda a: pl.BlockSpec(a.shape, lambda i: (0,) * a.ndim)
    row = lambda n: pl.BlockSpec((tm, n), lambda i: (i, 0))
    nq = MLA_HEADS * 2 * LANES
    nv = MLA_HEADS * V_HEAD_DIM
    args = (x, mod, norm_w.reshape(1, d), w_in_ext, q_norm.reshape(1, -1), wqa, wqb,
            kv_norm.reshape(1, -1), wkv, cos, sin)
    in_specs = [row(d), pl.BlockSpec((1, N_MOD, d), lambda i: (i // per_seq, 0, 0)), full(args[2]),
                full(w_in_ext), full(args[4]), full(wqa), full(wqb), full(args[7]), full(wkv),
                row(LANES), row(LANES)]
    return pl.pallas_call(
        _mla_proj_kernel,
        grid=(t // tm,),
        in_specs=in_specs,
        out_specs=[row(nq), row(nq), pl.BlockSpec((nv, tm), lambda i: (0, i))],
        out_shape=[jax.ShapeDtypeStruct((t, nq), BF16), jax.ShapeDtypeStruct((t, nq), BF16),
                   jax.ShapeDtypeStruct((nv, t), BF16)],
        compiler_params=_cparams("arbitrary"),
        name="mla_proj",
    )(*args)


def _attn_kernel(q_ref, k_ref, vt_ref, o_ref):
    tq = q_ref.shape[0]
    dq = 2 * LANES
    dv = V_HEAD_DIM
    qi = pl.program_id(2)
    heads = range(ATTN_HEADS)
    q = [q_ref[:, h * dq:(h + 1) * dq] for h in heads]

    def step(r0, carry, mask):
        s = [_dot_nt(k_ref[pl.ds(r0, tq), h * dq:(h + 1) * dq], q[h]) for h in heads]
        if mask is not None:
            s = [jnp.where(mask, sh, NEG_BIG) for sh in s]
        m_new = [jnp.maximum(carry[h][0], jnp.max(s[h], axis=0, keepdims=True)) for h in heads]
        p = [jnp.exp(s[h] - m_new[h]) for h in heads]
        alpha = [jnp.exp(carry[h][0] - m_new[h]) for h in heads]
        l = [alpha[h] * carry[h][1] + jnp.sum(p[h], axis=0, keepdims=True) for h in heads]
        acc = [alpha[h] * carry[h][2] + _dot(vt_ref[h * dv:(h + 1) * dv, pl.ds(r0, tq)], p[h].astype(BF16))
               for h in heads]
        return tuple((m_new[h], l[h], acc[h]) for h in heads)

    init = tuple((jnp.full((1, tq), NEG_BIG, F32), jnp.zeros((1, tq), F32), jnp.zeros((dv, tq), F32))
                 for _ in heads)
    carry = lax.fori_loop(0, qi, lambda j, c: step(pl.multiple_of(j * tq, tq), c, None), init)
    causal = (lax.broadcasted_iota(I32, (tq, tq), 0) <= lax.broadcasted_iota(I32, (tq, tq), 1))
    carry = step(pl.multiple_of(qi * tq, tq), carry, causal)
    for h in heads:
        _, l, acc = carry[h]
        o_ref[:, h * dv:(h + 1) * dv] = (acc / l).T.astype(o_ref.dtype)


def _attention(q, k, v, batch, seq):
    t = q.shape[0]
    tq = ATTN_TILE
    nq = seq // tq
    dq = ATTN_HEADS * 2 * LANES
    dv = ATTN_HEADS * V_HEAD_DIM
    return pl.pallas_call(
        _attn_kernel,
        grid=(batch, MLA_HEADS // ATTN_HEADS, nq),
        in_specs=[pl.BlockSpec((tq, dq), lambda b, h, i: (b * nq + i, h)),
                  pl.BlockSpec((seq, dq), lambda b, h, i: (b, h)),
                  pl.BlockSpec((dv, seq), lambda b, h, i: (h, b))],
        out_specs=pl.BlockSpec((tq, dv), lambda b, h, i: (b * nq + i, h)),
        out_shape=jax.ShapeDtypeStruct((t, MLA_HEADS * V_HEAD_DIM), BF16),
        compiler_params=_cparams("arbitrary", "arbitrary", "arbitrary"),
        name="mla_attention",
    )(q, k, v)


def _post_mixer_kernel(o_ref, x_ref, mod_ref, wo_ref, nw_ref, rw_ref, rb_ref,
                       x1_ref, h_ref, route_ref, tcnt_ref, trun_ref, cnt_ref, run_ref):
    tm, d = x_ref.shape
    i = pl.program_id(0)

    @pl.when(i == 0)
    def _():
        run_ref[...] = jnp.zeros_like(run_ref)

    mod = mod_ref[0]
    x1 = x_ref[...] + mod[2:3] * _dot(o_ref[...], wo_ref[...])
    x1_ref[...] = x1
    h = _norm_mod(x1, nw_ref[...], mod[3:4], mod[4:5])
    h_ref[...] = h.astype(BF16)

    h_hi = h.astype(BF16)
    h_lo = (h - h_hi.astype(F32)).astype(BF16)
    parts = _dot(h_hi, rw_ref[...]) + _dot(h_lo, rw_ref[...])
    lane = lax.broadcasted_iota(I32, (tm, LANES), 1)
    logits = jnp.where(lane < N_EXPERTS, parts + pltpu.roll(parts, LANES - N_EXPERTS, 1) + rb_ref[...], NEG_BIG)
    lane_f = lane.astype(F32)
    work = logits
    val, hot = [], []
    for _ in range(TOP_K):
        m = jnp.max(work, axis=-1, keepdims=True)
        a = jnp.min(jnp.where(work == m, lane_f, float(LANES)), axis=-1, keepdims=True)
        hot.append(lane_f == a)
        val.append(m)
        work = jnp.where(hot[-1], -jnp.inf, work)
    ex = [jnp.exp(v - val[0]) for v in val]
    den = ex[0] + ex[1] + ex[2] + ex[3]

    onehot = jnp.where(hot[0] | hot[1] | hot[2] | hot[3], 1.0, 0.0)
    row = lax.broadcasted_iota(I32, (tm, tm), 0)
    col = lax.broadcasted_iota(I32, (tm, tm), 1)
    tri = jnp.where(col < row, 1.0, 0.0).astype(BF16)
    rank = _dot(tri, onehot.astype(BF16))
    cnt = jnp.sum(onehot, axis=0, keepdims=True)
    cnt = jnp.floor((cnt + (SUBLANES - 1)) * (1.0 / SUBLANES)) * SUBLANES
    e_row = lax.broadcasted_iota(I32, (LANES, LANES), 0)
    e_col = lax.broadcasted_iota(I32, (LANES, LANES), 1)
    before = jnp.where(e_row < e_col, 1.0, 0.0)
    off = _dot_f32(jnp.broadcast_to(cnt, (SUBLANES, LANES)), before)[0:1]
    local = off + rank

    route = jnp.zeros((tm, LANES), F32)
    for kk in range(TOP_K):
        pos = jnp.sum(jnp.where(hot[kk], local, 0.0), axis=-1, keepdims=True)
        route = jnp.where(lane == kk, pos, route)
        route = jnp.where(lane == TOP_K + kk, ex[kk] / den, route)
    route_ref[...] = route
    tcnt_ref[0] = cnt.astype(I32)
    trun_ref[0] = run_ref[...].astype(I32)
    run = run_ref[...] + cnt
    run_ref[...] = run
    cnt_ref[...] = run.astype(I32)


def _post_mixer(o, x, mod, w_o, norm_w, router_w, router_b, seq):
    t, d = x.shape
    ko = o.shape[1]
    tm = ROUTE_TILE
    per_seq = seq // tm
    n_tiles = t // tm
    rw_hi = router_w.astype(BF16)
    rw_lo = (router_w - rw_hi.astype(F32)).astype(BF16)
    rw = jnp.zeros((d, LANES), BF16).at[:, :N_EXPERTS].set(rw_hi).at[:, N_EXPERTS:2 * N_EXPERTS].set(rw_lo)
    rb = jnp.zeros((1, LANES), F32).at[0, :N_EXPERTS].set(router_b)
    row = lambda n: pl.BlockSpec((tm, n), lambda i: (i, 0))
    const = lambda r, c: pl.BlockSpec((r, c), lambda i: (0, 0))
    per_tile = pl.BlockSpec((1, 1, LANES), lambda i: (i, 0, 0))
    return pl.pallas_call(
        _post_mixer_kernel,
        grid=(n_tiles,),
        in_specs=[row(ko), row(d), pl.BlockSpec((1, N_MOD, d), lambda i: (i // per_seq, 0, 0)),
                  const(ko, d), const(1, d), const(d, LANES), const(1, LANES)],
        out_specs=[row(d), row(d), row(LANES), per_tile, per_tile, const(1, LANES)],
        out_shape=[jax.ShapeDtypeStruct((t, d), F32), jax.ShapeDtypeStruct((t, d), BF16),
                   jax.ShapeDtypeStruct((t, LANES), F32),
                   jax.ShapeDtypeStruct((n_tiles, 1, LANES), I32),
                   jax.ShapeDtypeStruct((n_tiles, 1, LANES), I32),
                   jax.ShapeDtypeStruct((1, LANES), I32)],
        scratch_shapes=[pltpu.VMEM((1, LANES), F32)],
        compiler_params=_cparams("arbitrary"),
        name="post_mixer_router",
    )(o, x, mod, w_o.astype(BF16), norm_w.reshape(1, d), rw, rb)


def _tile_ceil(n):
    return (n + (MOE_TILE - 1)) // MOE_TILE * MOE_TILE


def _expert_starts(counts_ref, starts_ref):
    def body(e, acc):
        starts_ref[e] = acc
        return acc + _tile_ceil(counts_ref[e])
    return lax.fori_loop(0, N_EXPERTS, body, jnp.int32(0))


def _segment_copies(tile, tcnt_ref, trun_ref, starts_ref, visit):
    def per_expert(e, off):
        n = tcnt_ref[tile, e]
        base = starts_ref[e] + trun_ref[tile, e]
        for piece in SEG_PIECES:
            done = n & (-2 * piece)

            @pl.when((n & piece) != 0)
            def _():
                visit(pl.multiple_of(off + done, SUBLANES), pl.multiple_of(base + done, SUBLANES), piece)
        return off + n
    lax.fori_loop(0, N_EXPERTS, per_expert, jnp.int32(0))


def _dispatch_kernel(tcnt_ref, trun_ref, counts_ref, route_ref, h_ref, xs_ref, g_ref, zero_ref, starts_ref, sem):
    i = pl.program_id(0)
    n = pl.num_programs(0)
    tm, d = h_ref.shape
    slot = lax.rem(i, 2)

    def copies(tile, s, act):
        def visit(src, dst, nrows):
            act(pltpu.make_async_copy(g_ref.at[s, pl.ds(src, nrows)], xs_ref.at[pl.ds(dst, nrows)], sem.at[s]))
        _segment_copies(tile, tcnt_ref, trun_ref, starts_ref, visit)

    start = lambda c: c.start()
    wait = lambda c: c.wait()

    @pl.when(i == 0)
    def _():
        _expert_starts(counts_ref, starts_ref)

    @pl.when(i >= 2)
    def _():
        copies(i - 2, slot, wait)

    rt = route_ref[...].T
    r = lax.broadcasted_iota(I32, (SEG_ROWS, tm), 0).astype(F32)
    hit = (r == rt[0:1]) | (r == rt[1:2]) | (r == rt[2:3]) | (r == rt[3:4])
    g = _dot(jnp.where(hit, 1.0, 0.0).astype(BF16), h_ref[...])
    g_ref[slot] = _pack_pair(g[:, :d // 2], g[:, d // 2:])
    copies(i, slot, start)

    @pl.when(i == n - 1)
    def _():
        copies(i, slot, wait)

        @pl.when(i >= 1)
        def _():
            copies(i - 1, 1 - slot, wait)
        zero_ref[...] = jnp.zeros_like(zero_ref)

        def gap_fill(act):
            def per_expert(e, c):
                cnt = counts_ref[e]
                gap = _tile_ceil(cnt) - cnt
                base = starts_ref[e] + cnt
                for piece in GAP_PIECES:
                    done = gap & (-2 * piece)

                    @pl.when((gap & piece) != 0)
                    def _():
                        act(pltpu.make_async_copy(
                            zero_ref.at[pl.ds(0, piece)],
                            xs_ref.at[pl.ds(pl.multiple_of(base + done, SUBLANES), piece)], sem.at[slot]))
                return c
            lax.fori_loop(0, N_EXPERTS, per_expert, 0)

        used = starts_ref[N_EXPERTS - 1] + _tile_ceil(counts_ref[N_EXPERTS - 1])
        n_unused = (xs_ref.shape[0] - used) // MOE_TILE

        def tile_fill(act):
            def per_tile(kk, c):
                row = pl.multiple_of(used + kk * MOE_TILE, MOE_TILE)
                act(pltpu.make_async_copy(zero_ref, xs_ref.at[pl.ds(row, MOE_TILE)], sem.at[slot]))
                return c
            lax.fori_loop(0, n_unused, per_tile, 0)

        gap_fill(start)
        tile_fill(start)
        gap_fill(wait)
        tile_fill(wait)


def _sorted_rows(t):
    bound = t * TOP_K + (t // ROUTE_TILE) * N_EXPERTS * (SUBLANES - 1) + N_EXPERTS * (MOE_TILE - SUBLANES)
    return (bound + MOE_TILE - 1) // MOE_TILE * MOE_TILE


def _dispatch(h, route, tcnt, trun, counts):
    t, d = h.shape
    tm = ROUTE_TILE
    smem = pl.BlockSpec(memory_space=pltpu.SMEM)
    return pl.pallas_call(
        _dispatch_kernel,
        grid=(t // tm,),
        in_specs=[smem, smem, smem, pl.BlockSpec((tm, LANES), lambda i: (i, 0)),
                  pl.BlockSpec((tm, d), lambda i: (i, 0))],
        out_specs=pl.BlockSpec(memory_space=pl.ANY),
        out_shape=jax.ShapeDtypeStruct((_sorted_rows(t), d // 2), U32),
        scratch_shapes=[pltpu.VMEM((2, SEG_ROWS, d // 2), U32), pltpu.VMEM((MOE_TILE, d // 2), U32),
                        pltpu.SMEM((N_EXPERTS,), I32), pltpu.SemaphoreType.DMA((2,))],
        compiler_params=_cparams("arbitrary"),
        name="moe_dispatch",
    )(tcnt, trun, counts, route, h)


def _moe_schedule(counts, n_tiles, tile):
    ids = jnp.arange(N_EXPERTS, dtype=I32)
    n_own = (counts + tile - 1) // tile
    t_end = jnp.cumsum(n_own)
    t_start = t_end - n_own
    total = t_end[-1]
    later = jnp.where((ids[None, :] > ids[:, None]) & (counts[None, :] > 0), ids[None, :], N_EXPERTS)
    next_e = jnp.min(later, axis=1)
    next_e = jnp.where(next_e == N_EXPERTS, -1, next_e)
    step = jnp.arange(n_tiles, dtype=I32)
    live = step < total
    v = jnp.minimum(step, total - 1)
    e = jnp.minimum(jnp.sum(t_end[None, :] <= v[:, None], axis=1), N_EXPERTS - 1).astype(I32)
    pick = e[:, None] == ids[None, :]
    of_e = lambda a: jnp.sum(jnp.where(pick, a[None, :], 0), axis=1)
    new_expert = (live & (step == of_e(t_start))).astype(I32)
    return e, new_expert, live.astype(I32), of_e(next_e).astype(I32)


def _moe_kernel(exp_ref, newe_ref, live_ref, next_ref,
                x_ref, wgu_hbm, bgu_ref, wd_hbm, bd_ref, y_ref,
                wgu_f32, wd_f32, wgu_bf, wd_bf, sem, *, layer):
    v = pl.program_id(0)
    f = wd_bf.shape[0]
    half = x_ref.shape[1]

    def weight_copies(e):
        return (pltpu.make_async_copy(wgu_hbm.at[layer, e], wgu_f32, sem.at[0]),
                pltpu.make_async_copy(wd_hbm.at[layer, e], wd_f32, sem.at[1]))

    @pl.when(newe_ref[v] == 1)
    def _():
        e = exp_ref[v]

        @pl.when(v == 0)
        def _():
            for c in weight_copies(e):
                c.start()
        for c in weight_copies(e):
            c.wait()
        wgu_bf[...] = wgu_f32[...].astype(BF16)
        wd_bf[...] = wd_f32[...].astype(BF16)
        nxt = next_ref[v]

        @pl.when(nxt >= 0)
        def _():
            for c in weight_copies(nxt):
                c.start()

    @pl.when(live_ref[v] == 1)
    def _():
        x_lo, x_hi = _unpack_pair(x_ref[...])
        x = jnp.concatenate([x_lo.astype(BF16), x_hi.astype(BF16)], axis=1)
        gu = _dot(x, wgu_bf[...]) + bgu_ref[0]
        gate = jnp.minimum(gu[:, :f], SWIGLU_LIMIT)
        lin = jnp.clip(gu[:, f:], -SWIGLU_LIMIT, SWIGLU_LIMIT)
        act = gate * _sigmoid(SWIGLU_ALPHA * gate) * (lin + 1.0)
        y = _dot(act.astype(BF16), wd_bf[...]) + bd_ref[0]
        y_ref[...] = _pack_pair(y[:, :half], y[:, half:])

    @pl.when(live_ref[v] == 0)
    def _():
        y_ref[...] = jnp.zeros_like(y_ref)


def _moe_experts(xs, sched, w_gate_up, b_gate_up, w_down, b_down, layer):
    a, half = xs.shape
    depth, e, d, f2 = w_gate_up.shape
    f = f2 // 2
    tile = MOE_TILE
    grid_spec = pltpu.PrefetchScalarGridSpec(
        num_scalar_prefetch=4,
        grid=(a // tile,),
        in_specs=[pl.BlockSpec((tile, half), lambda v, ex, *_: (v, 0)),
                  pl.BlockSpec(memory_space=pl.ANY),
                  pl.BlockSpec((None, 1, 1, f2), lambda v, ex, *_: (layer, ex[v], 0, 0)),
                  pl.BlockSpec(memory_space=pl.ANY),
                  pl.BlockSpec((None, 1, 1, d), lambda v, ex, *_: (layer, ex[v], 0, 0))],
        out_specs=pl.BlockSpec((tile, half), lambda v, ex, *_: (v, 0)),
        scratch_shapes=[pltpu.VMEM((d, f2), F32), pltpu.VMEM((f, d), F32),
                        pltpu.VMEM((d, f2), BF16), pltpu.VMEM((f, d), BF16),
                        pltpu.SemaphoreType.DMA((2,))],
    )
    return pl.pallas_call(
        functools.partial(_moe_kernel, layer=layer),
        grid_spec=grid_spec,
        out_shape=jax.ShapeDtypeStruct((a, half), U32),
        compiler_params=_cparams("arbitrary"),
        name="moe_experts",
    )(*sched, xs, w_gate_up, b_gate_up.reshape(depth, e, 1, f2), w_down, b_down.reshape(depth, e, 1, d))


def _combine_kernel(tcnt_ref, trun_ref, counts_ref, ys_ref, route_ref, x_ref, mod_ref, fw_ref,
                    o_ref, y_buf, starts_ref, sem, *, final_norm):
    i = pl.program_id(0)
    n = pl.num_programs(0)
    tm, d = x_ref.shape
    slot = lax.rem(i, 2)

    def copies(tile, s, act):
        def visit(dst, src, nrows):
            act(pltpu.make_async_copy(ys_ref.at[pl.ds(src, nrows)], y_buf.at[s, pl.ds(dst, nrows)], sem.at[s]))
        _segment_copies(tile, tcnt_ref, trun_ref, starts_ref, visit)

    @pl.when(i == 0)
    def _():
        _expert_starts(counts_ref, starts_ref)
        y_buf[...] = jnp.zeros_like(y_buf)
        copies(0, 0, lambda c: c.start())

    @pl.when(i + 1 < n)
    def _():
        copies(i + 1, 1 - slot, lambda c: c.start())

    copies(i, slot, lambda c: c.wait())
    y_lo, y_hi = _unpack_pair(y_buf[slot])
    route = route_ref[...]
    r = lax.broadcasted_iota(I32, (tm, SEG_ROWS), 1).astype(F32)
    w = jnp.zeros((tm, SEG_ROWS), F32)
    for kk in range(TOP_K):
        w = jnp.where(r == route[:, kk:kk + 1], route[:, TOP_K + kk:TOP_K + kk + 1], w)
    w = w.astype(BF16)
    y = jnp.concatenate([_dot(w, y_lo.astype(BF16)), _dot(w, y_hi.astype(BF16))], axis=1)
    out = x_ref[...] + mod_ref[0][5:6] * y
    if final_norm:
        out = _rms(out, fw_ref[...])
    o_ref[...] = out


def _combine(ys, route, tcnt, trun, counts, x1, mod, final_w, seq, final_norm):
    t, d = x1.shape
    tm = ROUTE_TILE
    per_seq = seq // tm
    smem = pl.BlockSpec(memory_space=pltpu.SMEM)
    row = lambda n: pl.BlockSpec((tm, n), lambda i: (i, 0))
    return pl.pallas_call(
        functools.partial(_combine_kernel, final_norm=final_norm),
        grid=(t // tm,),
        in_specs=[smem, smem, smem, pl.BlockSpec(memory_space=pl.ANY), row(LANES), row(d),
                  pl.BlockSpec((1, N_MOD, d), lambda i: (i // per_seq, 0, 0)),
                  pl.BlockSpec((1, d), lambda i: (0, 0))],
        out_specs=row(d),
        out_shape=jax.ShapeDtypeStruct((t, d), F32),
        scratch_shapes=[pltpu.VMEM((2, SEG_ROWS, d // 2), U32), pltpu.SMEM((N_EXPERTS,), I32),
                        pltpu.SemaphoreType.DMA((2,))],
        compiler_params=_cparams("arbitrary"),
        name="moe_combine",
    )(tcnt, trun, counts, ys, route, x1, mod, final_w.reshape(1, d))


def _moe_block(o, x, mod, w_o, norm_w, router_w, router_b, w_gate_up, b_gate_up, w_down, b_down,
               final_w, seq, layer, final_norm):
    x1, h, route, tcnt, trun, cnt = _post_mixer(o, x, mod, w_o, norm_w, router_w, router_b, seq)
    n_tiles = tcnt.shape[0]
    tcnt = tcnt.reshape(n_tiles, LANES)
    trun = trun.reshape(n_tiles, LANES)
    counts = cnt[0, :N_EXPERTS]
    xs = _dispatch(h, route, tcnt, trun, counts)
    sched = _moe_schedule(counts, xs.shape[0] // MOE_TILE, MOE_TILE)
    ys = _moe_experts(xs, sched, w_gate_up, b_gate_up, w_down, b_down, layer)
    return _combine(ys, route, tcnt, trun, counts, x1, mod, final_w, seq, final_norm)


def _gdn_conv_kernel(x_ref, mod_ref, nw_ref, win_ref, cw_ref, q_ref, k_ref, v_ref, buf_ref, *, per_seq):
    tm = x_ref.shape[0]
    nqk = q_ref.shape[1]
    nconv = win_ref.shape[1]
    i = pl.program_id(0)
    mod = mod_ref[0]
    h = _norm_mod(x_ref[...], nw_ref[...], mod[0:1], mod[1:2])

    @pl.when(i % per_seq == 0)
    def _():
        buf_ref[0:SUBLANES, :] = jnp.zeros((SUBLANES, nconv), F32)

    buf_ref[SUBLANES:SUBLANES + tm, :] = _dot(h.astype(BF16), win_ref[...])
    cw = cw_ref[...]
    window = buf_ref[0:SUBLANES + tm, :]
    acc = cw[GDN_CONV - 1:GDN_CONV, :] * window[SUBLANES:, :]
    for back in range(1, GDN_CONV):
        acc = acc + cw[GDN_CONV - 1 - back:GDN_CONV - back, :] * pltpu.roll(window, back, 0)[SUBLANES:, :]
    buf_ref[0:SUBLANES, :] = buf_ref[tm:tm + SUBLANES, :]
    y = acc * _sigmoid(acc)

    def l2(a):
        return a * lax.rsqrt(jnp.sum(a * a, axis=-1, keepdims=True) + 1e-6)

    for hh in range(nqk // GDN_HEAD_DIM):
        s = slice(hh * GDN_HEAD_DIM, (hh + 1) * GDN_HEAD_DIM)
        q_ref[:, s] = (l2(y[:, s]) * GDN_HEAD_DIM ** -0.5).astype(BF16)
        k_ref[:, s] = l2(y[:, nqk + hh * GDN_HEAD_DIM:nqk + (hh + 1) * GDN_HEAD_DIM]).astype(BF16)
    v_ref[...] = y[:, 2 * nqk:].astype(BF16)


def _gdn_gate_kernel(x_ref, mod_ref, nw_ref, win_ref, alog_ref, dtb_ref, z_ref, bg_ref):
    nv = z_ref.shape[1]
    mod = mod_ref[0]
    h = _norm_mod(x_ref[...], nw_ref[...], mod[0:1], mod[1:2])
    pr = _dot(h.astype(BF16), win_ref[...])
    z_ref[...] = pr[:, :nv].astype(BF16)
    ba = pr[:, nv:]
    sp = ba + dtb_ref[...]
    softplus = jnp.maximum(sp, 0.0) + jnp.log(1.0 + jnp.exp(-jnp.abs(sp)))
    lane = lax.broadcasted_iota(I32, (1, LANES), 1)
    log_decay = -jnp.exp(alog_ref[...]) * softplus
    bg_ref[...] = jnp.where(lane < GDN_V_HEADS, _sigmoid(ba), _chunk_cumsum(log_decay))


def _gdn_proj(x, mod, norm_w, w_in, conv_w, a_log, dt_bias, seq):
    t, d = x.shape
    tm = TOKEN_TILE
    per_seq = seq // tm
    nqk = GDN_K_HEADS * GDN_HEAD_DIM
    nv = GDN_V_HEADS * GDN_HEAD_DIM
    nconv = 2 * nqk + nv
    w_conv = w_in[:, :nconv].astype(BF16)
    w_gate = jnp.zeros((d, nv + LANES), BF16).at[:, :w_in.shape[1] - nconv].set(w_in[:, nconv:].astype(BF16))
    pad = jnp.zeros((1, LANES), F32)
    alog = pad.at[0, GDN_V_HEADS:2 * GDN_V_HEADS].set(a_log)
    dtb = pad.at[0, GDN_V_HEADS:2 * GDN_V_HEADS].set(dt_bias)
    row = lambda n: pl.BlockSpec((tm, n), lambda i: (i, 0))
    const = lambda a: pl.BlockSpec(a.shape, lambda i: (0,) * a.ndim)
    modspec = pl.BlockSpec((1, N_MOD, d), lambda i: (i // per_seq, 0, 0))
    nw = norm_w.reshape(1, d)
    q, k, v = pl.pallas_call(
        functools.partial(_gdn_conv_kernel, per_seq=per_seq),
        grid=(t // tm,),
        in_specs=[row(d), modspec, const(nw), const(w_conv), const(conv_w)],
        out_specs=[row(nqk), row(nqk), row(nv)],
        out_shape=[jax.ShapeDtypeStruct((t, nqk), BF16), jax.ShapeDtypeStruct((t, nqk), BF16),
                   jax.ShapeDtypeStruct((t, nv), BF16)],
        scratch_shapes=[pltpu.VMEM((tm + 2 * SUBLANES, nconv), F32)],
        compiler_params=_cparams("arbitrary"),
        name="gdn_conv_proj",
    )(x, mod, nw, w_conv, conv_w)
    z, bg = pl.pallas_call(
        _gdn_gate_kernel,
        grid=(t // tm,),
        in_specs=[row(d), modspec, const(nw), const(w_gate), const(alog), const(dtb)],
        out_specs=[row(nv), row(LANES)],
        out_shape=[jax.ShapeDtypeStruct((t, nv), BF16), jax.ShapeDtypeStruct((t, LANES), F32)],
        compiler_params=_cparams("arbitrary"),
        name="gdn_gate_proj",
    )(x, mod, nw, w_gate, alog, dtb)
    return q, k, v, z, bg


def _head_col(block, lane_idx):
    lane = lax.broadcasted_iota(I32, (1, LANES), 1)
    col = jnp.sum(jnp.where(lane == lane_idx, block, 0.0), axis=-1, keepdims=True)
    return jnp.broadcast_to(col, block.shape)


def _chunk_cumsum(x):
    rin = lax.broadcasted_iota(I32, x.shape, 0) & (GDN_CHUNK - 1)
    s = 1
    while s < GDN_CHUNK:
        x = x + jnp.where(rin >= s, pltpu.roll(x, s, 0), 0.0)
        s *= 2
    return x


def _diag_blocks(m):
    c = GDN_CHUNK
    return jnp.concatenate([m[b * c:(b + 1) * c, b * c:(b + 1) * c] for b in range(m.shape[0] // c)], axis=0)


def _chunk_decay(gc):
    n = gc.shape[0]
    c = GDN_CHUNK
    gt = gc.T
    gj = jnp.concatenate([jnp.broadcast_to(gt[0:1, b * c:(b + 1) * c], (c, c)) for b in range(n // c)], axis=0)
    i_in = lax.broadcasted_iota(I32, (n, c), 0) & (c - 1)
    col = lax.broadcasted_iota(I32, (n, c), 1)
    lower = col <= i_in
    decay = jnp.where(lower, jnp.exp(jnp.where(lower, gc[:, :c] - gj, 0.0)), 0.0)
    return decay, col < i_in


def _gdn_l_kernel(k_ref, bg_ref, l_ref):
    kh = pl.program_id(1)
    sc = GDN_SUPER
    c = GDN_CHUNK

    def body(s, carry):
        r0 = pl.multiple_of(s * sc, sc)
        k = k_ref[pl.ds(r0, sc), :]
        bg = bg_ref[pl.ds(r0, sc), :]
        kk = _diag_blocks(_dot_nt(k, k))
        reps = range(GDN_REP)
        beta = [_head_col(bg, kh * GDN_REP + r) for r in reps]
        gc = [_head_col(bg, GDN_V_HEADS + kh * GDN_REP + r) for r in reps]
        dec = [_chunk_decay(x) for x in gc]
        for r in reps:
            decay, strict = dec[r]
            l_ref[0, r, pl.ds(r0, sc), :] = jnp.where(strict, kk * beta[r][:, :c] * decay, 0.0)
        return carry
    lax.fori_loop(0, k_ref.shape[0] // sc, body, 0)


def _gdn_l(k, bg, batch, seq):
    return pl.pallas_call(
        _gdn_l_kernel,
        grid=(batch, GDN_K_HEADS),
        in_specs=[pl.BlockSpec((seq, GDN_HEAD_DIM), lambda b, h: (b, h)),
                  pl.BlockSpec((seq, LANES), lambda b, h: (b, 0))],
        out_specs=pl.BlockSpec((1, GDN_REP, seq, GDN_CHUNK), lambda b, h: (b, h, 0, 0)),
        out_shape=jax.ShapeDtypeStruct((batch, GDN_V_HEADS, seq, GDN_CHUNK), F32),
        compiler_params=_cparams("arbitrary", "arbitrary"),
        name="gdn_chunk_l",
    )(k, bg)


def _tri_inv_kernel(a_ref, x_ref):
    c, n = a_ref.shape[1], a_ref.shape[2]
    sub_row = lax.broadcasted_iota(I32, (SUBLANES, n), 0)
    for c0 in range(0, c, SUBLANES):
        cols = slice(c0, c0 + SUBLANES)
        for i in range(c):
            if i < c0:
                x_ref[i, cols, :] = jnp.zeros((SUBLANES, n), F32)
                continue

            def sub(j, acc, i=i, cols=cols):
                return acc - a_ref[i, pl.ds(j, 1), :] * x_ref[j, cols, :]
            row = jnp.where(sub_row == i - c0, 1.0, 0.0)
            if i > c0:
                row = lax.fori_loop(c0, i, sub, row, unroll=min(i - c0, 8))
            x_ref[i, cols, :] = row


def _tri_inv(a):
    c, _, n = a.shape
    lanes = min(INV_LANES, n)
    spec = pl.BlockSpec((c, c, lanes), lambda i: (0, 0, i))
    return pl.pallas_call(
        _tri_inv_kernel,
        grid=(n // lanes,),
        in_specs=[spec],
        out_specs=spec,
        out_shape=jax.ShapeDtypeStruct(a.shape, F32),
        compiler_params=_cparams("arbitrary"),
        name="gdn_tri_inv",
    )(a)


def _gdn_scan_kernel(q_ref, k_ref, v_ref, z_ref, bg_ref, ti_ref, nw_ref, o_ref):
    grp = pl.program_id(1)
    sc = GDN_SUPER
    nc = sc // GDN_CHUNK
    hd = GDN_HEAD_DIM
    n_super = q_ref.shape[0] // sc
    nw = nw_ref[...]
    r_blk = lax.broadcasted_iota(I32, (sc, sc), 0) // GDN_CHUNK
    c_blk = lax.broadcasted_iota(I32, (sc, sc), 1) // GDN_CHUNK
    same = r_blk == c_blk

    heads = range(GDN_SCAN_HEADS)

    def prepare_head(s, j, shared):
        r0 = pl.multiple_of(s * sc, sc)
        kh = j // GDN_REP
        if kh not in shared:
            qb = q_ref[pl.ds(r0, sc), kh * hd:(kh + 1) * hd]
            kb = k_ref[pl.ds(r0, sc), kh * hd:(kh + 1) * hd]
            shared[kh] = (qb.astype(F32), kb.astype(F32), _diag_blocks(_dot_nt(qb, kb)))
        q, k, qk = shared[kh]
        head = grp * GDN_SCAN_HEADS + j
        bg = bg_ref[pl.ds(r0, sc), :]
        v = v_ref[pl.ds(r0, sc), j * hd:(j + 1) * hd].astype(F32)
        beta = _head_col(bg, head)
        gc = _head_col(bg, GDN_V_HEADS + head)
        egc = jnp.exp(gc)
        decay, _ = _chunk_decay(gc)
        a_intra = (qk * decay).astype(BF16)
        ti = ti_ref[0, j, pl.ds(r0, sc), :]
        t_inv = jnp.where(same, jnp.concatenate([ti] * nc, axis=1), 0.0).astype(BF16)
        uw = _dot(t_inv, jnp.concatenate([v * beta, k * beta * egc], axis=1).astype(BF16))
        g_last = [gc[(c + 1) * GDN_CHUNK - 1:(c + 1) * GDN_CHUNK, :] for c in range(nc)]
        gl_rows = jnp.concatenate([jnp.broadcast_to(g, (GDN_CHUNK, LANES)) for g in g_last], axis=0)
        k_dec_t = (k * jnp.exp(gl_rows - gc)).T.astype(BF16)
        eg_last = jnp.concatenate([jnp.exp(g) for g in g_last], axis=0)
        return uw[:, :hd], uw[:, hd:].astype(BF16), (q * egc).astype(BF16), k_dec_t, a_intra, eg_last

    def prepare(s):
        shared = {}
        return tuple(prepare_head(s, j, shared) for j in heads)

    def body(s, carry):
        states, prepared = carry
        states = list(states)
        r0 = pl.multiple_of(s * sc, sc)
        nxt = jnp.minimum(s + 1, n_super - 1)
        shared = {}
        following = []
        outs = [[] for _ in heads]
        for c in range(nc):
            rs = slice(c * GDN_CHUNK, (c + 1) * GDN_CHUNK)
            sb = [states[j].astype(BF16) for j in heads]
            v_new = [prepared[j][0][rs] - _dot(prepared[j][1][rs], sb[j]) for j in heads]
            vb = [v.astype(BF16) for v in v_new]
            for j in heads:
                outs[j].append(_dot(prepared[j][2][rs], sb[j]) + _dot(prepared[j][4][rs], vb[j]))
            states = [states[j] * prepared[j][5][c:c + 1, :] + _dot(prepared[j][3][:, rs], vb[j]) for j in heads]
            following.append(prepare_head(nxt, c, shared))
        for j in heads:
            o = jnp.concatenate(outs[j], axis=0)
            z = z_ref[pl.ds(r0, sc), j * hd:(j + 1) * hd].astype(F32)
            o_ref[pl.ds(r0, sc), j * hd:(j + 1) * hd] = (_rms(o, nw) * (z * _sigmoid(z))).astype(o_ref.dtype)
        return tuple(states), tuple(following)

    assert GDN_SCAN_HEADS == nc
    init = tuple(jnp.zeros((hd, hd), F32) for _ in heads)
    lax.fori_loop(0, n_super, body, (init, prepare(0)))


def _gdn_scan(q, k, v, z, bg, t_inv, norm_w, batch, seq):
    hd = GDN_HEAD_DIM
    nh = GDN_SCAN_HEADS
    kspec = pl.BlockSpec((seq, nh // GDN_REP * hd), lambda b, g: (b, g))
    vspec = pl.BlockSpec((seq, nh * hd), lambda b, g: (b, g))
    return pl.pallas_call(
        _gdn_scan_kernel,
        grid=(batch, GDN_V_HEADS // nh),
        in_specs=[kspec, kspec, vspec, vspec, pl.BlockSpec((seq, LANES), lambda b, g: (b, 0)),
                  pl.BlockSpec((1, nh, seq, GDN_CHUNK), lambda b, g: (b, g, 0, 0)),
                  pl.BlockSpec((1, hd), lambda b, g: (0, 0))],
        out_specs=vspec,
        out_shape=jax.ShapeDtypeStruct((batch * seq, GDN_V_HEADS * hd), BF16),
        compiler_params=_cparams("arbitrary", "arbitrary"),
        name="gdn_scan",
    )(q, k, v, z, bg, t_inv, norm_w.reshape(1, hd))


def _gdn_mixer(x, mod, norm_w, w_in, conv_w, a_log, dt_bias, gnorm_w, batch, seq):
    q, k, v, z, bg = _gdn_proj(x, mod, norm_w, w_in, conv_w, a_log, dt_bias, seq)
    c = GDN_CHUNK
    l_c = _gdn_l(k, bg, batch, seq)
    n_chunks = batch * GDN_V_HEADS * (seq // c)
    a = l_c.reshape(n_chunks, c, c).transpose(1, 2, 0)
    t_inv = _tri_inv(a).transpose(2, 0, 1).reshape(batch, GDN_V_HEADS, seq, c)
    return _gdn_scan(q, k, v, z, bg, t_inv, gnorm_w, batch, seq)


def kernel(x, c, positions, norm_mix, norm_ffn, ada_w, ada_b, mla_w_in, mla_q_norm, mla_w_uq,
           mla_kv_norm, mla_w_ukv, mla_w_o, gdn_w_in, gdn_conv_w, gdn_a_log, gdn_dt_bias, gdn_norm_w,
           gdn_w_o, router_w, router_b, moe_w_gate_up, moe_b_gate_up, moe_w_down, moe_b_down, final_norm):
    batch, seq, d = x.shape
    depth = ada_w.shape[0]
    xt = x.reshape(batch * seq, d)
    mod = _adaln_mod(c, ada_w, ada_b)
    cos, sin = _rope_tables(positions)
    for layer in range(depth):
        j = layer // 2
        if layer % 2 == 0:
            weights = _mla_weights(mla_w_in[j], mla_w_uq[j], mla_w_ukv[j])
            q, k, v = _mla_proj(xt, mod[layer], norm_mix[layer], weights, mla_q_norm[j],
                                mla_kv_norm[j], cos, sin, seq)
            o = _attention(q, k, v, batch, seq)
            w_o = mla_w_o[j]
        else:
            o = _gdn_mixer(xt, mod[layer], norm_mix[layer], gdn_w_in[j], gdn_conv_w[j], gdn_a_log[j],
                           gdn_dt_bias[j], gdn_norm_w[j], batch, seq)
            w_o = gdn_w_o[j]
        xt = _moe_block(o, xt, mod[layer], w_o, norm_ffn[layer], router_w[layer], router_b[layer],
                        moe_w_gate_up, moe_b_gate_up, moe_w_down, moe_b_down, final_norm, seq, layer,
                        layer == depth - 1)
    return xt.reshape(batch, seq, d)
```

```python
import functools
import math

import jax
import jax.numpy as jnp
from jax import lax
from jax.experimental import pallas as pl
from jax.experimental.pallas import tpu as pltpu

F32, BF16, I32, U32 = jnp.float32, jnp.bfloat16, jnp.int32, jnp.uint32

NORM_EPS = 1e-6
N_MOD = 6
MLA_HEADS = 8
QK_NOPE_DIM = 128
QK_ROPE_DIM = 64
V_HEAD_DIM = 128
Q_LORA_RANK = 384
KV_LORA_RANK = 256
ROPE_THETA = 10000.0
GDN_K_HEADS = 8
GDN_V_HEADS = 16
GDN_HEAD_DIM = 128
GDN_CONV = 4
GDN_CHUNK = 64
N_EXPERTS = 32
TOP_K = 4
SWIGLU_LIMIT = 7.0
SWIGLU_ALPHA = 1.702

LANES = 128
SUBLANES = 8
VMEM_LIMIT = 56 * 1024 * 1024

TOKEN_TILE = 256
ROUTE_TILE = 512
ATTN_TILE = 256
ATTN_HEADS = 8
MOE_TILE = 512
GDN_SUPER = 4 * GDN_CHUNK
GDN_REP = GDN_V_HEADS // GDN_K_HEADS
GDN_SCAN_HEADS = 4
GDN_L_KHEADS = 8
INV_LANES = 512
NEG_BIG = -1e30
SEG_PIECES = tuple(1 << b for b in range(ROUTE_TILE.bit_length() - 1, SUBLANES.bit_length() - 2, -1))
SEG_ROWS = ROUTE_TILE * TOP_K + N_EXPERTS * SUBLANES
GAP_PIECES = tuple(1 << b for b in range(MOE_TILE.bit_length() - 2, SUBLANES.bit_length() - 2, -1))


def _cparams(*sem):
    return pltpu.CompilerParams(dimension_semantics=sem, vmem_limit_bytes=VMEM_LIMIT)


def _sigmoid(x):
    return 1.0 / (1.0 + jnp.exp(-x))


def _rms(x, w):
    return x * lax.rsqrt(jnp.mean(x * x, axis=-1, keepdims=True) + NORM_EPS) * w


def _norm_mod(x, w, shift, scale):
    return _rms(x, w) * (1.0 + scale) + shift


def _dot(a, b):
    return jnp.dot(a, b, preferred_element_type=F32)


def _dot_nt(a, b):
    return lax.dot_general(a, b, (((1,), (1,)), ((), ())), preferred_element_type=F32)


def _dot_f32(a, b):
    return jnp.dot(a, b, preferred_element_type=F32, precision=lax.Precision.HIGHEST)


def _pack_pair(lo, hi):
    ulo = lax.bitcast_convert_type(lo.astype(BF16).astype(F32), U32) >> 16
    uhi = lax.bitcast_convert_type(hi.astype(BF16).astype(F32), U32) & jnp.uint32(0xFFFF0000)
    return ulo | uhi


def _unpack_pair(p):
    lo = lax.bitcast_convert_type(p << 16, F32)
    hi = lax.bitcast_convert_type(p & jnp.uint32(0xFFFF0000), F32)
    return lo, hi


def _mod_kernel(c_ref, w_ref, b_ref, o_ref):
    c = c_ref[...]
    cond = c * _sigmoid(c)
    o_ref[0] = _dot(cond.astype(BF16), w_ref[0].astype(BF16)) + b_ref[0]


def _adaln_mod(c, ada_w, ada_b):
    depth, d, n = ada_w.shape
    b = c.shape[0]
    tn = 1024
    out = pl.pallas_call(
        _mod_kernel,
        grid=(depth, n // tn),
        in_specs=[pl.BlockSpec((b, d), lambda l, j: (0, 0)),
                  pl.BlockSpec((1, d, tn), lambda l, j: (l, 0, j)),
                  pl.BlockSpec((1, 1, tn), lambda l, j: (l, 0, j))],
        out_specs=pl.BlockSpec((1, b, tn), lambda l, j: (l, 0, j)),
        out_shape=jax.ShapeDtypeStruct((depth, b, n), F32),
        compiler_params=_cparams("arbitrary", "arbitrary"),
        name="adaln_mod",
    )(c, ada_w, ada_b.reshape(depth, 1, n))
    return out.reshape(depth, b, N_MOD, d)


def _rope_kernel(pos_ref, cos_ref, sin_ref):
    pos = pos_ref[...].astype(F32)
    lane = lax.broadcasted_iota(I32, (1, LANES), 1)
    j = (lane & (QK_ROPE_DIM // 2 - 1)).astype(F32)
    inv = jnp.exp(j * (-2.0 / QK_ROPE_DIM * math.log(ROPE_THETA)))
    ang = pos * inv
    cos_ref[...] = jnp.cos(ang)
    sin_ref[...] = jnp.sin(ang)


def _rope_tables(positions):
    t = positions.size
    tm = 1024
    spec = pl.BlockSpec((tm, LANES), lambda i: (i, 0))
    return pl.pallas_call(
        _rope_kernel,
        grid=(t // tm,),
        in_specs=[pl.BlockSpec((tm, 1), lambda i: (i, 0))],
        out_specs=[spec, spec],
        out_shape=[jax.ShapeDtypeStruct((t, LANES), F32)] * 2,
        compiler_params=_cparams("arbitrary"),
        name="rope_tables",
    )(positions.reshape(t, 1))


def _mla_proj_kernel(x_ref, mod_ref, nw_ref, win_ref, qn_ref, wqa_ref, wqb_ref, kvn_ref, wkv_ref,
                     cos_ref, sin_ref, q_ref, k_ref, v_ref):
    mod = mod_ref[0]
    h = _norm_mod(x_ref[...], nw_ref[...], mod[0:1], mod[1:2])
    lat = _dot(h.astype(BF16), win_ref[...])
    q_lat = lat[:, :Q_LORA_RANK]
    kv_lat = lat[:, Q_LORA_RANK:Q_LORA_RANK + KV_LORA_RANK]
    kp = lat[:, Q_LORA_RANK + KV_LORA_RANK:]
    cos = cos_ref[...]
    sin = sin_ref[...]
    scale = (QK_NOPE_DIM + QK_ROPE_DIM) ** -0.5

    qn = _rms(q_lat, qn_ref[...]).astype(BF16)
    qa = _dot(qn, wqa_ref[...])
    qb = _dot(qn, wqb_ref[...])
    for hh in range(MLA_HEADS):
        o = hh * 2 * LANES
        q_ref[:, o:o + LANES] = (qa[:, o:o + LANES] * scale).astype(BF16)
        pe = qa[:, o + LANES:o + 2 * LANES] * cos + qb[:, hh * LANES:(hh + 1) * LANES] * sin
        q_ref[:, o + LANES:o + 2 * LANES] = (pe * scale).astype(BF16)

    kvn = _rms(kv_lat, kvn_ref[...]).astype(BF16)
    kv = _dot(kvn, wkv_ref[...])
    lane = lax.broadcasted_iota(I32, (1, LANES), 1)
    first = lane < QK_ROPE_DIM
    u = kp * jnp.where(first, cos, sin)
    kr = jnp.where(first, u + pltpu.roll(u, QK_ROPE_DIM, 1), 0.0).astype(BF16)
    nk = MLA_HEADS * QK_NOPE_DIM
    for hh in range(MLA_HEADS):
        o = hh * 2 * LANES
        k_ref[:, o:o + LANES] = kv[:, hh * LANES:(hh + 1) * LANES].astype(BF16)
        k_ref[:, o + LANES:o + 2 * LANES] = kr
    v_ref[...] = kv[:, nk:].T.astype(BF16)


def _rotate_half_cols(w):
    half = w.shape[-1] // 2
    return jnp.concatenate([-w[..., half:], w[..., :half]], axis=-1)


def _mla_weights(w_in, w_uq, w_ukv):
    kpe = w_in[:, Q_LORA_RANK + KV_LORA_RANK:]
    w_in_ext = jnp.concatenate([w_in, _rotate_half_cols(kpe)], axis=1).astype(BF16)
    wq = w_uq.reshape(Q_LORA_RANK, MLA_HEADS, QK_NOPE_DIM + QK_ROPE_DIM)
    zeros = jnp.zeros((Q_LORA_RANK, MLA_HEADS, LANES - QK_ROPE_DIM), w_uq.dtype)
    wq_pe = wq[:, :, QK_NOPE_DIM:]
    wqa = jnp.concatenate([wq[:, :, :QK_NOPE_DIM], wq_pe, zeros], axis=2)
    wqb = jnp.concatenate([_rotate_half_cols(wq_pe), zeros], axis=2)
    wkv = w_ukv.reshape(KV_LORA_RANK, MLA_HEADS, QK_NOPE_DIM + V_HEAD_DIM)
    wkv = jnp.concatenate([wkv[:, :, :QK_NOPE_DIM].reshape(KV_LORA_RANK, -1),
                           wkv[:, :, QK_NOPE_DIM:].reshape(KV_LORA_RANK, -1)], axis=1)
    return (w_in_ext, wqa.reshape(Q_LORA_RANK, -1).astype(BF16),
            wqb.reshape(Q_LORA_RANK, -1).astype(BF16), wkv.astype(BF16))


def _mla_proj(x, mod, norm_w, weights, q_norm, kv_norm, cos, sin, seq):
    t, d = x.shape
    tm = TOKEN_TILE
    w_in_ext, wqa, wqb, wkv = weights
    per_seq = seq // tm
    full = lambda a: pl.BlockSpec(a.shape, lambda i: (0,) * a.ndim)
    row = lambda n: pl.BlockSpec((tm, n), lambda i: (i, 0))
    nq = MLA_HEADS * 2 * LANES
    nv = MLA_HEADS * V_HEAD_DIM
    args = (x, mod, norm_w.reshape(1, d), w_in_ext, q_norm.reshape(1, -1), wqa, wqb,
            kv_norm.reshape(1, -1), wkv, cos, sin)
    in_specs = [row(d), pl.BlockSpec((1, N_MOD, d), lambda i: (i // per_seq, 0, 0)), full(args[2]),
                full(w_in_ext), full(args[4]), full(wqa), full(wqb), full(args[7]), full(wkv),
                row(LANES), row(LANES)]
    return pl.pallas_call(
        _mla_proj_kernel,
        grid=(t // tm,),
        in_specs=in_specs,
        out_specs=[row(nq), row(nq), pl.BlockSpec((nv, tm), lambda i: (0, i))],
        out_shape=[jax.ShapeDtypeStruct((t, nq), BF16), jax.ShapeDtypeStruct((t, nq), BF16),
                   jax.ShapeDtypeStruct((nv, t), BF16)],
        compiler_params=_cparams("arbitrary"),
        name="mla_proj",
    )(*args)


def _attn_kernel(q_ref, k_ref, vt_ref, o_ref):
    tq = q_ref.shape[0]
    dq = 2 * LANES
    dv = V_HEAD_DIM
    qi = pl.program_id(2)
    heads = range(ATTN_HEADS)
    q = [q_ref[:, h * dq:(h + 1) * dq] for h in heads]

    def step(r0, carry, mask):
        s = [_dot_nt(k_ref[pl.ds(r0, tq), h * dq:(h + 1) * dq], q[h]) for h in heads]
        if mask is not None:
            s = [jnp.where(mask, sh, NEG_BIG) for sh in s]
        m_new = [jnp.maximum(carry[h][0], jnp.max(s[h], axis=0, keepdims=True)) for h in heads]
        p = [jnp.exp(s[h] - m_new[h]) for h in heads]
        alpha = [jnp.exp(carry[h][0] - m_new[h]) for h in heads]
        l = [alpha[h] * carry[h][1] + jnp.sum(p[h], axis=0, keepdims=True) for h in heads]
        acc = [alpha[h] * carry[h][2] + _dot(vt_ref[h * dv:(h + 1) * dv, pl.ds(r0, tq)], p[h].astype(BF16))
               for h in heads]
        return tuple((m_new[h], l[h], acc[h]) for h in heads)

    init = tuple((jnp.full((1, tq), NEG_BIG, F32), jnp.zeros((1, tq), F32), jnp.zeros((dv, tq), F32))
                 for _ in heads)
    carry = lax.fori_loop(0, qi, lambda j, c: step(pl.multiple_of(j * tq, tq), c, None), init)
    causal = (lax.broadcasted_iota(I32, (tq, tq), 0) <= lax.broadcasted_iota(I32, (tq, tq), 1))
    carry = step(pl.multiple_of(qi * tq, tq), carry, causal)
    for h in heads:
        _, l, acc = carry[h]
        o_ref[:, h * dv:(h + 1) * dv] = (acc / l).T.astype(o_ref.dtype)


def _attention(q, k, v, batch, seq):
    t = q.shape[0]
    tq = ATTN_TILE
    nq = seq // tq
    dq = ATTN_HEADS * 2 * LANES
    dv = ATTN_HEADS * V_HEAD_DIM
    return pl.pallas_call(
        _attn_kernel,
        grid=(batch, MLA_HEADS // ATTN_HEADS, nq),
        in_specs=[pl.BlockSpec((tq, dq), lambda b, h, i: (b * nq + i, h)),
                  pl.BlockSpec((seq, dq), lambda b, h, i: (b, h)),
                  pl.BlockSpec((dv, seq), lambda b, h, i: (h, b))],
        out_specs=pl.BlockSpec((tq, dv), lambda b, h, i: (b * nq + i, h)),
        out_shape=jax.ShapeDtypeStruct((t, MLA_HEADS * V_HEAD_DIM), BF16),
        compiler_params=_cparams("arbitrary", "arbitrary", "arbitrary"),
        name="mla_attention",
    )(q, k, v)


def _post_mixer_kernel(o_ref, x_ref, mod_ref, wo_ref, nw_ref, rw_ref, rb_ref,
                       x1_ref, h_ref, route_ref, tcnt_ref, trun_ref, cnt_ref, run_ref):
    tm, d = x_ref.shape
    i = pl.program_id(0)

    @pl.when(i == 0)
    def _():
        run_ref[...] = jnp.zeros_like(run_ref)

    mod = mod_ref[0]
    x1 = x_ref[...] + mod[2:3] * _dot(o_ref[...], wo_ref[...])
    x1_ref[...] = x1
    h = _norm_mod(x1, nw_ref[...], mod[3:4], mod[4:5])
    h_ref[...] = h.astype(BF16)

    h_hi = h.astype(BF16)
    h_lo = (h - h_hi.astype(F32)).astype(BF16)
    parts = _dot(h_hi, rw_ref[...]) + _dot(h_lo, rw_ref[...])
    lane = lax.broadcasted_iota(I32, (tm, LANES), 1)
    logits = jnp.where(lane < N_EXPERTS, parts + pltpu.roll(parts, LANES - N_EXPERTS, 1) + rb_ref[...], NEG_BIG)
    lane_f = lane.astype(F32)
    work = logits
    val, hot = [], []
    for _ in range(TOP_K):
        m = jnp.max(work, axis=-1, keepdims=True)
        a = jnp.min(jnp.where(work == m, lane_f, float(LANES)), axis=-1, keepdims=True)
        hot.append(lane_f == a)
        val.append(m)
        work = jnp.where(hot[-1], -jnp.inf, work)
    ex = [jnp.exp(v - val[0]) for v in val]
    den = ex[0] + ex[1] + ex[2] + ex[3]

    onehot = jnp.where(hot[0] | hot[1] | hot[2] | hot[3], 1.0, 0.0)
    row = lax.broadcasted_iota(I32, (tm, tm), 0)
    col = lax.broadcasted_iota(I32, (tm, tm), 1)
    tri = jnp.where(col < row, 1.0, 0.0).astype(BF16)
    rank = _dot(tri, onehot.astype(BF16))
    cnt = jnp.sum(onehot, axis=0, keepdims=True)
    cnt = jnp.floor((cnt + (SUBLANES - 1)) * (1.0 / SUBLANES)) * SUBLANES
    e_row = lax.broadcasted_iota(I32, (LANES, LANES), 0)
    e_col = lax.broadcasted_iota(I32, (LANES, LANES), 1)
    before = jnp.where(e_row < e_col, 1.0, 0.0)
    off = _dot_f32(jnp.broadcast_to(cnt, (SUBLANES, LANES)), before)[0:1]
    local = off + rank

    route = jnp.zeros((tm, LANES), F32)
    for kk in range(TOP_K):
        pos = jnp.sum(jnp.where(hot[kk], local, 0.0), axis=-1, keepdims=True)
        route = jnp.where(lane == kk, pos, route)
        route = jnp.where(lane == TOP_K + kk, ex[kk] / den, route)
    route_ref[...] = route
    tcnt_ref[0] = cnt.astype(I32)
    trun_ref[0] = run_ref[...].astype(I32)
    run = run_ref[...] + cnt
    run_ref[...] = run
    cnt_ref[...] = run.astype(I32)


def _post_mixer(o, x, mod, w_o, norm_w, router_w, router_b, seq):
    t, d = x.shape
    ko = o.shape[1]
    tm = ROUTE_TILE
    per_seq = seq // tm
    n_tiles = t // tm
    rw_hi = router_w.astype(BF16)
    rw_lo = (router_w - rw_hi.astype(F32)).astype(BF16)
    rw = jnp.zeros((d, LANES), BF16).at[:, :N_EXPERTS].set(rw_hi).at[:, N_EXPERTS:2 * N_EXPERTS].set(rw_lo)
    rb = jnp.zeros((1, LANES), F32).at[0, :N_EXPERTS].set(router_b)
    row = lambda n: pl.BlockSpec((tm, n), lambda i: (i, 0))
    const = lambda r, c: pl.BlockSpec((r, c), lambda i: (0, 0))
    per_tile = pl.BlockSpec((1, 1, LANES), lambda i: (i, 0, 0))
    return pl.pallas_call(
        _post_mixer_kernel,
        grid=(n_tiles,),
        in_specs=[row(ko), row(d), pl.BlockSpec((1, N_MOD, d), lambda i: (i // per_seq, 0, 0)),
                  const(ko, d), const(1, d), const(d, LANES), const(1, LANES)],
        out_specs=[row(d), row(d), row(LANES), per_tile, per_tile, const(1, LANES)],
        out_shape=[jax.ShapeDtypeStruct((t, d), F32), jax.ShapeDtypeStruct((t, d), BF16),
                   jax.ShapeDtypeStruct((t, LANES), F32),
                   jax.ShapeDtypeStruct((n_tiles, 1, LANES), I32),
                   jax.ShapeDtypeStruct((n_tiles, 1, LANES), I32),
                   jax.ShapeDtypeStruct((1, LANES), I32)],
        scratch_shapes=[pltpu.VMEM((1, LANES), F32)],
        compiler_params=_cparams("arbitrary"),
        name="post_mixer_router",
    )(o, x, mod, w_o.astype(BF16), norm_w.reshape(1, d), rw, rb)


def _tile_ceil(n):
    return (n + (MOE_TILE - 1)) // MOE_TILE * MOE_TILE


def _expert_starts(counts_ref, starts_ref):
    def body(e, acc):
        starts_ref[e] = acc
        return acc + _tile_ceil(counts_ref[e])
    return lax.fori_loop(0, N_EXPERTS, body, jnp.int32(0))


def _segment_copies(tile, tcnt_ref, trun_ref, starts_ref, visit):
    def per_expert(e, off):
        n = tcnt_ref[tile, e]
        base = starts_ref[e] + trun_ref[tile, e]
        for piece in SEG_PIECES:
            done = n & (-2 * piece)

            @pl.when((n & piece) != 0)
            def _():
                visit(pl.multiple_of(off + done, SUBLANES), pl.multiple_of(base + done, SUBLANES), piece)
        return off + n
    lax.fori_loop(0, N_EXPERTS, per_expert, jnp.int32(0))


def _dispatch_kernel(tcnt_ref, trun_ref, counts_ref, route_ref, h_ref, xs_ref, g_ref, zero_ref, starts_ref, sem):
    i = pl.program_id(0)
    n = pl.num_programs(0)
    tm, d = h_ref.shape
    slot = lax.rem(i, 2)

    def copies(tile, s, act):
        def visit(src, dst, nrows):
            act(pltpu.make_async_copy(g_ref.at[s, pl.ds(src, nrows)], xs_ref.at[pl.ds(dst, nrows)], sem.at[s]))
        _segment_copies(tile, tcnt_ref, trun_ref, starts_ref, visit)

    start = lambda c: c.start()
    wait = lambda c: c.wait()

    @pl.when(i == 0)
    def _():
        _expert_starts(counts_ref, starts_ref)

    @pl.when(i >= 2)
    def _():
        copies(i - 2, slot, wait)

    rt = route_ref[...].T
    r = lax.broadcasted_iota(I32, (SEG_ROWS, tm), 0).astype(F32)
    hit = (r == rt[0:1]) | (r == rt[1:2]) | (r == rt[2:3]) | (r == rt[3:4])
    g = _dot(jnp.where(hit, 1.0, 0.0).astype(BF16), h_ref[...])
    g_ref[slot] = _pack_pair(g[:, :d // 2], g[:, d // 2:])
    copies(i, slot, start)

    @pl.when(i == n - 1)
    def _():
        copies(i, slot, wait)

        @pl.when(i >= 1)
        def _():
            copies(i - 1, 1 - slot, wait)
        zero_ref[...] = jnp.zeros_like(zero_ref)

        def gap_fill(act):
            def per_expert(e, c):
                cnt = counts_ref[e]
                gap = _tile_ceil(cnt) - cnt
                base = starts_ref[e] + cnt
                for piece in GAP_PIECES:
                    done = gap & (-2 * piece)

                    @pl.when((gap & piece) != 0)
                    def _():
                        act(pltpu.make_async_copy(
                            zero_ref.at[pl.ds(0, piece)],
                            xs_ref.at[pl.ds(pl.multiple_of(base + done, SUBLANES), piece)], sem.at[slot]))
                return c
            lax.fori_loop(0, N_EXPERTS, per_expert, 0)

        used = starts_ref[N_EXPERTS - 1] + _tile_ceil(counts_ref[N_EXPERTS - 1])
        n_unused = (xs_ref.shape[0] - used) // MOE_TILE

        def tile_fill(act):
            def per_tile(kk, c):
                row = pl.multiple_of(used + kk * MOE_TILE, MOE_TILE)
                act(pltpu.make_async_copy(zero_ref, xs_ref.at[pl.ds(row, MOE_TILE)], sem.at[slot]))
                return c
            lax.fori_loop(0, n_unused, per_tile, 0)

        gap_fill(start)
        tile_fill(start)
        gap_fill(wait)
        tile_fill(wait)


def _sorted_rows(t):
    bound = t * TOP_K + (t // ROUTE_TILE) * N_EXPERTS * (SUBLANES - 1) + N_EXPERTS * (MOE_TILE - SUBLANES)
    return (bound + MOE_TILE - 1) // MOE_TILE * MOE_TILE


def _dispatch(h, route, tcnt, trun, counts):
    t, d = h.shape
    tm = ROUTE_TILE
    smem = pl.BlockSpec(memory_space=pltpu.SMEM)
    return pl.pallas_call(
        _dispatch_kernel,
        grid=(t // tm,),
        in_specs=[smem, smem, smem, pl.BlockSpec((tm, LANES), lambda i: (i, 0)),
                  pl.BlockSpec((tm, d), lambda i: (i, 0))],
        out_specs=pl.BlockSpec(memory_space=pl.ANY),
        out_shape=jax.ShapeDtypeStruct((_sorted_rows(t), d // 2), U32),
        scratch_shapes=[pltpu.VMEM((2, SEG_ROWS, d // 2), U32), pltpu.VMEM((MOE_TILE, d // 2), U32),
                        pltpu.SMEM((N_EXPERTS,), I32), pltpu.SemaphoreType.DMA((2,))],
        compiler_params=_cparams("arbitrary"),
        name="moe_dispatch",
    )(tcnt, trun, counts, route, h)


def _moe_schedule(counts, n_tiles, tile):
    ids = jnp.arange(N_EXPERTS, dtype=I32)
    n_own = (counts + tile - 1) // tile
    t_end = jnp.cumsum(n_own)
    t_start = t_end - n_own
    total = t_end[-1]
    later = jnp.where((ids[None, :] > ids[:, None]) & (counts[None, :] > 0), ids[None, :], N_EXPERTS)
    next_e = jnp.min(later, axis=1)
    next_e = jnp.where(next_e == N_EXPERTS, -1, next_e)
    step = jnp.arange(n_tiles, dtype=I32)
    live = step < total
    v = jnp.minimum(step, total - 1)
    e = jnp.minimum(jnp.sum(t_end[None, :] <= v[:, None], axis=1), N_EXPERTS - 1).astype(I32)
    pick = e[:, None] == ids[None, :]
    of_e = lambda a: jnp.sum(jnp.where(pick, a[None, :], 0), axis=1)
    new_expert = (live & (step == of_e(t_start))).astype(I32)
    return e, new_expert, live.astype(I32), of_e(next_e).astype(I32)


def _moe_kernel(exp_ref, newe_ref, live_ref, next_ref,
                x_ref, wgu_hbm, bgu_ref, wd_hbm, bd_ref, y_ref,
                wgu_f32, wd_f32, wgu_bf, wd_bf, sem, *, layer):
    v = pl.program_id(0)
    f = wd_bf.shape[0]
    half = x_ref.shape[1]

    def weight_copies(e):
        return (pltpu.make_async_copy(wgu_hbm.at[layer, e], wgu_f32, sem.at[0]),
                pltpu.make_async_copy(wd_hbm.at[layer, e], wd_f32, sem.at[1]))

    @pl.when(newe_ref[v] == 1)
    def _():
        e = exp_ref[v]

        @pl.when(v == 0)
        def _():
            for c in weight_copies(e):
                c.start()
        for c in weight_copies(e):
            c.wait()
        wgu_bf[...] = wgu_f32[...].astype(BF16)
        wd_bf[...] = wd_f32[...].astype(BF16)
        nxt = next_ref[v]

        @pl.when(nxt >= 0)
        def _():
            for c in weight_copies(nxt):
                c.start()

    @pl.when(live_ref[v] == 1)
    def _():
        x_lo, x_hi = _unpack_pair(x_ref[...])
        x = jnp.concatenate([x_lo.astype(BF16), x_hi.astype(BF16)], axis=1)
        gu = _dot(x, wgu_bf[...]) + bgu_ref[0]
        gate = jnp.minimum(gu[:, :f], SWIGLU_LIMIT)
        lin = jnp.clip(gu[:, f:], -SWIGLU_LIMIT, SWIGLU_LIMIT)
        act = gate * _sigmoid(SWIGLU_ALPHA * gate) * (lin + 1.0)
        y = _dot(act.astype(BF16), wd_bf[...]) + bd_ref[0]
        y_ref[...] = _pack_pair(y[:, :half], y[:, half:])

    @pl.when(live_ref[v] == 0)
    def _():
        y_ref[...] = jnp.zeros_like(y_ref)


def _moe_experts(xs, sched, w_gate_up, b_gate_up, w_down, b_down, layer):
    a, half = xs.shape
    depth, e, d, f2 = w_gate_up.shape
    f = f2 // 2
    tile = MOE_TILE
    grid_spec = pltpu.PrefetchScalarGridSpec(
        num_scalar_prefetch=4,
        grid=(a // tile,),
        in_specs=[pl.BlockSpec((tile, half), lambda v, ex, *_: (v, 0)),
                  pl.BlockSpec(memory_space=pl.ANY),
                  pl.BlockSpec((None, 1, 1, f2), lambda v, ex, *_: (layer, ex[v], 0, 0)),
                  pl.BlockSpec(memory_space=pl.ANY),
                  pl.BlockSpec((None, 1, 1, d), lambda v, ex, *_: (layer, ex[v], 0, 0))],
        out_specs=pl.BlockSpec((tile, half), lambda v, ex, *_: (v, 0)),
        scratch_shapes=[pltpu.VMEM((d, f2), F32), pltpu.VMEM((f, d), F32),
                        pltpu.VMEM((d, f2), BF16), pltpu.VMEM((f, d), BF16),
                        pltpu.SemaphoreType.DMA((2,))],
    )
    return pl.pallas_call(
        functools.partial(_moe_kernel, layer=layer),
        grid_spec=grid_spec,
        out_shape=jax.ShapeDtypeStruct((a, half), U32),
        compiler_params=_cparams("arbitrary"),
        name="moe_experts",
    )(*sched, xs, w_gate_up, b_gate_up.reshape(depth, e, 1, f2), w_down, b_down.reshape(depth, e, 1, d))


def _combine_kernel(tcnt_ref, trun_ref, counts_ref, ys_ref, route_ref, x_ref, mod_ref, fw_ref,
                    o_ref, y_buf, starts_ref, sem, *, final_norm):
    i = pl.program_id(0)
    n = pl.num_programs(0)
    tm, d = x_ref.shape
    slot = lax.rem(i, 2)

    def copies(tile, s, act):
        def visit(dst, src, nrows):
            act(pltpu.make_async_copy(ys_ref.at[pl.ds(src, nrows)], y_buf.at[s, pl.ds(dst, nrows)], sem.at[s]))
        _segment_copies(tile, tcnt_ref, trun_ref, starts_ref, visit)

    @pl.when(i == 0)
    def _():
        _expert_starts(counts_ref, starts_ref)
        y_buf[...] = jnp.zeros_like(y_buf)
        copies(0, 0, lambda c: c.start())

    @pl.when(i + 1 < n)
    def _():
        copies(i + 1, 1 - slot, lambda c: c.start())

    copies(i, slot, lambda c: c.wait())
    y_lo, y_hi = _unpack_pair(y_buf[slot])
    route = route_ref[...]
    r = lax.broadcasted_iota(I32, (tm, SEG_ROWS), 1).astype(F32)
    w = jnp.zeros((tm, SEG_ROWS), F32)
    for kk in range(TOP_K):
        w = jnp.where(r == route[:, kk:kk + 1], route[:, TOP_K + kk:TOP_K + kk + 1], w)
    w = w.astype(BF16)
    y = jnp.concatenate([_dot(w, y_lo.astype(BF16)), _dot(w, y_hi.astype(BF16))], axis=1)
    out = x_ref[...] + mod_ref[0][5:6] * y
    if final_norm:
        out = _rms(out, fw_ref[...])
    o_ref[...] = out


def _combine(ys, route, tcnt, trun, counts, x1, mod, final_w, seq, final_norm):
    t, d = x1.shape
    tm = ROUTE_TILE
    per_seq = seq // tm
    smem = pl.BlockSpec(memory_space=pltpu.SMEM)
    row = lambda n: pl.BlockSpec((tm, n), lambda i: (i, 0))
    return pl.pallas_call(
        functools.partial(_combine_kernel, final_norm=final_norm),
        grid=(t // tm,),
        in_specs=[smem, smem, smem, pl.BlockSpec(memory_space=pl.ANY), row(LANES), row(d),
                  pl.BlockSpec((1, N_MOD, d), lambda i: (i // per_seq, 0, 0)),
                  pl.BlockSpec((1, d), lambda i: (0, 0))],
        out_specs=row(d),
        out_shape=jax.ShapeDtypeStruct((t, d), F32),
        scratch_shapes=[pltpu.VMEM((2, SEG_ROWS, d // 2), U32), pltpu.SMEM((N_EXPERTS,), I32),
                        pltpu.SemaphoreType.DMA((2,))],
        compiler_params=_cparams("arbitrary"),
        name="moe_combine",
    )(tcnt, trun, counts, ys, route, x1, mod, final_w.reshape(1, d))


def _moe_block(o, x, mod, w_o, norm_w, router_w, router_b, w_gate_up, b_gate_up, w_down, b_down,
               final_w, seq, layer, final_norm):
    x1, h, route, tcnt, trun, cnt = _post_mixer(o, x, mod, w_o, norm_w, router_w, router_b, seq)
    n_tiles = tcnt.shape[0]
    tcnt = tcnt.reshape(n_tiles, LANES)
    trun = trun.reshape(n_tiles, LANES)
    counts = cnt[0, :N_EXPERTS]
    xs = _dispatch(h, route, tcnt, trun, counts)
    sched = _moe_schedule(counts, xs.shape[0] // MOE_TILE, MOE_TILE)
    ys = _moe_experts(xs, sched, w_gate_up, b_gate_up, w_down, b_down, layer)
    return _combine(ys, route, tcnt, trun, counts, x1, mod, final_w, seq, final_norm)


def _gdn_conv_kernel(x_ref, mod_ref, nw_ref, win_ref, cw_ref, q_ref, k_ref, v_ref, buf_ref, *, per_seq):
    tm = x_ref.shape[0]
    nqk = q_ref.shape[1]
    nconv = win_ref.shape[1]
    i = pl.program_id(0)
    mod = mod_ref[0]
    h = _norm_mod(x_ref[...], nw_ref[...], mod[0:1], mod[1:2])

    @pl.when(i % per_seq == 0)
    def _():
        buf_ref[0:SUBLANES, :] = jnp.zeros((SUBLANES, nconv), F32)

    buf_ref[SUBLANES:SUBLANES + tm, :] = _dot(h.astype(BF16), win_ref[...])
    cw = cw_ref[...]
    window = buf_ref[0:SUBLANES + tm, :]
    acc = cw[GDN_CONV - 1:GDN_CONV, :] * window[SUBLANES:, :]
    for back in range(1, GDN_CONV):
        acc = acc + cw[GDN_CONV - 1 - back:GDN_CONV - back, :] * pltpu.roll(window, back, 0)[SUBLANES:, :]
    buf_ref[0:SUBLANES, :] = buf_ref[tm:tm + SUBLANES, :]
    y = acc * _sigmoid(acc)

    def l2(a):
        return a * lax.rsqrt(jnp.sum(a * a, axis=-1, keepdims=True) + 1e-6)

    for hh in range(nqk // GDN_HEAD_DIM):
        s = slice(hh * GDN_HEAD_DIM, (hh + 1) * GDN_HEAD_DIM)
        q_ref[:, s] = (l2(y[:, s]) * GDN_HEAD_DIM ** -0.5).astype(BF16)
        k_ref[:, s] = l2(y[:, nqk + hh * GDN_HEAD_DIM:nqk + (hh + 1) * GDN_HEAD_DIM]).astype(BF16)
    v_ref[...] = y[:, 2 * nqk:].astype(BF16)


def _gdn_gate_kernel(x_ref, mod_ref, nw_ref, win_ref, alog_ref, dtb_ref, z_ref, bg_ref):
    nv = z_ref.shape[1]
    mod = mod_ref[0]
    h = _norm_mod(x_ref[...], nw_ref[...], mod[0:1], mod[1:2])
    pr = _dot(h.astype(BF16), win_ref[...])
    z_ref[...] = pr[:, :nv].astype(BF16)
    ba = pr[:, nv:]
    sp = ba + dtb_ref[...]
    softplus = jnp.maximum(sp, 0.0) + jnp.log(1.0 + jnp.exp(-jnp.abs(sp)))
    lane = lax.broadcasted_iota(I32, (1, LANES), 1)
    log_decay = -jnp.exp(alog_ref[...]) * softplus
    bg_ref[...] = jnp.where(lane < GDN_V_HEADS, _sigmoid(ba), _chunk_cumsum(log_decay))


def _gdn_proj(x, mod, norm_w, w_in, conv_w, a_log, dt_bias, seq):
    t, d = x.shape
    tm = TOKEN_TILE
    per_seq = seq // tm
    nqk = GDN_K_HEADS * GDN_HEAD_DIM
    nv = GDN_V_HEADS * GDN_HEAD_DIM
    nconv = 2 * nqk + nv
    w_conv = w_in[:, :nconv].astype(BF16)
    w_gate = jnp.zeros((d, nv + LANES), BF16).at[:, :w_in.shape[1] - nconv].set(w_in[:, nconv:].astype(BF16))
    pad = jnp.zeros((1, LANES), F32)
    alog = pad.at[0, GDN_V_HEADS:2 * GDN_V_HEADS].set(a_log)
    dtb = pad.at[0, GDN_V_HEADS:2 * GDN_V_HEADS].set(dt_bias)
    row = lambda n: pl.BlockSpec((tm, n), lambda i: (i, 0))
    const = lambda a: pl.BlockSpec(a.shape, lambda i: (0,) * a.ndim)
    modspec = pl.BlockSpec((1, N_MOD, d), lambda i: (i // per_seq, 0, 0))
    nw = norm_w.reshape(1, d)
    q, k, v = pl.pallas_call(
        functools.partial(_gdn_conv_kernel, per_seq=per_seq),
        grid=(t // tm,),
        in_specs=[row(d), modspec, const(nw), const(w_conv), const(conv_w)],
        out_specs=[row(nqk), row(nqk), row(nv)],
        out_shape=[jax.ShapeDtypeStruct((t, nqk), BF16), jax.ShapeDtypeStruct((t, nqk), BF16),
                   jax.ShapeDtypeStruct((t, nv), BF16)],
        scratch_shapes=[pltpu.VMEM((tm + 2 * SUBLANES, nconv), F32)],
        compiler_params=_cparams("arbitrary"),
        name="gdn_conv_proj",
    )(x, mod, nw, w_conv, conv_w)
    z, bg = pl.pallas_call(
        _gdn_gate_kernel,
        grid=(t // tm,),
        in_specs=[row(d), modspec, const(nw), const(w_gate), const(alog), const(dtb)],
        out_specs=[row(nv), row(LANES)],
        out_shape=[jax.ShapeDtypeStruct((t, nv), BF16), jax.ShapeDtypeStruct((t, LANES), F32)],
        compiler_params=_cparams("arbitrary"),
        name="gdn_gate_proj",
    )(x, mod, nw, w_gate, alog, dtb)
    return q, k, v, z, bg


def _head_col(block, lane_idx):
    lane = lax.broadcasted_iota(I32, (1, LANES), 1)
    col = jnp.sum(jnp.where(lane == lane_idx, block, 0.0), axis=-1, keepdims=True)
    return jnp.broadcast_to(col, block.shape)


def _chunk_cumsum(x):
    rin = lax.broadcasted_iota(I32, x.shape, 0) & (GDN_CHUNK - 1)
    s = 1
    while s < GDN_CHUNK:
        x = x + jnp.where(rin >= s, pltpu.roll(x, s, 0), 0.0)
        s *= 2
    return x


def _diag_blocks(m):
    c = GDN_CHUNK
    return jnp.concatenate([m[b * c:(b + 1) * c, b * c:(b + 1) * c] for b in range(m.shape[0] // c)], axis=0)


def _chunk_decay(gc):
    n = gc.shape[0]
    c = GDN_CHUNK
    gt = gc.T
    gj = jnp.concatenate([jnp.broadcast_to(gt[0:1, b * c:(b + 1) * c], (c, c)) for b in range(n // c)], axis=0)
    i_in = lax.broadcasted_iota(I32, (n, c), 0) & (c - 1)
    col = lax.broadcasted_iota(I32, (n, c), 1)
    lower = col <= i_in
    decay = jnp.where(lower, jnp.exp(jnp.where(lower, gc[:, :c] - gj, 0.0)), 0.0)
    return decay, col < i_in


def _gdn_l_kernel(k_ref, bg_ref, l_ref):
    grp = pl.program_id(1)
    sc = GDN_SUPER
    c = GDN_CHUNK
    hd = GDN_HEAD_DIM
    vheads = range(GDN_L_KHEADS * GDN_REP)

    def body(s, carry):
        r0 = pl.multiple_of(s * sc, sc)
        bg = bg_ref[pl.ds(r0, sc), :]
        kk = []
        for kh in range(GDN_L_KHEADS):
            k = k_ref[pl.ds(r0, sc), kh * hd:(kh + 1) * hd]
            kk.append(_diag_blocks(_dot_nt(k, k)))
        first = grp * GDN_L_KHEADS * GDN_REP
        beta = [_head_col(bg, first + j) for j in vheads]
        gc = [_head_col(bg, GDN_V_HEADS + first + j) for j in vheads]
        dec = [_chunk_decay(x) for x in gc]
        for j in vheads:
            decay, strict = dec[j]
            l_ref[0, j, pl.ds(r0, sc), :] = jnp.where(strict, kk[j // GDN_REP] * beta[j][:, :c] * decay, 0.0)
        return carry
    lax.fori_loop(0, k_ref.shape[0] // sc, body, 0)


def _gdn_l(k, bg, batch, seq):
    nk = GDN_L_KHEADS
    return pl.pallas_call(
        _gdn_l_kernel,
        grid=(batch, GDN_K_HEADS // nk),
        in_specs=[pl.BlockSpec((seq, nk * GDN_HEAD_DIM), lambda b, h: (b, h)),
                  pl.BlockSpec((seq, LANES), lambda b, h: (b, 0))],
        out_specs=pl.BlockSpec((1, nk * GDN_REP, seq, GDN_CHUNK), lambda b, h: (b, h, 0, 0)),
        out_shape=jax.ShapeDtypeStruct((batch, GDN_V_HEADS, seq, GDN_CHUNK), F32),
        compiler_params=_cparams("arbitrary", "arbitrary"),
        name="gdn_chunk_l",
    )(k, bg)


def _tri_inv_kernel(a_ref, x_ref):
    c, n = a_ref.shape[1], a_ref.shape[2]
    sub_row = lax.broadcasted_iota(I32, (SUBLANES, n), 0)
    for c0 in range(0, c, SUBLANES):
        cols = slice(c0, c0 + SUBLANES)
        for i in range(c):
            if i < c0:
                x_ref[i, cols, :] = jnp.zeros((SUBLANES, n), F32)
                continue

            def sub(j, acc, i=i, cols=cols):
                return acc - a_ref[i, pl.ds(j, 1), :] * x_ref[j, cols, :]
            row = jnp.where(sub_row == i - c0, 1.0, 0.0)
            if i > c0:
                row = lax.fori_loop(c0, i, sub, row, unroll=min(i - c0, 8))
            x_ref[i, cols, :] = row


def _tri_inv(a):
    c, _, n = a.shape
    lanes = min(INV_LANES, n)
    spec = pl.BlockSpec((c, c, lanes), lambda i: (0, 0, i))
    return pl.pallas_call(
        _tri_inv_kernel,
        grid=(n // lanes,),
        in_specs=[spec],
        out_specs=spec,
        out_shape=jax.ShapeDtypeStruct(a.shape, F32),
        compiler_params=_cparams("arbitrary"),
        name="gdn_tri_inv",
    )(a)


def _gdn_scan_kernel(q_ref, k_ref, v_ref, z_ref, bg_ref, ti_ref, nw_ref, o_ref):
    grp = pl.program_id(1)
    sc = GDN_SUPER
    nc = sc // GDN_CHUNK
    hd = GDN_HEAD_DIM
    n_super = q_ref.shape[0] // sc
    nw = nw_ref[...]
    r_blk = lax.broadcasted_iota(I32, (sc, sc), 0) // GDN_CHUNK
    c_blk = lax.broadcasted_iota(I32, (sc, sc), 1) // GDN_CHUNK
    same = r_blk == c_blk

    heads = range(GDN_SCAN_HEADS)

    def prepare_head(s, j, shared):
        r0 = pl.multiple_of(s * sc, sc)
        kh = j // GDN_REP
        if kh not in shared:
            qb = q_ref[pl.ds(r0, sc), kh * hd:(kh + 1) * hd]
            kb = k_ref[pl.ds(r0, sc), kh * hd:(kh + 1) * hd]
            shared[kh] = (qb.astype(F32), kb.astype(F32), _diag_blocks(_dot_nt(qb, kb)))
        q, k, qk = shared[kh]
        head = grp * GDN_SCAN_HEADS + j
        bg = bg_ref[pl.ds(r0, sc), :]
        v = v_ref[pl.ds(r0, sc), j * hd:(j + 1) * hd].astype(F32)
        beta = _head_col(bg, head)
        gc = _head_col(bg, GDN_V_HEADS + head)
        egc = jnp.exp(gc)
        decay, _ = _chunk_decay(gc)
        a_intra = (qk * decay).astype(BF16)
        ti = ti_ref[0, j, pl.ds(r0, sc), :]
        t_inv = jnp.where(same, jnp.concatenate([ti] * nc, axis=1), 0.0).astype(BF16)
        uw = _dot(t_inv, jnp.concatenate([v * beta, k * beta * egc], axis=1).astype(BF16))
        g_last = [gc[(c + 1) * GDN_CHUNK - 1:(c + 1) * GDN_CHUNK, :] for c in range(nc)]
        gl_rows = jnp.concatenate([jnp.broadcast_to(g, (GDN_CHUNK, LANES)) for g in g_last], axis=0)
        k_dec_t = (k * jnp.exp(gl_rows - gc)).T.astype(BF16)
        eg_last = jnp.concatenate([jnp.exp(g) for g in g_last], axis=0)
        return uw[:, :hd], uw[:, hd:].astype(BF16), (q * egc).astype(BF16), k_dec_t, a_intra, eg_last

    def prepare(s):
        shared = {}
        return tuple(prepare_head(s, j, shared) for j in heads)

    def body(s, carry):
        states, prepared = carry
        states = list(states)
        r0 = pl.multiple_of(s * sc, sc)
        nxt = jnp.minimum(s + 1, n_super - 1)
        shared = {}
        following = []
        outs = [[] for _ in heads]
        for c in range(nc):
            rs = slice(c * GDN_CHUNK, (c + 1) * GDN_CHUNK)
            sb = [states[j].astype(BF16) for j in heads]
            v_new = [prepared[j][0][rs] - _dot(prepared[j][1][rs], sb[j]) for j in heads]
            vb = [v.astype(BF16) for v in v_new]
            for j in heads:
                outs[j].append(_dot(prepared[j][2][rs], sb[j]) + _dot(prepared[j][4][rs], vb[j]))
            states = [states[j] * prepared[j][5][c:c + 1, :] + _dot(prepared[j][3][:, rs], vb[j]) for j in heads]
            following.append(prepare_head(nxt, c, shared))
        for j in heads:
            o = jnp.concatenate(outs[j], axis=0)
            z = z_ref[pl.ds(r0, sc), j * hd:(j + 1) * hd].astype(F32)
            o_ref[pl.ds(r0, sc), j * hd:(j + 1) * hd] = (_rms(o, nw) * (z * _sigmoid(z))).astype(o_ref.dtype)
        return tuple(states), tuple(following)

    assert GDN_SCAN_HEADS == nc
    init = tuple(jnp.zeros((hd, hd), F32) for _ in heads)
    lax.fori_loop(0, n_super, body, (init, prepare(0)))


def _gdn_scan(q, k, v, z, bg, t_inv, norm_w, batch, seq):
    hd = GDN_HEAD_DIM
    nh = GDN_SCAN_HEADS
    kspec = pl.BlockSpec((seq, nh // GDN_REP * hd), lambda b, g: (b, g))
    vspec = pl.BlockSpec((seq, nh * hd), lambda b, g: (b, g))
    return pl.pallas_call(
        _gdn_scan_kernel,
        grid=(batch, GDN_V_HEADS // nh),
        in_specs=[kspec, kspec, vspec, vspec, pl.BlockSpec((seq, LANES), lambda b, g: (b, 0)),
                  pl.BlockSpec((1, nh, seq, GDN_CHUNK), lambda b, g: (b, g, 0, 0)),
                  pl.BlockSpec((1, hd), lambda b, g: (0, 0))],
        out_specs=vspec,
        out_shape=jax.ShapeDtypeStruct((batch * seq, GDN_V_HEADS * hd), BF16),
        compiler_params=_cparams("arbitrary", "arbitrary"),
        name="gdn_scan",
    )(q, k, v, z, bg, t_inv, norm_w.reshape(1, hd))


def _gdn_mixer(x, mod, norm_w, w_in, conv_w, a_log, dt_bias, gnorm_w, batch, seq):
    q, k, v, z, bg = _gdn_proj(x, mod, norm_w, w_in, conv_w, a_log, dt_bias, seq)
    c = GDN_CHUNK
    l_c = _gdn_l(k, bg, batch, seq)
    n_chunks = batch * GDN_V_HEADS * (seq // c)
    a = l_c.reshape(n_chunks, c, c).transpose(1, 2, 0)
    t_inv = _tri_inv(a).transpose(2, 0, 1).reshape(batch, GDN_V_HEADS, seq, c)
    return _gdn_scan(q, k, v, z, bg, t_inv, gnorm_w, batch, seq)


def kernel(x, c, positions, norm_mix, norm_ffn, ada_w, ada_b, mla_w_in, mla_q_norm, mla_w_uq,
           mla_kv_norm, mla_w_ukv, mla_w_o, gdn_w_in, gdn_conv_w, gdn_a_log, gdn_dt_bias, gdn_norm_w,
           gdn_w_o, router_w, router_b, moe_w_gate_up, moe_b_gate_up, moe_w_down, moe_b_down, final_norm):
    batch, seq, d = x.shape
    depth = ada_w.shape[0]
    xt = x.reshape(batch * seq, d)
    mod = _adaln_mod(c, ada_w, ada_b)
    cos, sin = _rope_tables(positions)
    for layer in range(depth):
        j = layer // 2
        if layer % 2 == 0:
            weights = _mla_weights(mla_w_in[j], mla_w_uq[j], mla_w_ukv[j])
            q, k, v = _mla_proj(xt, mod[layer], norm_mix[layer], weights, mla_q_norm[j],
                                mla_kv_norm[j], cos, sin, seq)
            o = _attention(q, k, v, batch, seq)
            w_o = mla_w_o[j]
        else:
            o = _gdn_mixer(xt, mod[layer], norm_mix[layer], gdn_w_in[j], gdn_conv_w[j], gdn_a_log[j],
                           gdn_dt_bias[j], gdn_norm_w[j], batch, seq)
            w_o = gdn_w_o[j]
        xt = _moe_block(o, xt, mod[layer], w_o, norm_ffn[layer], router_w[layer], router_b[layer],
                        moe_w_gate_up, moe_b_gate_up, moe_w_down, moe_b_down, final_norm, seq, layer,
                        layer == depth - 1)
    return xt.reshape(batch, seq, d)
```

```python
import functools
import math

import jax
import jax.numpy as jnp
from jax import lax
from jax.experimental import pallas as pl
from jax.experimental.pallas import tpu as pltpu

F32, BF16, I32, U32 = jnp.float32, jnp.bfloat16, jnp.int32, jnp.uint32

NORM_EPS = 1e-6
N_MOD = 6
MLA_HEADS = 8
QK_NOPE_DIM = 128
QK_ROPE_DIM = 64
V_HEAD_DIM = 128
Q_LORA_RANK = 384
KV_LORA_RANK = 256
ROPE_THETA = 10000.0
GDN_K_HEADS = 8
GDN_V_HEADS = 16
GDN_HEAD_DIM = 128
GDN_CONV = 4
GDN_CHUNK = 64
N_EXPERTS = 32
TOP_K = 4
SWIGLU_LIMIT = 7.0
SWIGLU_ALPHA = 1.702

LANES = 128
SUBLANES = 8
VMEM_LIMIT = 56 * 1024 * 1024

TOKEN_TILE = 256
ROUTE_TILE = 512
ATTN_TILE = 256
ATTN_HEADS = 8
MOE_TILE = 512
GDN_SUPER = 4 * GDN_CHUNK
GDN_REP = GDN_V_HEADS // GDN_K_HEADS
GDN_SCAN_HEADS = 4
GDN_L_KHEADS = 8
INV_LANES = 512
NEG_BIG = -1e30
SEG_PIECES = tuple(1 << b for b in range(ROUTE_TILE.bit_length() - 1, SUBLANES.bit_length() - 2, -1))
SEG_ROWS = ROUTE_TILE * TOP_K + N_EXPERTS * SUBLANES
GAP_PIECES = tuple(1 << b for b in range(MOE_TILE.bit_length() - 2, SUBLANES.bit_length() - 2, -1))


def _cparams(*sem):
    return pltpu.CompilerParams(dimension_semantics=sem, vmem_limit_bytes=VMEM_LIMIT)


def _sigmoid(x):
    return 1.0 / (1.0 + jnp.exp(-x))


def _rms(x, w):
    return x * lax.rsqrt(jnp.mean(x * x, axis=-1, keepdims=True) + NORM_EPS) * w


def _norm_mod(x, w, shift, scale):
    return _rms(x, w) * (1.0 + scale) + shift


def _dot(a, b):
    return jnp.dot(a, b, preferred_element_type=F32)


def _dot_nt(a, b):
    return lax.dot_general(a, b, (((1,), (1,)), ((), ())), preferred_element_type=F32)


def _dot_f32(a, b):
    return jnp.dot(a, b, preferred_element_type=F32, precision=lax.Precision.HIGHEST)


def _pack_pair(lo, hi):
    ulo = lax.bitcast_convert_type(lo.astype(BF16).astype(F32), U32) >> 16
    uhi = lax.bitcast_convert_type(hi.astype(BF16).astype(F32), U32) & jnp.uint32(0xFFFF0000)
    return ulo | uhi


def _unpack_pair(p):
    lo = lax.bitcast_convert_type(p << 16, F32)
    hi = lax.bitcast_convert_type(p & jnp.uint32(0xFFFF0000), F32)
    return lo, hi


def _mod_kernel(c_ref, w_ref, b_ref, o_ref):
    c = c_ref[...]
    cond = c * _sigmoid(c)
    o_ref[0] = _dot(cond.astype(BF16), w_ref[0].astype(BF16)) + b_ref[0]


def _adaln_mod(c, ada_w, ada_b):
    depth, d, n = ada_w.shape
    b = c.shape[0]
    tn = 1024
    out = pl.pallas_call(
        _mod_kernel,
        grid=(depth, n // tn),
        in_specs=[pl.BlockSpec((b, d), lambda l, j: (0, 0)),
                  pl.BlockSpec((1, d, tn), lambda l, j: (l, 0, j)),
                  pl.BlockSpec((1, 1, tn), lambda l, j: (l, 0, j))],
        out_specs=pl.BlockSpec((1, b, tn), lambda l, j: (l, 0, j)),
        out_shape=jax.ShapeDtypeStruct((depth, b, n), F32),
        compiler_params=_cparams("arbitrary", "arbitrary"),
        name="adaln_mod",
    )(c, ada_w, ada_b.reshape(depth, 1, n))
    return out.reshape(depth, b, N_MOD, d)


def _rope_kernel(pos_ref, cos_ref, sin_ref):
    pos = pos_ref[...].astype(F32)
    lane = lax.broadcasted_iota(I32, (1, LANES), 1)
    j = (lane & (QK_ROPE_DIM // 2 - 1)).astype(F32)
    inv = jnp.exp(j * (-2.0 / QK_ROPE_DIM * math.log(ROPE_THETA)))
    ang = pos * inv
    cos_ref[...] = jnp.cos(ang)
    sin_ref[...] = jnp.sin(ang)


def _rope_tables(positions):
    t = positions.size
    tm = 1024
    spec = pl.BlockSpec((tm, LANES), lambda i: (i, 0))
    return pl.pallas_call(
        _rope_kernel,
        grid=(t // tm,),
        in_specs=[pl.BlockSpec((tm, 1), lambda i: (i, 0))],
        out_specs=[spec, spec],
        out_shape=[jax.ShapeDtypeStruct((t, LANES), F32)] * 2,
        compiler_params=_cparams("arbitrary"),
        name="rope_tables",
    )(positions.reshape(t, 1))


def _mla_proj_kernel(x_ref, mod_ref, nw_ref, win_ref, qn_ref, wqa_ref, wqb_ref, kvn_ref, wkv_ref,
                     cos_ref, sin_ref, q_ref, k_ref, v_ref):
    mod = mod_ref[0]
    h = _norm_mod(x_ref[...], nw_ref[...], mod[0:1], mod[1:2])
    lat = _dot(h.astype(BF16), win_ref[...])
    q_lat = lat[:, :Q_LORA_RANK]
    kv_lat = lat[:, Q_LORA_RANK:Q_LORA_RANK + KV_LORA_RANK]
    kp = lat[:, Q_LORA_RANK + KV_LORA_RANK:]
    cos = cos_ref[...]
    sin = sin_ref[...]
    scale = (QK_NOPE_DIM + QK_ROPE_DIM) ** -0.5

    qn = _rms(q_lat, qn_ref[...]).astype(BF16)
    qa = _dot(qn, wqa_ref[...])
    qb = _dot(qn, wqb_ref[...])
    for hh in range(MLA_HEADS):
        o = hh * 2 * LANES
        q_ref[:, o:o + LANES] = (qa[:, o:o + LANES] * scale).astype(BF16)
        pe = qa[:, o + LANES:o + 2 * LANES] * cos + qb[:, hh * LANES:(hh + 1) * LANES] * sin
        q_ref[:, o + LANES:o + 2 * LANES] = (pe * scale).astype(BF16)

    kvn = _rms(kv_lat, kvn_ref[...]).astype(BF16)
    kv = _dot(kvn, wkv_ref[...])
    lane = lax.broadcasted_iota(I32, (1, LANES), 1)
    first = lane < QK_ROPE_DIM
    u = kp * jnp.where(first, cos, sin)
    kr = jnp.where(first, u + pltpu.roll(u, QK_ROPE_DIM, 1), 0.0).astype(BF16)
    nk = MLA_HEADS * QK_NOPE_DIM
    for hh in range(MLA_HEADS):
        o = hh * 2 * LANES
        k_ref[:, o:o + LANES] = kv[:, hh * LANES:(hh + 1) * LANES].astype(BF16)
        k_ref[:, o + LANES:o + 2 * LANES] = kr
    v_ref[...] = kv[:, nk:].T.astype(BF16)


def _rotate_half_cols(w):
    half = w.shape[-1] // 2
    return jnp.concatenate([-w[..., half:], w[..., :half]], axis=-1)


def _mla_weights(w_in, w_uq, w_ukv):
    kpe = w_in[:, Q_LORA_RANK + KV_LORA_RANK:]
    w_in_ext = jnp.concatenate([w_in, _rotate_half_cols(kpe)], axis=1).astype(BF16)
    wq = w_uq.reshape(Q_LORA_RANK, MLA_HEADS, QK_NOPE_DIM + QK_ROPE_DIM)
    zeros = jnp.zeros((Q_LORA_RANK, MLA_HEADS, LANES - QK_ROPE_DIM), w_uq.dtype)
    wq_pe = wq[:, :, QK_NOPE_DIM:]
    wqa = jnp.concatenate([wq[:, :, :QK_NOPE_DIM], wq_pe, zeros], axis=2)
    wqb = jnp.concatenate([_rotate_half_cols(wq_pe), zeros], axis=2)
    wkv = w_ukv.reshape(KV_LORA_RANK, MLA_HEADS, QK_NOPE_DIM + V_HEAD_DIM)
    wkv = jnp.concatenate([wkv[:, :, :QK_NOPE_DIM].reshape(KV_LORA_RANK, -1),
                           wkv[:, :, QK_NOPE_DIM:].reshape(KV_LORA_RANK, -1)], axis=1)
    return (w_in_ext, wqa.reshape(Q_LORA_RANK, -1).astype(BF16),
            wqb.reshape(Q_LORA_RANK, -1).astype(BF16), wkv.astype(BF16))


def _mla_proj(x, mod, norm_w, weights, q_norm, kv_norm, cos, sin, seq):
    t, d = x.shape
    tm = TOKEN_TILE
    w_in_ext, wqa, wqb, wkv = weights
    per_seq = seq // tm
    full = lambda a: pl.BlockSpec(a.shape, lambda i: (0,) * a.ndim)
    row = lambda n: pl.BlockSpec((tm, n), lambda i: (i, 0))
    nq = MLA_HEADS * 2 * LANES
    nv = MLA_HEADS * V_HEAD_DIM
    args = (x, mod, norm_w.reshape(1, d), w_in_ext, q_norm.reshape(1, -1), wqa, wqb,
            kv_norm.reshape(1, -1), wkv, cos, sin)
    in_specs = [row(d), pl.BlockSpec((1, N_MOD, d), lambda i: (i // per_seq, 0, 0)), full(args[2]),
                full(w_in_ext), full(args[4]), full(wqa), full(wqb), full(args[7]), full(wkv),
                row(LANES), row(LANES)]
    return pl.pallas_call(
        _mla_proj_kernel,
        grid=(t // tm,),
        in_specs=in_specs,
        out_specs=[row(nq), row(nq), pl.BlockSpec((nv, tm), lambda i: (0, i))],
        out_shape=[jax.ShapeDtypeStruct((t, nq), BF16), jax.ShapeDtypeStruct((t, nq), BF16),
                   jax.ShapeDtypeStruct((nv, t), BF16)],
        compiler_params=_cparams("arbitrary"),
        name="mla_proj",
    )(*args)


def _attn_kernel(q_ref, k_ref, vt_ref, o_ref):
    tq = q_ref.shape[0]
    dq = 2 * LANES
    dv = V_HEAD_DIM
    qi = pl.program_id(2)
    heads = range(ATTN_HEADS)
    q = [q_ref[:, h * dq:(h + 1) * dq] for h in heads]

    def step(r0, carry, mask):
        s = [_dot_nt(k_ref[pl.ds(r0, tq), h * dq:(h + 1) * dq], q[h]) for h in heads]
        if mask is not None:
            s = [jnp.where(mask, sh, NEG_BIG) for sh in s]
        m_new = [jnp.maximum(carry[h][0], jnp.max(s[h], axis=0, keepdims=True)) for h in heads]
        p = [jnp.exp(s[h] - m_new[h]) for h in heads]
        alpha = [jnp.exp(carry[h][0] - m_new[h]) for h in heads]
        l = [alpha[h] * carry[h][1] + jnp.sum(p[h], axis=0, keepdims=True) for h in heads]
        acc = [alpha[h] * carry[h][2] + _dot(vt_ref[h * dv:(h + 1) * dv, pl.ds(r0, tq)], p[h].astype(BF16))
               for h in heads]
        return tuple((m_new[h], l[h], acc[h]) for h in heads)

    init = tuple((jnp.full((1, tq), NEG_BIG, F32), jnp.zeros((1, tq), F32), jnp.zeros((dv, tq), F32))
                 for _ in heads)
    carry = lax.fori_loop(0, qi, lambda j, c: step(pl.multiple_of(j * tq, tq), c, None), init)
    causal = (lax.broadcasted_iota(I32, (tq, tq), 0) <= lax.broadcasted_iota(I32, (tq, tq), 1))
    carry = step(pl.multiple_of(qi * tq, tq), carry, causal)
    for h in heads:
        _, l, acc = carry[h]
        o_ref[:, h * dv:(h + 1) * dv] = (acc / l).T.astype(o_ref.dtype)


def _attention(q, k, v, batch, seq):
    t = q.shape[0]
    tq = ATTN_TILE
    nq = seq // tq
    dq = ATTN_HEADS * 2 * LANES
    dv = ATTN_HEADS * V_HEAD_DIM
    return pl.pallas_call(
        _attn_kernel,
        grid=(batch, MLA_HEADS // ATTN_HEADS, nq),
        in_specs=[pl.BlockSpec((tq, dq), lambda b, h, i: (b * nq + i, h)),
                  pl.BlockSpec((seq, dq), lambda b, h, i: (b, h)),
                  pl.BlockSpec((dv, seq), lambda b, h, i: (h, b))],
        out_specs=pl.BlockSpec((tq, dv), lambda b, h, i: (b * nq + i, h)),
        out_shape=jax.ShapeDtypeStruct((t, MLA_HEADS * V_HEAD_DIM), BF16),
        compiler_params=_cparams("arbitrary", "arbitrary", "arbitrary"),
        name="mla_attention",
    )(q, k, v)


def _post_mixer_kernel(o_ref, x_ref, mod_ref, wo_ref, nw_ref, rw_ref, rb_ref,
                       x1_ref, h_ref, route_ref, tcnt_ref, trun_ref, cnt_ref, run_ref):
    tm, d = x_ref.shape
    i = pl.program_id(0)

    @pl.when(i == 0)
    def _():
        run_ref[...] = jnp.zeros_like(run_ref)

    mod = mod_ref[0]
    x1 = x_ref[...] + mod[2:3] * _dot(o_ref[...], wo_ref[...])
    x1_ref[...] = x1
    h = _norm_mod(x1, nw_ref[...], mod[3:4], mod[4:5])
    h_ref[...] = h.astype(BF16)

    h_hi = h.astype(BF16)
    h_lo = (h - h_hi.astype(F32)).astype(BF16)
    parts = _dot(h_hi, rw_ref[...]) + _dot(h_lo, rw_ref[...])
    lane = lax.broadcasted_iota(I32, (tm, LANES), 1)
    logits = jnp.where(lane < N_EXPERTS, parts + pltpu.roll(parts, LANES - N_EXPERTS, 1) + rb_ref[...], NEG_BIG)
    lane_f = lane.astype(F32)
    work = logits
    val, hot = [], []
    for _ in range(TOP_K):
        m = jnp.max(work, axis=-1, keepdims=True)
        a = jnp.min(jnp.where(work == m, lane_f, float(LANES)), axis=-1, keepdims=True)
        hot.append(lane_f == a)
        val.append(m)
        work = jnp.where(hot[-1], -jnp.inf, work)
    ex = [jnp.exp(v - val[0]) for v in val]
    den = ex[0] + ex[1] + ex[2] + ex[3]

    onehot = jnp.where(hot[0] | hot[1] | hot[2] | hot[3], 1.0, 0.0)
    row = lax.broadcasted_iota(I32, (tm, tm), 0)
    col = lax.broadcasted_iota(I32, (tm, tm), 1)
    tri = jnp.where(col < row, 1.0, 0.0).astype(BF16)
    rank = _dot(tri, onehot.astype(BF16))
    cnt = jnp.sum(onehot, axis=0, keepdims=True)
    cnt = jnp.floor((cnt + (SUBLANES - 1)) * (1.0 / SUBLANES)) * SUBLANES
    e_row = lax.broadcasted_iota(I32, (LANES, LANES), 0)
    e_col = lax.broadcasted_iota(I32, (LANES, LANES), 1)
    before = jnp.where(e_row < e_col, 1.0, 0.0)
    off = _dot_f32(jnp.broadcast_to(cnt, (SUBLANES, LANES)), before)[0:1]
    local = off + rank

    route = jnp.zeros((tm, LANES), F32)
    for kk in range(TOP_K):
        pos = jnp.sum(jnp.where(hot[kk], local, 0.0), axis=-1, keepdims=True)
        route = jnp.where(lane == kk, pos, route)
        route = jnp.where(lane == TOP_K + kk, ex[kk] / den, route)
    route_ref[...] = route
    tcnt_ref[0] = cnt.astype(I32)
    trun_ref[0] = run_ref[...].astype(I32)
    run = run_ref[...] + cnt
    run_ref[...] = run
    cnt_ref[...] = run.astype(I32)


def _post_mixer(o, x, mod, w_o, norm_w, router_w, router_b, seq):
    t, d = x.shape
    ko = o.shape[1]
    tm = ROUTE_TILE
    per_seq = seq // tm
    n_tiles = t // tm
    rw_hi = router_w.astype(BF16)
    rw_lo = (router_w - rw_hi.astype(F32)).astype(BF16)
    rw = jnp.zeros((d, LANES), BF16).at[:, :N_EXPERTS].set(rw_hi).at[:, N_EXPERTS:2 * N_EXPERTS].set(rw_lo)
    rb = jnp.zeros((1, LANES), F32).at[0, :N_EXPERTS].set(router_b)
    row = lambda n: pl.BlockSpec((tm, n), lambda i: (i, 0))
    const = lambda r, c: pl.BlockSpec((r, c), lambda i: (0, 0))
    per_tile = pl.BlockSpec((1, 1, LANES), lambda i: (i, 0, 0))
    return pl.pallas_call(
        _post_mixer_kernel,
        grid=(n_tiles,),
        in_specs=[row(ko), row(d), pl.BlockSpec((1, N_MOD, d), lambda i: (i // per_seq, 0, 0)),
                  const(ko, d), const(1, d), const(d, LANES), const(1, LANES)],
        out_specs=[row(d), row(d), row(LANES), per_tile, per_tile, const(1, LANES)],
        out_shape=[jax.ShapeDtypeStruct((t, d), F32), jax.ShapeDtypeStruct((t, d), BF16),
                   jax.ShapeDtypeStruct((t, LANES), F32),
                   jax.ShapeDtypeStruct((n_tiles, 1, LANES), I32),
                   jax.ShapeDtypeStruct((n_tiles, 1, LANES), I32),
                   jax.ShapeDtypeStruct((1, LANES), I32)],
        scratch_shapes=[pltpu.VMEM((1, LANES), F32)],
        compiler_params=_cparams("arbitrary"),
        name="post_mixer_router",
    )(o, x, mod, w_o.astype(BF16), norm_w.reshape(1, d), rw, rb)


def _tile_ceil(n):
    return (n + (MOE_TILE - 1)) // MOE_TILE * MOE_TILE


def _expert_starts(counts_ref, starts_ref):
    def body(e, acc):
        starts_ref[e] = acc
        return acc + _tile_ceil(counts_ref[e])
    return lax.fori_loop(0, N_EXPERTS, body, jnp.int32(0))


def _segment_copies(tile, tcnt_ref, trun_ref, starts_ref, visit):
    def per_expert(e, off):
        n = tcnt_ref[tile, e]
        base = starts_ref[e] + trun_ref[tile, e]
        for piece in SEG_PIECES:
            done = n & (-2 * piece)

            @pl.when((n & piece) != 0)
            def _():
                visit(pl.multiple_of(off + done, SUBLANES), pl.multiple_of(base + done, SUBLANES), piece)
        return off + n
    lax.fori_loop(0, N_EXPERTS, per_expert, jnp.int32(0))


def _dispatch_kernel(tcnt_ref, trun_ref, counts_ref, route_ref, h_ref, xs_ref, g_ref, zero_ref, starts_ref, sem):
    i = pl.program_id(0)
    n = pl.num_programs(0)
    tm, d = h_ref.shape
    slot = lax.rem(i, 2)

    def copies(tile, s, act):
        def visit(src, dst, nrows):
            act(pltpu.make_async_copy(g_ref.at[s, pl.ds(src, nrows)], xs_ref.at[pl.ds(dst, nrows)], sem.at[s]))
        _segment_copies(tile, tcnt_ref, trun_ref, starts_ref, visit)

    start = lambda c: c.start()
    wait = lambda c: c.wait()

    @pl.when(i == 0)
    def _():
        _expert_starts(counts_ref, starts_ref)

    @pl.when(i >= 2)
    def _():
        copies(i - 2, slot, wait)

    rt = route_ref[...].T
    r = lax.broadcasted_iota(I32, (SEG_ROWS, tm), 0).astype(F32)
    hit = (r == rt[0:1]) | (r == rt[1:2]) | (r == rt[2:3]) | (r == rt[3:4])
    g = _dot(jnp.where(hit, 1.0, 0.0).astype(BF16), h_ref[...])
    g_ref[slot] = _pack_pair(g[:, :d // 2], g[:, d // 2:])
    copies(i, slot, start)

    @pl.when(i == n - 1)
    def _():
        copies(i, slot, wait)

        @pl.when(i >= 1)
        def _():
            copies(i - 1, 1 - slot, wait)
        zero_ref[...] = jnp.zeros_like(zero_ref)

        def gap_fill(act):
            def per_expert(e, c):
                cnt = counts_ref[e]
                gap = _tile_ceil(cnt) - cnt
                base = starts_ref[e] + cnt
                for piece in GAP_PIECES:
                    done = gap & (-2 * piece)

                    @pl.when((gap & piece) != 0)
                    def _():
                        act(pltpu.make_async_copy(
                            zero_ref.at[pl.ds(0, piece)],
                            xs_ref.at[pl.ds(pl.multiple_of(base + done, SUBLANES), piece)], sem.at[slot]))
                return c
            lax.fori_loop(0, N_EXPERTS, per_expert, 0)

        used = starts_ref[N_EXPERTS - 1] + _tile_ceil(counts_ref[N_EXPERTS - 1])
        n_unused = (xs_ref.shape[0] - used) // MOE_TILE

        def tile_fill(act):
            def per_tile(kk, c):
                row = pl.multiple_of(used + kk * MOE_TILE, MOE_TILE)
                act(pltpu.make_async_copy(zero_ref, xs_ref.at[pl.ds(row, MOE_TILE)], sem.at[slot]))
                return c
            lax.fori_loop(0, n_unused, per_tile, 0)

        gap_fill(start)
        tile_fill(start)
        gap_fill(wait)
        tile_fill(wait)


def _sorted_rows(t):
    bound = t * TOP_K + (t // ROUTE_TILE) * N_EXPERTS * (SUBLANES - 1) + N_EXPERTS * (MOE_TILE - SUBLANES)
    return (bound + MOE_TILE - 1) // MOE_TILE * MOE_TILE


def _dispatch(h, route, tcnt, trun, counts):
    t, d = h.shape
    tm = ROUTE_TILE
    smem = pl.BlockSpec(memory_space=pltpu.SMEM)
    return pl.pallas_call(
        _dispatch_kernel,
        grid=(t // tm,),
        in_specs=[smem, smem, smem, pl.BlockSpec((tm, LANES), lambda i: (i, 0)),
                  pl.BlockSpec((tm, d), lambda i: (i, 0))],
        out_specs=pl.BlockSpec(memory_space=pl.ANY),
        out_shape=jax.ShapeDtypeStruct((_sorted_rows(t), d // 2), U32),
        scratch_shapes=[pltpu.VMEM((2, SEG_ROWS, d // 2), U32), pltpu.VMEM((MOE_TILE, d // 2), U32),
                        pltpu.SMEM((N_EXPERTS,), I32), pltpu.SemaphoreType.DMA((2,))],
        compiler_params=_cparams("arbitrary"),
        name="moe_dispatch",
    )(tcnt, trun, counts, route, h)


def _moe_schedule(counts, n_tiles, tile):
    ids = jnp.arange(N_EXPERTS, dtype=I32)
    n_own = (counts + tile - 1) // tile
    t_end = jnp.cumsum(n_own)
    t_start = t_end - n_own
    total = t_end[-1]
    later = jnp.where((ids[None, :] > ids[:, None]) & (counts[None, :] > 0), ids[None, :], N_EXPERTS)
    next_e = jnp.min(later, axis=1)
    next_e = jnp.where(next_e == N_EXPERTS, -1, next_e)
    step = jnp.arange(n_tiles, dtype=I32)
    live = step < total
    v = jnp.minimum(step, total - 1)
    e = jnp.minimum(jnp.sum(t_end[None, :] <= v[:, None], axis=1), N_EXPERTS - 1).astype(I32)
    pick = e[:, None] == ids[None, :]
    of_e = lambda a: jnp.sum(jnp.where(pick, a[None, :], 0), axis=1)
    new_expert = (live & (step == of_e(t_start))).astype(I32)
    return e, new_expert, live.astype(I32), of_e(next_e).astype(I32)


def _moe_kernel(exp_ref, newe_ref, live_ref, next_ref,
                x_ref, wgu_hbm, bgu_ref, wd_hbm, bd_ref, y_ref,
                wgu_f32, wd_f32, wgu_bf, wd_bf, sem, *, layer):
    v = pl.program_id(0)
    f = wd_bf.shape[0]
    half = x_ref.shape[1]

    def weight_copies(e):
        return (pltpu.make_async_copy(wgu_hbm.at[layer, e], wgu_f32, sem.at[0]),
                pltpu.make_async_copy(wd_hbm.at[layer, e], wd_f32, sem.at[1]))

    @pl.when(newe_ref[v] == 1)
    def _():
        e = exp_ref[v]

        @pl.when(v == 0)
        def _():
            for c in weight_copies(e):
                c.start()
        for c in weight_copies(e):
            c.wait()
        wgu_bf[...] = wgu_f32[...].astype(BF16)
        wd_bf[...] = wd_f32[...].astype(BF16)
        nxt = next_ref[v]

        @pl.when(nxt >= 0)
        def _():
            for c in weight_copies(nxt):
                c.start()

    @pl.when(live_ref[v] == 1)
    def _():
        x_lo, x_hi = _unpack_pair(x_ref[...])
        x = jnp.concatenate([x_lo.astype(BF16), x_hi.astype(BF16)], axis=1)
        gu = _dot(x, wgu_bf[...]) + bgu_ref[0]
        gate = jnp.minimum(gu[:, :f], SWIGLU_LIMIT)
        lin = jnp.clip(gu[:, f:], -SWIGLU_LIMIT, SWIGLU_LIMIT)
        act = gate * _sigmoid(SWIGLU_ALPHA * gate) * (lin + 1.0)
        y = _dot(act.astype(BF16), wd_bf[...]) + bd_ref[0]
        y_ref[...] = _pack_pair(y[:, :half], y[:, half:])

    @pl.when(live_ref[v] == 0)
    def _():
        y_ref[...] = jnp.zeros_like(y_ref)


def _moe_experts(xs, sched, w_gate_up, b_gate_up, w_down, b_down, layer):
    a, half = xs.shape
    depth, e, d, f2 = w_gate_up.shape
    f = f2 // 2
    tile = MOE_TILE
    grid_spec = pltpu.PrefetchScalarGridSpec(
        num_scalar_prefetch=4,
        grid=(a // tile,),
        in_specs=[pl.BlockSpec((tile, half), lambda v, ex, *_: (v, 0)),
                  pl.BlockSpec(memory_space=pl.ANY),
                  pl.BlockSpec((None, 1, 1, f2), lambda v, ex, *_: (layer, ex[v], 0, 0)),
                  pl.BlockSpec(memory_space=pl.ANY),
                  pl.BlockSpec((None, 1, 1, d), lambda v, ex, *_: (layer, ex[v], 0, 0))],
        out_specs=pl.BlockSpec((tile, half), lambda v, ex, *_: (v, 0)),
        scratch_shapes=[pltpu.VMEM((d, f2), F32), pltpu.VMEM((f, d), F32),
                        pltpu.VMEM((d, f2), BF16), pltpu.VMEM((f, d), BF16),
                        pltpu.SemaphoreType.DMA((2,))],
    )
    return pl.pallas_call(
        functools.partial(_moe_kernel, layer=layer),
        grid_spec=grid_spec,
        out_shape=jax.ShapeDtypeStruct((a, half), U32),
        compiler_params=_cparams("arbitrary"),
        name="moe_experts",
    )(*sched, xs, w_gate_up, b_gate_up.reshape(depth, e, 1, f2), w_down, b_down.reshape(depth, e, 1, d))


def _combine_kernel(tcnt_ref, trun_ref, counts_ref, ys_ref, route_ref, x_ref, mod_ref, fw_ref,
                    o_ref, y_buf, starts_ref, sem, *, final_norm):
    i = pl.program_id(0)
    n = pl.num_programs(0)
    tm, d = x_ref.shape
    slot = lax.rem(i, 2)

    def copies(tile, s, act):
        def visit(dst, src, nrows):
            act(pltpu.make_async_copy(ys_ref.at[pl.ds(src, nrows)], y_buf.at[s, pl.ds(dst, nrows)], sem.at[s]))
        _segment_copies(tile, tcnt_ref, trun_ref, starts_ref, visit)

    @pl.when(i == 0)
    def _():
        _expert_starts(counts_ref, starts_ref)
        y_buf[...] = jnp.zeros_like(y_buf)
        copies(0, 0, lambda c: c.start())

    @pl.when(i + 1 < n)
    def _():
        copies(i + 1, 1 - slot, lambda c: c.start())

    copies(i, slot, lambda c: c.wait())
    y_lo, y_hi = _unpack_pair(y_buf[slot])
    route = route_ref[...]
    r = lax.broadcasted_iota(I32, (tm, SEG_ROWS), 1).astype(F32)
    w = jnp.zeros((tm, SEG_ROWS), F32)
    for kk in range(TOP_K):
        w = jnp.where(r == route[:, kk:kk + 1], route[:, TOP_K + kk:TOP_K + kk + 1], w)
    w = w.astype(BF16)
    y = jnp.concatenate([_dot(w, y_lo.astype(BF16)), _dot(w, y_hi.astype(BF16))], axis=1)
    out = x_ref[...] + mod_ref[0][5:6] * y
    if final_norm:
        out = _rms(out, fw_ref[...])
    o_ref[...] = out


def _combine(ys, route, tcnt, trun, counts, x1, mod, final_w, seq, final_norm):
    t, d = x1.shape
    tm = ROUTE_TILE
    per_seq = seq // tm
    smem = pl.BlockSpec(memory_space=pltpu.SMEM)
    row = lambda n: pl.BlockSpec((tm, n), lambda i: (i, 0))
    return pl.pallas_call(
        functools.partial(_combine_kernel, final_norm=final_norm),
        grid=(t // tm,),
        in_specs=[smem, smem, smem, pl.BlockSpec(memory_space=pl.ANY), row(LANES), row(d),
                  pl.BlockSpec((1, N_MOD, d), lambda i: (i // per_seq, 0, 0)),
                  pl.BlockSpec((1, d), lambda i: (0, 0))],
        out_specs=row(d),
        out_shape=jax.ShapeDtypeStruct((t, d), F32),
        scratch_shapes=[pltpu.VMEM((2, SEG_ROWS, d // 2), U32), pltpu.SMEM((N_EXPERTS,), I32),
                        pltpu.SemaphoreType.DMA((2,))],
        compiler_params=_cparams("arbitrary"),
        name="moe_combine",
    )(tcnt, trun, counts, ys, route, x1, mod, final_w.reshape(1, d))


def _moe_block(o, x, mod, w_o, norm_w, router_w, router_b, w_gate_up, b_gate_up, w_down, b_down,
               final_w, seq, layer, final_norm):
    x1, h, route, tcnt, trun, cnt = _post_mixer(o, x, mod, w_o, norm_w, router_w, router_b, seq)
    n_tiles = tcnt.shape[0]
    tcnt = tcnt.reshape(n_tiles, LANES)
    trun = trun.reshape(n_tiles, LANES)
    counts = cnt[0, :N_EXPERTS]
    xs = _dispatch(h, route, tcnt, trun, counts)
    sched = _moe_schedule(counts, xs.shape[0] // MOE_TILE, MOE_TILE)
    ys = _moe_experts(xs, sched, w_gate_up, b_gate_up, w_down, b_down, layer)
    return _combine(ys, route, tcnt, trun, counts, x1, mod, final_w, seq, final_norm)


def _gdn_conv_kernel(x_ref, mod_ref, nw_ref, win_ref, cw_ref, q_ref, k_ref, v_ref, buf_ref, *, per_seq):
    tm = x_ref.shape[0]
    nqk = q_ref.shape[1]
    nconv = win_ref.shape[1]
    i = pl.program_id(0)
    mod = mod_ref[0]
    h = _norm_mod(x_ref[...], nw_ref[...], mod[0:1], mod[1:2])

    @pl.when(i % per_seq == 0)
    def _():
        buf_ref[0:SUBLANES, :] = jnp.zeros((SUBLANES, nconv), F32)

    buf_ref[SUBLANES:SUBLANES + tm, :] = _dot(h.astype(BF16), win_ref[...])
    cw = cw_ref[...]
    window = buf_ref[0:SUBLANES + tm, :]
    acc = cw[GDN_CONV - 1:GDN_CONV, :] * window[SUBLANES:, :]
    for back in range(1, GDN_CONV):
        acc = acc + cw[GDN_CONV - 1 - back:GDN_CONV - back, :] * pltpu.roll(window, back, 0)[SUBLANES:, :]
    buf_ref[0:SUBLANES, :] = buf_ref[tm:tm + SUBLANES, :]
    y = acc * _sigmoid(acc)

    def l2(a):
        return a * lax.rsqrt(jnp.sum(a * a, axis=-1, keepdims=True) + 1e-6)

    for hh in range(nqk // GDN_HEAD_DIM):
        s = slice(hh * GDN_HEAD_DIM, (hh + 1) * GDN_HEAD_DIM)
        q_ref[:, s] = (l2(y[:, s]) * GDN_HEAD_DIM ** -0.5).astype(BF16)
        k_ref[:, s] = l2(y[:, nqk + hh * GDN_HEAD_DIM:nqk + (hh + 1) * GDN_HEAD_DIM]).astype(BF16)
    v_ref[...] = y[:, 2 * nqk:].astype(BF16)


def _gdn_gate_kernel(x_ref, mod_ref, nw_ref, win_ref, alog_ref, dtb_ref, z_ref, bg_ref):
    nv = z_ref.shape[1]
    mod = mod_ref[0]
    h = _norm_mod(x_ref[...], nw_ref[...], mod[0:1], mod[1:2])
    pr = _dot(h.astype(BF16), win_ref[...])
    z_ref[...] = pr[:, :nv].astype(BF16)
    ba = pr[:, nv:]
    sp = ba + dtb_ref[...]
    softplus = jnp.maximum(sp, 0.0) + jnp.log(1.0 + jnp.exp(-jnp.abs(sp)))
    lane = lax.broadcasted_iota(I32, (1, LANES), 1)
    log_decay = -jnp.exp(alog_ref[...]) * softplus
    bg_ref[...] = jnp.where(lane < GDN_V_HEADS, _sigmoid(ba), _chunk_cumsum(log_decay))


def _gdn_proj(x, mod, norm_w, w_in, conv_w, a_log, dt_bias, seq):
    t, d = x.shape
    tm = TOKEN_TILE
    per_seq = seq // tm
    nqk = GDN_K_HEADS * GDN_HEAD_DIM
    nv = GDN_V_HEADS * GDN_HEAD_DIM
    nconv = 2 * nqk + nv
    w_conv = w_in[:, :nconv].astype(BF16)
    w_gate = jnp.zeros((d, nv + LANES), BF16).at[:, :w_in.shape[1] - nconv].set(w_in[:, nconv:].astype(BF16))
    pad = jnp.zeros((1, LANES), F32)
    alog = pad.at[0, GDN_V_HEADS:2 * GDN_V_HEADS].set(a_log)
    dtb = pad.at[0, GDN_V_HEADS:2 * GDN_V_HEADS].set(dt_bias)
    row = lambda n: pl.BlockSpec((tm, n), lambda i: (i, 0))
    const = lambda a: pl.BlockSpec(a.shape, lambda i: (0,) * a.ndim)
    modspec = pl.BlockSpec((1, N_MOD, d), lambda i: (i // per_seq, 0, 0))
    nw = norm_w.reshape(1, d)
    q, k, v = pl.pallas_call(
        functools.partial(_gdn_conv_kernel, per_seq=per_seq),
        grid=(t // tm,),
        in_specs=[row(d), modspec, const(nw), const(w_conv), const(conv_w)],
        out_specs=[row(nqk), row(nqk), row(nv)],
        out_shape=[jax.ShapeDtypeStruct((t, nqk), BF16), jax.ShapeDtypeStruct((t, nqk), BF16),
                   jax.ShapeDtypeStruct((t, nv), BF16)],
        scratch_shapes=[pltpu.VMEM((tm + 2 * SUBLANES, nconv), F32)],
        compiler_params=_cparams("arbitrary"),
        name="gdn_conv_proj",
    )(x, mod, nw, w_conv, conv_w)
    z, bg = pl.pallas_call(
        _gdn_gate_kernel,
        grid=(t // tm,),
        in_specs=[row(d), modspec, const(nw), const(w_gate), const(alog), const(dtb)],
        out_specs=[row(nv), row(LANES)],
        out_shape=[jax.ShapeDtypeStruct((t, nv), BF16), jax.ShapeDtypeStruct((t, LANES), F32)],
        compiler_params=_cparams("arbitrary"),
        name="gdn_gate_proj",
    )(x, mod, nw, w_gate, alog, dtb)
    return q, k, v, z, bg


def _head_col(block, lane_idx):
    lane = lax.broadcasted_iota(I32, (1, LANES), 1)
    col = jnp.sum(jnp.where(lane == lane_idx, block, 0.0), axis=-1, keepdims=True)
    return jnp.broadcast_to(col, block.shape)


def _chunk_cumsum(x):
    rin = lax.broadcasted_iota(I32, x.shape, 0) & (GDN_CHUNK - 1)
    s = 1
    while s < GDN_CHUNK:
        x = x + jnp.where(rin >= s, pltpu.roll(x, s, 0), 0.0)
        s *= 2
    return x


def _diag_blocks(m):
    c = GDN_CHUNK
    return jnp.concatenate([m[b * c:(b + 1) * c, b * c:(b + 1) * c] for b in range(m.shape[0] // c)], axis=0)


def _chunk_decay(gc, g_row):
    n = gc.shape[0]
    c = GDN_CHUNK
    gj = jnp.concatenate([jnp.broadcast_to(g_row[:, b * c:(b + 1) * c], (c, c)) for b in range(n // c)], axis=0)
    i_in = lax.broadcasted_iota(I32, (n, c), 0) & (c - 1)
    col = lax.broadcasted_iota(I32, (n, c), 1)
    lower = col <= i_in
    decay = jnp.where(lower, jnp.exp(jnp.where(lower, gc[:, :c] - gj, 0.0)), 0.0)
    return decay, col < i_in


def _gdn_l_kernel(k_ref, bg_ref, l_ref, bgt_ref):
    grp = pl.program_id(1)
    sc = GDN_SUPER
    c = GDN_CHUNK
    hd = GDN_HEAD_DIM
    vheads = range(GDN_L_KHEADS * GDN_REP)

    def body(s, carry):
        r0 = pl.multiple_of(s * sc, sc)
        bg = bg_ref[pl.ds(r0, sc), :]
        kk = []
        for kh in range(GDN_L_KHEADS):
            k = k_ref[pl.ds(r0, sc), kh * hd:(kh + 1) * hd]
            kk.append(_diag_blocks(_dot_nt(k, k)))
        first = grp * GDN_L_KHEADS * GDN_REP
        bgt_ref[...] = bg.T
        beta = [_head_col(bg, first + j) for j in vheads]
        gc = [_head_col(bg, GDN_V_HEADS + first + j) for j in vheads]
        dec = [_chunk_decay(gc[j], bgt_ref[pl.ds(GDN_V_HEADS + first + j, 1), :]) for j in vheads]
        for j in vheads:
            decay, strict = dec[j]
            l_ref[0, j, pl.ds(r0, sc), :] = jnp.where(strict, kk[j // GDN_REP] * beta[j][:, :c] * decay, 0.0)
        return carry
    lax.fori_loop(0, k_ref.shape[0] // sc, body, 0)


def _gdn_l(k, bg, batch, seq):
    nk = GDN_L_KHEADS
    return pl.pallas_call(
        _gdn_l_kernel,
        grid=(batch, GDN_K_HEADS // nk),
        in_specs=[pl.BlockSpec((seq, nk * GDN_HEAD_DIM), lambda b, h: (b, h)),
                  pl.BlockSpec((seq, LANES), lambda b, h: (b, 0))],
        out_specs=pl.BlockSpec((1, nk * GDN_REP, seq, GDN_CHUNK), lambda b, h: (b, h, 0, 0)),
        out_shape=jax.ShapeDtypeStruct((batch, GDN_V_HEADS, seq, GDN_CHUNK), F32),
        scratch_shapes=[pltpu.VMEM((LANES, GDN_SUPER), F32)],
        compiler_params=_cparams("arbitrary", "arbitrary"),
        name="gdn_chunk_l",
    )(k, bg)


def _tri_inv_kernel(a_ref, x_ref):
    c, n = a_ref.shape[1], a_ref.shape[2]
    sub_row = lax.broadcasted_iota(I32, (SUBLANES, n), 0)
    for c0 in range(0, c, SUBLANES):
        cols = slice(c0, c0 + SUBLANES)
        for i in range(c):
            if i < c0:
                x_ref[i, cols, :] = jnp.zeros((SUBLANES, n), F32)
                continue

            def sub(j, acc, i=i, cols=cols):
                return acc - a_ref[i, pl.ds(j, 1), :] * x_ref[j, cols, :]
            row = jnp.where(sub_row == i - c0, 1.0, 0.0)
            if i > c0:
                row = lax.fori_loop(c0, i, sub, row, unroll=min(i - c0, 8))
            x_ref[i, cols, :] = row


def _tri_inv(a):
    c, _, n = a.shape
    lanes = min(INV_LANES, n)
    spec = pl.BlockSpec((c, c, lanes), lambda i: (0, 0, i))
    return pl.pallas_call(
        _tri_inv_kernel,
        grid=(n // lanes,),
        in_specs=[spec],
        out_specs=spec,
        out_shape=jax.ShapeDtypeStruct(a.shape, F32),
        compiler_params=_cparams("arbitrary"),
        name="gdn_tri_inv",
    )(a)


def _gdn_scan_kernel(q_ref, k_ref, v_ref, z_ref, bg_ref, ti_ref, nw_ref, o_ref):
    grp = pl.program_id(1)
    sc = GDN_SUPER
    nc = sc // GDN_CHUNK
    hd = GDN_HEAD_DIM
    n_super = q_ref.shape[0] // sc
    nw = nw_ref[...]
    r_blk = lax.broadcasted_iota(I32, (sc, sc), 0) // GDN_CHUNK
    c_blk = lax.broadcasted_iota(I32, (sc, sc), 1) // GDN_CHUNK
    same = r_blk == c_blk

    heads = range(GDN_SCAN_HEADS)

    def prepare_head(s, j, shared):
        r0 = pl.multiple_of(s * sc, sc)
        kh = j // GDN_REP
        if kh not in shared:
            qb = q_ref[pl.ds(r0, sc), kh * hd:(kh + 1) * hd]
            kb = k_ref[pl.ds(r0, sc), kh * hd:(kh + 1) * hd]
            shared[kh] = (qb.astype(F32), kb.astype(F32), _diag_blocks(_dot_nt(qb, kb)))
        q, k, qk = shared[kh]
        head = grp * GDN_SCAN_HEADS + j
        bg = bg_ref[pl.ds(r0, sc), :]
        v = v_ref[pl.ds(r0, sc), j * hd:(j + 1) * hd].astype(F32)
        beta = _head_col(bg, head)
        gc = _head_col(bg, GDN_V_HEADS + head)
        egc = jnp.exp(gc)
        decay, _ = _chunk_decay(gc, gc.T[0:1, :])
        a_intra = (qk * decay).astype(BF16)
        ti = ti_ref[0, j, pl.ds(r0, sc), :]
        t_inv = jnp.where(same, jnp.concatenate([ti] * nc, axis=1), 0.0).astype(BF16)
        uw = _dot(t_inv, jnp.concatenate([v * beta, k * beta * egc], axis=1).astype(BF16))
        g_last = [gc[(c + 1) * GDN_CHUNK - 1:(c + 1) * GDN_CHUNK, :] for c in range(nc)]
        gl_rows = jnp.concatenate([jnp.broadcast_to(g, (GDN_CHUNK, LANES)) for g in g_last], axis=0)
        k_dec_t = (k * jnp.exp(gl_rows - gc)).T.astype(BF16)
        eg_last = jnp.concatenate([jnp.exp(g) for g in g_last], axis=0)
        return uw[:, :hd], uw[:, hd:].astype(BF16), (q * egc).astype(BF16), k_dec_t, a_intra, eg_last

    def prepare(s):
        shared = {}
        return tuple(prepare_head(s, j, shared) for j in heads)

    def body(s, carry):
        states, prepared = carry
        states = list(states)
        r0 = pl.multiple_of(s * sc, sc)
        nxt = jnp.minimum(s + 1, n_super - 1)
        shared = {}
        following = []
        outs = [[] for _ in heads]
        for c in range(nc):
            rs = slice(c * GDN_CHUNK, (c + 1) * GDN_CHUNK)
            sb = [states[j].astype(BF16) for j in heads]
            v_new = [prepared[j][0][rs] - _dot(prepared[j][1][rs], sb[j]) for j in heads]
            vb = [v.astype(BF16) for v in v_new]
            for j in heads:
                outs[j].append(_dot(prepared[j][2][rs], sb[j]) + _dot(prepared[j][4][rs], vb[j]))
            states = [states[j] * prepared[j][5][c:c + 1, :] + _dot(prepared[j][3][:, rs], vb[j]) for j in heads]
            following.append(prepare_head(nxt, c, shared))
        for j in heads:
            o = jnp.concatenate(outs[j], axis=0)
            z = z_ref[pl.ds(r0, sc), j * hd:(j + 1) * hd].astype(F32)
            o_ref[pl.ds(r0, sc), j * hd:(j + 1) * hd] = (_rms(o, nw) * (z * _sigmoid(z))).astype(o_ref.dtype)
        return tuple(states), tuple(following)

    assert GDN_SCAN_HEADS == nc
    init = tuple(jnp.zeros((hd, hd), F32) for _ in heads)
    lax.fori_loop(0, n_super, body, (init, prepare(0)))


def _gdn_scan(q, k, v, z, bg, t_inv, norm_w, batch, seq):
    hd = GDN_HEAD_DIM
    nh = GDN_SCAN_HEADS
    kspec = pl.BlockSpec((seq, nh // GDN_REP * hd), lambda b, g: (b, g))
    vspec = pl.BlockSpec((seq, nh * hd), lambda b, g: (b, g))
    return pl.pallas_call(
        _gdn_scan_kernel,
        grid=(batch, GDN_V_HEADS // nh),
        in_specs=[kspec, kspec, vspec, vspec, pl.BlockSpec((seq, LANES), lambda b, g: (b, 0)),
                  pl.BlockSpec((1, nh, seq, GDN_CHUNK), lambda b, g: (b, g, 0, 0)),
                  pl.BlockSpec((1, hd), lambda b, g: (0, 0))],
        out_specs=vspec,
        out_shape=jax.ShapeDtypeStruct((batch * seq, GDN_V_HEADS * hd), BF16),
        compiler_params=_cparams("arbitrary", "arbitrary"),
        name="gdn_scan",
    )(q, k, v, z, bg, t_inv, norm_w.reshape(1, hd))


def _gdn_mixer(x, mod, norm_w, w_in, conv_w, a_log, dt_bias, gnorm_w, batch, seq):
    q, k, v, z, bg = _gdn_proj(x, mod, norm_w, w_in, conv_w, a_log, dt_bias, seq)
    c = GDN_CHUNK
    l_c = _gdn_l(k, bg, batch, seq)
    n_chunks = batch * GDN_V_HEADS * (seq // c)
    a = l_c.reshape(n_chunks, c, c).transpose(1, 2, 0)
    t_inv = _tri_inv(a).transpose(2, 0, 1).reshape(batch, GDN_V_HEADS, seq, c)
    return _gdn_scan(q, k, v, z, bg, t_inv, gnorm_w, batch, seq)


def kernel(x, c, positions, norm_mix, norm_ffn, ada_w, ada_b, mla_w_in, mla_q_norm, mla_w_uq,
           mla_kv_norm, mla_w_ukv, mla_w_o, gdn_w_in, gdn_conv_w, gdn_a_log, gdn_dt_bias, gdn_norm_w,
           gdn_w_o, router_w, router_b, moe_w_gate_up, moe_b_gate_up, moe_w_down, moe_b_down, final_norm):
    batch, seq, d = x.shape
    depth = ada_w.shape[0]
    xt = x.reshape(batch * seq, d)
    mod = _adaln_mod(c, ada_w, ada_b)
    cos, sin = _rope_tables(positions)
    for layer in range(depth):
        j = layer // 2
        if layer % 2 == 0:
            weights = _mla_weights(mla_w_in[j], mla_w_uq[j], mla_w_ukv[j])
            q, k, v = _mla_proj(xt, mod[layer], norm_mix[layer], weights, mla_q_norm[j],
                                mla_kv_norm[j], cos, sin, seq)
            o = _attention(q, k, v, batch, seq)
            w_o = mla_w_o[j]
        else:
            o = _gdn_mixer(xt, mod[layer], norm_mix[layer], gdn_w_in[j], gdn_conv_w[j], gdn_a_log[j],
                           gdn_dt_bias[j], gdn_norm_w[j], batch, seq)
            w_o = gdn_w_o[j]
        xt = _moe_block(o, xt, mod[layer], w_o, norm_ffn[layer], router_w[layer], router_b[layer],
                        moe_w_gate_up, moe_b_gate_up, moe_w_down, moe_b_down, final_norm, seq, layer,
                        layer == depth - 1)
    return xt.reshape(batch, seq, d)
```
